```python
import math
import jax, jax.numpy as jnp
from jax import lax
import numpy as np

D_MODEL = 1024
BATCH = 8
SEQ = 2048
DEPTH = 4
DEC_BATCH = 32
DEC_SEQ = 1
PAST_LEN = 16384
PAGE_SIZE = 128

GDN_HEADS = 4
GDN_DK = 128
GDN_DV = 128
GDN_CONV = 4
GDN_CHUNK = 64
GDN_KEY_W = GDN_HEADS * GDN_DK
GDN_VAL_W = GDN_HEADS * GDN_DV
GDN_CONV_CH = 2 * GDN_KEY_W + GDN_VAL_W
RWKV_HEADS = 8
RWKV_HEAD = 64
RWKV_W = RWKV_HEADS * RWKV_HEAD
RWKV_DECAY_LORA = 64
RWKV_AAA_LORA = 64
RWKV_GATE_LORA = 128
RWKV_COLS = 3 * RWKV_W + RWKV_DECAY_LORA + RWKV_AAA_LORA + RWKV_GATE_LORA
RWKV_LN_EPS = 64e-5
MLA_HEADS = 4
MLA_Q_RANK = 256
MLA_KV_RANK = 256
MLA_NOPE = 128
MLA_ROPE = 64
MLA_V = 128
MLA_SCALE = 1.0 / math.sqrt(MLA_NOPE + MLA_ROPE)
ROPE_THETA = 10000.0
Q_BLOCK = 128
IN_COLS = GDN_CONV_CH + 2 * GDN_HEADS + GDN_VAL_W + RWKV_COLS + MLA_Q_RANK + MLA_KV_RANK + MLA_ROPE + 3 * D_MODEL
D_FF = ((8 * D_MODEL // 3 + 255) // 256) * 256
NORM_EPS = 1e-6

kernel_name = 'hybrid_gdn_rwkv7_mla_step'

F32 = jnp.float32


def split_cols(a, sizes):
    idx, acc = [], 0
    for s in sizes[:-1]:
        acc += s
        idx.append(acc)
    return jnp.split(a, idx, axis=-1)


def rms_norm(x, g, eps=NORM_EPS):
    xf = x.astype(F32)
    y = xf * lax.rsqrt(jnp.mean(xf * xf, axis=-1, keepdims=True) + eps)
    return (y * g.astype(F32)).astype(x.dtype)


def l2norm(x, eps=1e-6):
    xf = x.astype(F32)
    return (xf * lax.rsqrt(jnp.sum(xf * xf, axis=-1, keepdims=True) + eps)).astype(x.dtype)


def head_group_norm(x, g, b, eps=RWKV_LN_EPS):
    B, T, H, N = x.shape
    xf = x.astype(F32)
    mu = jnp.mean(xf, axis=-1, keepdims=True)
    var = jnp.mean(jnp.square(xf - mu), axis=-1, keepdims=True)
    y = ((xf - mu) * lax.rsqrt(var + eps)).reshape(B, T, H * N)
    return y * g + b


def rope(x, pos):
    half = MLA_ROPE // 2
    freq = ROPE_THETA ** (-jnp.arange(half, dtype=F32) / half)
    ang = pos.astype(F32)[:, None] * freq
    shape = (1, ang.shape[0]) + (1,) * (x.ndim - 3) + (half,)
    cos = jnp.cos(ang).reshape(shape)
    sin = jnp.sin(ang).reshape(shape)
    x1, x2 = x[..., :half], x[..., half:]
    return jnp.concatenate([x1 * cos - x2 * sin, x2 * cos + x1 * sin], axis=-1).astype(x.dtype)


def causal_short_conv(x, hist, w):
    T = x.shape[1]
    full = jnp.concatenate([hist, x], axis=1)
    y = full[:, 0:T] * w[0]
    for i in range(1, GDN_CONV):
        y = y + full[:, i:i + T] * w[i]
    return jax.nn.silu(y), full[:, T:]


def gdn_chunked(q, k, v, log_a, beta, s0):
    B, T, H, DK = q.shape
    C = GDN_CHUNK
    N = T // C

    def blocks(a):
        a = a.reshape((B, N, C, H) + a.shape[3:])
        return jnp.moveaxis(jnp.moveaxis(a, 1, 0), 3, 2)

    qb = blocks(q * (DK ** -0.5))
    kb = blocks(k)
    vb = blocks(v)
    bb = blocks(beta)
    gb = jnp.cumsum(blocks(log_a.astype(F32)), axis=-1)
    incl = jnp.tril(jnp.ones((C, C), dtype=bool))
    strict = jnp.tril(jnp.ones((C, C), dtype=bool), -1)
    diff = gb[..., :, None] - gb[..., None, :]
    decay = jnp.where(incl, jnp.exp(jnp.where(incl, diff, 0.0)), 0.0)
    k_beta = kb * bb[..., None]
    lmat = jnp.where(strict, jnp.einsum('nbhcd,nbhsd->nbhcs', k_beta, kb) * decay, 0.0).astype(F32)
    eye = jnp.eye(C, dtype=F32)
    tmat = lax.linalg.triangular_solve(eye + lmat, jnp.broadcast_to(eye, lmat.shape),
                                       left_side=True, lower=True)
    u = jnp.einsum('nbhcs,nbhse->nbhce', tmat, vb * bb[..., None])
    w = jnp.einsum('nbhcs,nbhsd->nbhcd', tmat, k_beta * jnp.exp(gb)[..., None])
    qk = jnp.where(incl, jnp.einsum('nbhcd,nbhsd->nbhcs', qb, kb) * decay, 0.0)

    def step(s, xs):
        q_n, k_n, u_n, w_n, qk_n, g_n = xs
        v_new = u_n - jnp.einsum('bhcd,bhde->bhce', w_n, s)
        o_n = (jnp.einsum('bhcd,bhde->bhce', q_n * jnp.exp(g_n)[..., None], s)
               + jnp.einsum('bhcs,bhse->bhce', qk_n, v_new))
        g_last = g_n[..., -1]
        s = (s * jnp.exp(g_last)[..., None, None]
             + jnp.einsum('bhcd,bhce->bhde', k_n * jnp.exp(g_last[..., None] - g_n)[..., None], v_new))
        return s, o_n

    s_fin, o = lax.scan(step, s0.astype(F32), (qb, kb, u, w, qk, gb))
    o = jnp.swapaxes(jnp.moveaxis(o, 0, 1), 2, 3).reshape(B, T, H, -1)
    return o, s_fin


def gdn_recurrent(q, k, v, log_a, beta, s0):
    q = q * (q.shape[-1] ** -0.5)

    def step(s, xs):
        q_t, k_t, v_t, a_t, b_t = xs
        s = s * jnp.exp(a_t)[..., None, None]
        u = b_t[..., None] * (v_t - jnp.einsum('bhd,bhde->bhe', k_t, s))
        s = s + k_t[..., :, None] * u[..., None, :]
        return s, jnp.einsum('bhd,bhde->bhe', q_t, s)

    xs = tuple(jnp.moveaxis(a, 1, 0) for a in (q, k, v, log_a.astype(F32), beta))
    s_fin, o = lax.scan(step, s0.astype(F32), xs)
    return jnp.moveaxis(o, 0, 1), s_fin


def rwkv7_scan(r, d, k, v, kk, a, s0):
    def step(s, xs):
        r_t, d_t, k_t, v_t, kk_t, a_t = xs
        sa = jnp.einsum('bhvk,bhk->bhv', s, -kk_t)
        s = (s * d_t[:, :, None, :] + sa[..., None] * (kk_t * a_t)[:, :, None, :]
             + v_t[..., None] * k_t[:, :, None, :])
        return s, jnp.einsum('bhvk,bhk->bhv', s, r_t)

    xs = tuple(jnp.moveaxis(t, 1, 0) for t in (r, d, k, v, kk, a))
    s_fin, y = lax.scan(step, s0.astype(F32), xs)
    return jnp.moveaxis(y, 0, 1), s_fin


def mla_prompt(q_nope, q_rope, ckv, kr, w_uk, w_uv):
    B, S = ckv.shape[:2]
    k_nope = jnp.einsum('bsr,rhd->bshd', ckv, w_uk)
    v = jnp.einsum('bsr,rhd->bshd', ckv, w_uv)
    nb = S // Q_BLOCK
    key_pos = jnp.arange(S)

    def block(xs):
        qn, qr, start = xs
        s = (jnp.einsum('bqhd,bshd->bhqs', qn, k_nope)
             + jnp.einsum('bqhd,bsd->bhqs', qr, kr)).astype(F32) * MLA_SCALE
        qpos = start + jnp.arange(Q_BLOCK)
        s = jnp.where(key_pos[None, :] <= qpos[:, None], s, -jnp.inf)
        p = jax.nn.softmax(s, axis=-1).astype(v.dtype)
        return jnp.einsum('bhqs,bshd->bqhd', p, v)

    qn_b = jnp.moveaxis(q_nope.reshape(B, nb, Q_BLOCK, MLA_HEADS, MLA_NOPE), 1, 0)
    qr_b = jnp.moveaxis(q_rope.reshape(B, nb, Q_BLOCK, MLA_HEADS, MLA_ROPE), 1, 0)
    o = lax.map(block, (qn_b, qr_b, jnp.arange(nb) * Q_BLOCK))
    return jnp.moveaxis(o, 0, 1).reshape(B, S, MLA_HEADS * MLA_V)


def mla_sample(q_nope, q_rope, ckv_new, kr_new, ckv_past, kr_past, w_uk, w_uv):
    B, T = ckv_new.shape[:2]
    P = ckv_past.shape[1]
    q_lat = jnp.einsum('bthd,rhd->bthr', q_nope, w_uk)
    s_past = (jnp.einsum('bthr,bsr->bhts', q_lat, ckv_past)
              + jnp.einsum('bthd,bsd->bhts', q_rope, kr_past)).astype(F32) * MLA_SCALE
    s_new = (jnp.einsum('bthr,bsr->bhts', q_lat, ckv_new)
             + jnp.einsum('bthd,bsd->bhts', q_rope, kr_new)).astype(F32) * MLA_SCALE
    s_new = jnp.where(jnp.tril(jnp.ones((T, T), dtype=bool)), s_new, -jnp.inf)
    p = jax.nn.softmax(jnp.concatenate([s_past, s_new], axis=-1), axis=-1).astype(ckv_new.dtype)
    o_lat = (jnp.einsum('bhts,bsr->bthr', p[..., :P], ckv_past)
             + jnp.einsum('bhts,bsr->bthr', p[..., P:], ckv_new))
    o = jnp.einsum('bthr,rhd->bthd', o_lat, w_uv)
    return o.reshape(B, T, MLA_HEADS * MLA_V)


def token_mixer(h, pos, lp, gdn_s0, conv_hist, rwkv_s0, shift_hist, ckv_past, kr_past, prompt_mode):
    B, T, _ = h.shape
    proj = jnp.einsum('btd,dc->btc', h, lp['w_in'])
    (qkv_pre, beta_raw, alpha_raw, z_raw, rw_pre, cq_raw, ckv_raw, kr_raw, gate_raw) = split_cols(
        proj, (GDN_CONV_CH, GDN_HEADS, GDN_HEADS, GDN_VAL_W, RWKV_COLS,
               MLA_Q_RANK, MLA_KV_RANK, MLA_ROPE, 3 * D_MODEL))

    qkv, conv_buf = causal_short_conv(qkv_pre, conv_hist, lp['gdn_conv_w'])
    q_a, k_a, v_a = split_cols(qkv, (GDN_KEY_W, GDN_KEY_W, GDN_VAL_W))
    q_a = l2norm(q_a.reshape(B, T, GDN_HEADS, GDN_DK))
    k_a = l2norm(k_a.reshape(B, T, GDN_HEADS, GDN_DK))
    v_a = v_a.reshape(B, T, GDN_HEADS, GDN_DV)
    beta = jax.nn.sigmoid(beta_raw)
    log_a = -jnp.exp(lp['gdn_a_log'].astype(F32)) * jax.nn.softplus((alpha_raw + lp['gdn_dt_bias']).astype(F32))
    if prompt_mode:
        o_a, gdn_s = gdn_chunked(q_a, k_a, v_a, log_a, beta, gdn_s0)
    else:
        o_a, gdn_s = gdn_recurrent(q_a, k_a, v_a, log_a, beta, gdn_s0)
    o_a = (rms_norm(o_a, lp['gdn_norm_g'])
           * jax.nn.silu(z_raw).reshape(B, T, GDN_HEADS, GDN_DV)).reshape(B, T, GDN_VAL_W)

    prev = jnp.concatenate([shift_hist, rw_pre], axis=1)[:, :T]
    xm = rw_pre + (prev - rw_pre) * lp['rwkv_mu']
    r, k_b, v_b, w_lo, a_lo, g_lo = split_cols(
        xm, (RWKV_W, RWKV_W, RWKV_W, RWKV_DECAY_LORA, RWKV_AAA_LORA, RWKV_GATE_LORA))
    w_log = -jax.nn.softplus(-(lp['rwkv_w0'] + jnp.tanh(w_lo) @ lp['rwkv_w2'])) - 0.5
    decay = jnp.exp(-jnp.exp(w_log.astype(F32)))
    a = jax.nn.sigmoid(lp['rwkv_a0'] + a_lo @ lp['rwkv_a2'])
    g_b = jax.nn.sigmoid(g_lo) @ lp['rwkv_g2']
    hshape = (B, T, RWKV_HEADS, RWKV_HEAD)
    kk = l2norm((k_b * lp['rwkv_k_k']).reshape(hshape))
    k_b = k_b * (1.0 + (a - 1.0) * lp['rwkv_k_a'])
    r_h, k_h, v_h = r.reshape(hshape), k_b.reshape(hshape), v_b.reshape(hshape)
    o_b, rwkv_s = rwkv7_scan(r_h, decay.reshape(hshape), k_h, v_h, kk, a.reshape(hshape), rwkv_s0)
    o_b = head_group_norm(o_b, lp['rwkv_ln_g'], lp['rwkv_ln_b'])
    bonus = (jnp.sum(r_h * k_h * lp['rwkv_r_k'], axis=-1, keepdims=True) * v_h).reshape(B, T, RWKV_W)
    o_b = (o_b + bonus) * g_b

    q_c = jnp.einsum('btr,rhd->bthd', rms_norm(cq_raw, lp['mla_q_norm_g']), lp['mla_w_uq'])
    q_nope = q_c[..., :MLA_NOPE]
    q_rope = rope(q_c[..., MLA_NOPE:], pos)
    ckv = rms_norm(ckv_raw, lp['mla_kv_norm_g'])
    kr = rope(kr_raw, pos)
    if prompt_mode:
        o_c = mla_prompt(q_nope, q_rope, ckv, kr, lp['mla_w_uk'], lp['mla_w_uv'])
    else:
        o_c = mla_sample(q_nope, q_rope, ckv, kr, ckv_past, kr_past, lp['mla_w_uk'], lp['mla_w_uv'])

    gates = jax.nn.sigmoid(gate_raw + lp['b_gate']).reshape(B, T, 3, D_MODEL)
    merged = (gates[:, :, 0] * (o_a @ lp['w_br_a'])
              + gates[:, :, 1] * (o_b @ lp['w_br_b'])
              + gates[:, :, 2] * (o_c @ lp['w_br_c']))
    out = (merged @ lp['w_out']).astype(h.dtype)
    return (out, ckv, kr, gdn_s.astype(h.dtype), conv_buf, rwkv_s.astype(h.dtype), rw_pre[:, -1])


def swiglu(h, w_up, w_down):
    gate, up = split_cols(h @ w_up, (D_FF, D_FF))
    return (jax.nn.silu(gate) * up) @ w_down


def setup_inputs(seed: int = 0) -> dict:
    key = jax.random.key(seed)
    ks = iter(jax.random.split(key, 64))

    def nrm(shape, scale):
        return jax.random.normal(next(ks), shape, F32) * scale

    def gain(shape):
        return 1.0 + nrm(shape, 0.01)

    def unif(shape, lo, hi):
        return jax.random.uniform(next(ks), shape, F32, lo, hi)

    n_pages = PAST_LEN // PAGE_SIZE
    n_pool = (5 * DEC_BATCH * n_pages) // 4
    page_table = jax.random.permutation(next(ks), n_pool)[:DEC_BATCH * n_pages]
    page_table = page_table.reshape(DEC_BATCH, n_pages).astype(jnp.int32)
    dt = jnp.exp(unif((DEPTH, GDN_HEADS), math.log(1e-3), math.log(1e-1)))
    return {
        'x_prompt': nrm((BATCH, SEQ, D_MODEL), 1.0),
        'x_sample': nrm((DEC_BATCH, DEC_SEQ, D_MODEL), 1.0),
        'cache_ckv': nrm((DEPTH, n_pool, PAGE_SIZE, MLA_KV_RANK), 1.0),
        'cache_krope': nrm((DEPTH, n_pool, PAGE_SIZE, MLA_ROPE), 1.0),
        'page_table': page_table,
        'state_gdn': nrm((DEPTH, DEC_BATCH, GDN_HEADS, GDN_DK, GDN_DV), 0.05),
        'state_gdn_conv': nrm((DEPTH, DEC_BATCH, GDN_CONV - 1, GDN_CONV_CH), 1.0),
        'state_rwkv': nrm((DEPTH, DEC_BATCH, RWKV_HEADS, RWKV_HEAD, RWKV_HEAD), 0.05),
        'state_rwkv_shift': nrm((DEPTH, DEC_BATCH, RWKV_COLS), 1.0),
        'norm_mix_g': gain((DEPTH, D_MODEL)),
        'norm_ffn_g': gain((DEPTH, D_MODEL)),
        'norm_final_g': gain((D_MODEL,)),
        'w_in': nrm((DEPTH, D_MODEL, IN_COLS), D_MODEL ** -0.5),
        'b_gate': nrm((DEPTH, 3 * D_MODEL), 0.02),
        'gdn_conv_w': nrm((DEPTH, GDN_CONV, GDN_CONV_CH), 0.5),
        'gdn_a_log': jnp.log(unif((DEPTH, GDN_HEADS), 1.0, 16.0)),
        'gdn_dt_bias': dt + jnp.log(-jnp.expm1(-dt)),
        'gdn_norm_g': gain((DEPTH, GDN_DV)),
        'rwkv_mu': unif((DEPTH, RWKV_COLS), 0.0, 1.0),
        'rwkv_w0': unif((DEPTH, RWKV_W), -6.0, -1.0),
        'rwkv_w2': nrm((DEPTH, RWKV_DECAY_LORA, RWKV_W), 0.1),
        'rwkv_a0': nrm((DEPTH, RWKV_W), 0.1),
        'rwkv_a2': nrm((DEPTH, RWKV_AAA_LORA, RWKV_W), 0.1),
        'rwkv_g2': nrm((DEPTH, RWKV_GATE_LORA, RWKV_W), RWKV_GATE_LORA ** -0.5),
        'rwkv_k_k': 0.85 + nrm((DEPTH, RWKV_W), 0.01),
        'rwkv_k_a': gain((DEPTH, RWKV_W)),
        'rwkv_r_k': nrm((DEPTH, RWKV_HEADS, RWKV_HEAD), 0.1),
        'rwkv_ln_g': gain((DEPTH, RWKV_W)),
        'rwkv_ln_b': nrm((DEPTH, RWKV_W), 0.01),
        'mla_q_norm_g': gain((DEPTH, MLA_Q_RANK)),
        'mla_kv_norm_g': gain((DEPTH, MLA_KV_RANK)),
        'mla_w_uq': nrm((DEPTH, MLA_Q_RANK, MLA_HEADS, MLA_NOPE + MLA_ROPE), MLA_Q_RANK ** -0.5),
        'mla_w_uk': nrm((DEPTH, MLA_KV_RANK, MLA_HEADS, MLA_NOPE), MLA_KV_RANK ** -0.5),
        'mla_w_uv': nrm((DEPTH, MLA_KV_RANK, MLA_HEADS, MLA_V), MLA_KV_RANK ** -0.5),
        'w_br_a': nrm((DEPTH, GDN_VAL_W, D_MODEL), GDN_VAL_W ** -0.5),
        'w_br_b': nrm((DEPTH, RWKV_W, D_MODEL), RWKV_W ** -0.5),
        'w_br_c': nrm((DEPTH, MLA_HEADS * MLA_V, D_MODEL), (MLA_HEADS * MLA_V) ** -0.5),
        'w_out': nrm((DEPTH, D_MODEL, D_MODEL), D_MODEL ** -0.5),
        'w_ffn_up': nrm((DEPTH, D_MODEL, 2 * D_FF), D_MODEL ** -0.5),
        'w_ffn_down': nrm((DEPTH, D_FF, D_MODEL), D_FF ** -0.5),
    }


def reference(x_prompt, x_sample, cache_ckv, cache_krope, page_table, state_gdn, state_gdn_conv,
              state_rwkv, state_rwkv_shift, norm_mix_g, norm_ffn_g, norm_final_g, w_in, b_gate,
              gdn_conv_w, gdn_a_log, gdn_dt_bias, gdn_norm_g, rwkv_mu, rwkv_w0, rwkv_w2, rwkv_a0,
              rwkv_a2, rwkv_g2, rwkv_k_k, rwkv_k_a, rwkv_r_k, rwkv_ln_g, rwkv_ln_b, mla_q_norm_g,
              mla_kv_norm_g, mla_w_uq, mla_w_uk, mla_w_uv, w_br_a, w_br_b, w_br_c, w_out,
              w_ffn_up, w_ffn_down):
    bp, sp = x_prompt.shape[:2]
    bs, ss = x_sample.shape[:2]
    past_len = page_table.shape[1] * cache_ckv.shape[2]
    pos_p = jnp.arange(sp, dtype=jnp.int32)
    pos_s = past_len + jnp.arange(ss, dtype=jnp.int32)
    xp, xs = x_prompt, x_sample
    new_p = [[] for _ in range(6)]
    new_s = [[] for _ in range(6)]
    for l in range(DEPTH):
        lp = {
            'w_in': w_in[l], 'b_gate': b_gate[l], 'gdn_conv_w': gdn_conv_w[l],
            'gdn_a_log': gdn_a_log[l], 'gdn_dt_bias': gdn_dt_bias[l], 'gdn_norm_g': gdn_norm_g[l],
            'rwkv_mu': rwkv_mu[l], 'rwkv_w0': rwkv_w0[l], 'rwkv_w2': rwkv_w2[l], 'rwkv_a0': rwkv_a0[l],
            'rwkv_a2': rwkv_a2[l], 'rwkv_g2': rwkv_g2[l], 'rwkv_k_k': rwkv_k_k[l], 'rwkv_k_a': rwkv_k_a[l],
            'rwkv_r_k': rwkv_r_k[l], 'rwkv_ln_g': rwkv_ln_g[l], 'rwkv_ln_b': rwkv_ln_b[l],
            'mla_q_norm_g': mla_q_norm_g[l], 'mla_kv_norm_g': mla_kv_norm_g[l],
            'mla_w_uq': mla_w_uq[l], 'mla_w_uk': mla_w_uk[l], 'mla_w_uv': mla_w_uv[l],
            'w_br_a': w_br_a[l], 'w_br_b': w_br_b[l], 'w_br_c': w_br_c[l], 'w_out': w_out[l],
        }
        mix_p, *st_p = token_mixer(
            rms_norm(xp, norm_mix_g[l]), pos_p, lp,
            jnp.zeros((bp, GDN_HEADS, GDN_DK, GDN_DV), F32),
            jnp.zeros((bp, GDN_CONV - 1, GDN_CONV_CH), xp.dtype),
            jnp.zeros((bp, RWKV_HEADS, RWKV_HEAD, RWKV_HEAD), F32),
            jnp.zeros((bp, 1, RWKV_COLS), xp.dtype), None, None, True)
        for lst, arr in zip(new_p, st_p):
            lst.append(arr)
        xp = xp + mix_p
        xp = xp + swiglu(rms_norm(xp, norm_ffn_g[l]), w_ffn_up[l], w_ffn_down[l])
        ckv_past = cache_ckv[l, page_table].reshape(bs, past_len, MLA_KV_RANK)
        kr_past = cache_krope[l, page_table].reshape(bs, past_len, MLA_ROPE)
        mix_s, *st_s = token_mixer(
            rms_norm(xs, norm_mix_g[l]), pos_s, lp, state_gdn[l], state_gdn_conv[l],
            state_rwkv[l], state_rwkv_shift[l][:, None], ckv_past, kr_past, False)
        for lst, arr in zip(new_s, st_s):
            lst.append(arr)
        xs = xs + mix_s
        xs = xs + swiglu(rms_norm(xs, norm_ffn_g[l]), w_ffn_up[l], w_ffn_down[l])
    y_prompt = rms_norm(xp, norm_final_g)
    y_sample = rms_norm(xs, norm_final_g)
    p_ckv, p_krope, p_gdn, p_gdn_conv, p_rwkv, p_rwkv_shift = [jnp.stack(a) for a in new_p]
    s_ckv, s_krope, s_gdn, s_gdn_conv, s_rwkv, s_rwkv_shift = [jnp.stack(a) for a in new_s]
    return (y_prompt, y_sample, p_ckv, p_krope, p_gdn, p_gdn_conv, p_rwkv, p_rwkv_shift,
            s_ckv, s_krope, s_gdn, s_gdn_conv, s_rwkv, s_rwkv_shift)
```

```python
import functools
import math

import jax
import jax.numpy as jnp
from jax import lax
from jax.experimental import pallas as pl
from jax.experimental.pallas import tpu as pltpu

F32 = jnp.float32
BF16 = jnp.bfloat16

D_MODEL = 1024
GDN_HEADS = 4
GDN_DK = 128
GDN_DV = 128
GDN_CONV = 4
GDN_KEY_W = GDN_HEADS * GDN_DK
GDN_VAL_W = GDN_HEADS * GDN_DV
GDN_CONV_CH = 2 * GDN_KEY_W + GDN_VAL_W
RWKV_HEADS = 8
RWKV_HEAD = 64
RWKV_W = RWKV_HEADS * RWKV_HEAD
RWKV_DECAY_LORA = 64
RWKV_AAA_LORA = 64
RWKV_GATE_LORA = 128
RWKV_COLS = 3 * RWKV_W + RWKV_DECAY_LORA + RWKV_AAA_LORA + RWKV_GATE_LORA
RWKV_LN_EPS = 64e-5
MLA_HEADS = 4
MLA_Q_RANK = 256
MLA_KV_RANK = 256
MLA_NOPE = 128
MLA_ROPE = 64
MLA_V = 128
MLA_SCALE = 1.0 / math.sqrt(MLA_NOPE + MLA_ROPE)
ROPE_THETA = 10000.0
D_FF = ((8 * D_MODEL // 3 + 255) // 256) * 256
NORM_EPS = 1e-6
L2_EPS = 1e-6

LANES = 128
SUBLANES = 8
VMEM_LIMIT = 56 * 1024 * 1024

_O_QKV = 0
_O_BETA = _O_QKV + GDN_CONV_CH
_O_ALPHA = _O_BETA + GDN_HEADS
_O_Z = _O_ALPHA + GDN_HEADS
_O_RW = _O_Z + GDN_VAL_W
_O_CQ = _O_RW + RWKV_COLS
_O_CKV = _O_CQ + MLA_Q_RANK
_O_KR = _O_CKV + MLA_KV_RANK
_O_GATE = _O_KR + MLA_ROPE

RW_PAD = 3 * RWKV_W + 3 * LANES
GDN_PCOLS = GDN_CONV_CH + GDN_VAL_W + LANES
PRE_ROW0 = SUBLANES


def _bdot(a, b):
    return jnp.dot(a.astype(BF16), b.astype(BF16), preferred_element_type=F32)


def _bdot_nt(a, b):
    return lax.dot_general(a.astype(BF16), b.astype(BF16), (((1,), (1,)), ((), ())),
                           preferred_element_type=F32)


def _hdot(a, b):
    return jnp.dot(a, b, precision=lax.Precision.HIGHEST, preferred_element_type=F32)


def _split_dot(a, b_exact, parts):
    acc = None
    rem = a
    for _ in range(parts):
        hi = rem.astype(BF16)
        t = jnp.dot(hi, b_exact, preferred_element_type=F32)
        acc = t if acc is None else acc + t
        rem = rem - hi.astype(F32)
    return acc


def _rms(x, g, eps=NORM_EPS):
    return x * lax.rsqrt(jnp.mean(x * x, axis=-1, keepdims=True) + eps) * g


def _sigmoid(x):
    return 1.0 / (1.0 + jnp.exp(-x))


def _softplus(x):
    return jnp.maximum(x, 0.0) + jnp.log(1.0 + jnp.exp(-jnp.abs(x)))


def _tri_masks(C):
    r = lax.broadcasted_iota(jnp.int32, (C, C), 0)
    c = lax.broadcasted_iota(jnp.int32, (C, C), 1)
    masks = []
    k = 0
    while (1 << k) < C:
        rr = r >> k
        cc = c >> k
        m = jnp.where((rr ^ cc) == 1, jnp.where((rr & 1) == 1, 1.0, 0.0), 0.0)
        masks.append(m.astype(F32))
        k += 1
    return r, c, masks


def _unit_lower_inverse(L, masks, eye):
    T = eye - L * masks[0]
    for m in masks[1:]:
        T = T - _hdot(T, _hdot(L * m, T))
    return T


def _gdn_kernel(x_ref, s0_ref, hist_ref, ng_ref, wg_ref, wbat_ref, cw_ref, prow_ref, pcol_ref, gng_ref,
                o_ref, s_ref, cb_ref, pre_scr, *, Tt, C, t_real, t_total):
    ti = pl.program_id(1)

    @pl.when(ti == 0)
    def _():
        s_ref[...] = s0_ref[...]
        pre_scr[PRE_ROW0 - 3:PRE_ROW0, :] = hist_ref[0]

    h = _rms(x_ref[0], ng_ref[...]).astype(BF16)
    proj = jnp.dot(h, wg_ref[...], preferred_element_type=F32)
    ba_row = lax.dot_general(wbat_ref[...], h, (((1,), (1,)), ((), ())),
                             preferred_element_type=F32)

    pre_scr[PRE_ROW0:PRE_ROW0 + Tt, :] = proj[:, :GDN_CONV_CH]
    cw = cw_ref[...]
    y = pre_scr[PRE_ROW0 - 3:PRE_ROW0 - 3 + Tt, :] * cw[0:1]
    for i in range(1, GDN_CONV):
        y = y + pre_scr[PRE_ROW0 - 3 + i:PRE_ROW0 - 3 + i + Tt, :] * cw[i:i + 1]
    qkv = y * _sigmoid(y)
    real_rows = min(Tt, t_real)
    carry = pre_scr[PRE_ROW0 - 3 + real_rows:PRE_ROW0 + real_rows, :]
    pre_scr[PRE_ROW0 - 3:PRE_ROW0, :] = carry
    cb_ref[0] = carry

    ba = proj[:, GDN_CONV_CH + GDN_VAL_W:]
    prow = prow_ref[...]
    beta_col = _sigmoid(ba)
    loga_col = -jnp.exp(prow[0:1]) * _softplus(ba + prow[1:2])
    pcol = pcol_ref[...]
    loga_row = -jnp.exp(pcol[:, 0:1]) * _softplus(ba_row + pcol[:, 1:2])
    if t_real < t_total:
        tcol = lax.broadcasted_iota(jnp.int32, (Tt, 1), 0) + ti * Tt
        trow = lax.broadcasted_iota(jnp.int32, (1, Tt), 1) + ti * Tt
        beta_col = jnp.where(tcol < t_real, beta_col, 0.0)
        loga_col = jnp.where(tcol < t_real, loga_col, 0.0)
        loga_row = jnp.where(trow < t_real, loga_row, 0.0)

    rt = lax.broadcasted_iota(jnp.int32, (Tt, Tt), 0)
    ct = lax.broadcasted_iota(jnp.int32, (Tt, Tt), 1)
    sh = C.bit_length() - 1
    same = (rt >> sh) == (ct >> sh)
    tri = jnp.where(same, jnp.where(rt >= ct, 1.0, 0.0), 0.0).astype(F32)
    tri_t = jnp.where(same, jnp.where(ct >= rt, 1.0, 0.0), 0.0).astype(F32)
    gcol = _hdot(tri, loga_col)
    grow = _hdot(loga_row, tri_t)

    r, c, masks = _tri_masks(C)
    eye = jnp.where(r == c, 1.0, 0.0).astype(F32)
    incl = r >= c
    strict = r > c
    z = proj[:, GDN_CONV_CH:GDN_CONV_CH + GDN_VAL_W]
    gng = gng_ref[...]
    scale = GDN_DK ** -0.5

    for n in range(Tt // C):
        r0 = n * C
        for hh in range(GDN_HEADS):
            lk = slice(hh * GDN_DK, (hh + 1) * GDN_DK)
            qh = qkv[r0:r0 + C, lk]
            kh = qkv[r0:r0 + C, GDN_KEY_W + hh * GDN_DK:GDN_KEY_W + (hh + 1) * GDN_DK]
            vh = qkv[r0:r0 + C, 2 * GDN_KEY_W + hh * GDN_DV:2 * GDN_KEY_W + (hh + 1) * GDN_DV]
            qh = qh * lax.rsqrt(jnp.sum(qh * qh, axis=-1, keepdims=True) + L2_EPS) * scale
            kh = kh * lax.rsqrt(jnp.sum(kh * kh, axis=-1, keepdims=True) + L2_EPS)
            bc = beta_col[r0:r0 + C, hh:hh + 1]
            gc = gcol[r0:r0 + C, GDN_HEADS + hh:GDN_HEADS + hh + 1]
            gr = grow[GDN_HEADS + hh:GDN_HEADS + hh + 1, r0:r0 + C]
            decay = jnp.where(incl, jnp.exp(jnp.where(incl, gc - gr, 0.0)), 0.0)
            eg = jnp.exp(gc)
            kb = kh * bc
            L = jnp.where(strict, _bdot_nt(kb, kh) * decay, 0.0)
            tm = _unit_lower_inverse(L, masks, eye)
            u = _bdot(tm, vh * bc)
            w = _bdot(tm, kb * eg)
            qk = jnp.where(incl, _bdot_nt(qh, kh) * decay, 0.0)
            S = s_ref[0, hh]
            v_new = u - _bdot(w, S)
            o = _bdot(qh * eg, S) + _bdot(qk, v_new)
            g_last = gc[C - 1:C, :]
            kd = kh * jnp.exp(g_last - gc)
            s_ref[0, hh] = S * jnp.exp(g_last) + _bdot(kd.T, v_new)
            zh = z[r0:r0 + C, hh * GDN_DV:(hh + 1) * GDN_DV]
            o_ref[0, r0:r0 + C, hh * GDN_DV:(hh + 1) * GDN_DV] = _rms(o, gng) * (zh * _sigmoid(zh))


def _gdn_call(x, s0, hist, w, *, Tt, C, t_real):
    B, T, D = x.shape
    kern = functools.partial(_gdn_kernel, Tt=Tt, C=C, t_real=t_real, t_total=T)
    full = lambda a: pl.BlockSpec(a.shape, lambda b, t: (0,) * a.ndim)
    return pl.pallas_call(
        kern,
        grid=(B, T // Tt),
        in_specs=[
            pl.BlockSpec((1, Tt, D), lambda b, t: (b, t, 0)),
            pl.BlockSpec((1, GDN_HEADS, GDN_DK, GDN_DV), lambda b, t: (b, 0, 0, 0)),
            pl.BlockSpec((1, GDN_CONV - 1, GDN_CONV_CH), lambda b, t: (b, 0, 0)),
            full(w['norm_mix_g']), full(w['gdn_w']), full(w['gdn_wba_t']), full(w['gdn_conv_w']),
            full(w['gdn_prow']), full(w['gdn_pcol']), full(w['gdn_norm_g']),
        ],
        out_specs=[
            pl.BlockSpec((1, Tt, GDN_VAL_W), lambda b, t: (b, t, 0)),
            pl.BlockSpec((1, GDN_HEADS, GDN_DK, GDN_DV), lambda b, t: (b, 0, 0, 0)),
            pl.BlockSpec((1, GDN_CONV - 1, GDN_CONV_CH), lambda b, t: (b, 0, 0)),
        ],
        out_shape=[
            jax.ShapeDtypeStruct((B, T, GDN_VAL_W), F32),
            jax.ShapeDtypeStruct((B, GDN_HEADS, GDN_DK, GDN_DV), F32),
            jax.ShapeDtypeStruct((B, GDN_CONV - 1, GDN_CONV_CH), F32),
        ],
        scratch_shapes=[pltpu.VMEM((PRE_ROW0 + Tt, GDN_CONV_CH), F32)],
        compiler_params=pltpu.CompilerParams(
            dimension_semantics=("arbitrary", "arbitrary"), vmem_limit_bytes=VMEM_LIMIT),
        name="gdn_mixer",
    )(x, s0, hist, w['norm_mix_g'], w['gdn_w'], w['gdn_wba_t'], w['gdn_conv_w'],
      w['gdn_prow'], w['gdn_pcol'], w['gdn_norm_g'])


def _rwkv_kernel(x_ref, s0_ref, sh_ref, ng_ref, wr_ref, mu_ref, w0_ref, w2_ref, a0_ref, a2_ref, g2_ref,
                 kk_ref, ka_ref, rk_ref, lng_ref, lnb_ref, e_ref,
                 o_ref, s_ref, sho_ref, pre_scr, y_scr, *, Tt, C, t_real, t_total):
    ti = pl.program_id(1)
    N = RWKV_HEAD

    @pl.when(ti == 0)
    def _():
        s_ref[...] = s0_ref[...]
        pre_scr[PRE_ROW0 - 1:PRE_ROW0, :] = sh_ref[0]

    h = _rms(x_ref[0], ng_ref[...]).astype(BF16)
    pre = jnp.dot(h, wr_ref[...], preferred_element_type=F32)
    pre_scr[PRE_ROW0:PRE_ROW0 + Tt, :] = pre
    prev = pre_scr[PRE_ROW0 - 1:PRE_ROW0 - 1 + Tt, :]
    real_rows = min(Tt, t_real)
    last = pre_scr[PRE_ROW0 - 1 + real_rows:PRE_ROW0 + real_rows, :]
    pre_scr[PRE_ROW0 - 1:PRE_ROW0, :] = last
    sho_ref[0] = last

    xm = pre + (prev - pre) * mu_ref[...]
    W = RWKV_W
    rr = xm[:, 0:W]
    kx = xm[:, W:2 * W]
    vv = xm[:, 2 * W:3 * W]
    wlo = xm[:, 3 * W:3 * W + LANES]
    alo = xm[:, 3 * W + LANES:3 * W + 2 * LANES]
    glo = xm[:, 3 * W + 2 * LANES:3 * W + 3 * LANES]
    w_log = -_softplus(-(w0_ref[...] + _bdot(jnp.tanh(wlo), w2_ref[...]))) - 0.5
    logd = -jnp.exp(w_log)
    a = _sigmoid(a0_ref[...] + _bdot(alo, a2_ref[...]))
    gb = _bdot(_sigmoid(glo), g2_ref[...])
    E = e_ref[...]
    kkr = kx * kk_ref[...]
    kk = kkr * lax.rsqrt(_split_dot(kkr * kkr, E, 2) + L2_EPS)
    kb = kx * (1.0 + (a - 1.0) * ka_ref[...])
    a_eff = a
    v_eff = vv
    if t_real < t_total:
        tcol = lax.broadcasted_iota(jnp.int32, (Tt, 1), 0) + ti * Tt
        valid = tcol < t_real
        logd = jnp.where(valid, logd, 0.0)
        a_eff = jnp.where(valid, a, 0.0)
        v_eff = jnp.where(valid, vv, 0.0)

    rt = lax.broadcasted_iota(jnp.int32, (Tt, Tt), 0)
    ct = lax.broadcasted_iota(jnp.int32, (Tt, Tt), 1)
    sh = C.bit_length() - 1
    tri = jnp.where((rt >> sh) == (ct >> sh), jnp.where(rt >= ct, 1.0, 0.0), 0.0).astype(F32)
    logG = _hdot(tri, logd)
    G = jnp.exp(logG)
    Ginv = jnp.exp(-logG)
    aq_all = kk * jnp.exp(logG - logd)
    bk_all = -(a_eff * kk) * Ginv
    kd_all = kb * Ginv
    rq_all = rr * G

    r, c, masks = _tri_masks(C)
    eye = jnp.where(r == c, 1.0, 0.0).astype(F32)
    incl = r >= c
    strict = r > c

    for n in range(Tt // C):
        r0 = n * C
        for hh in range(RWKV_HEADS):
            ls = slice(hh * N, (hh + 1) * N)
            aq = aq_all[r0:r0 + C, ls]
            bk = bk_all[r0:r0 + C, ls]
            kd = kd_all[r0:r0 + C, ls]
            rq = rq_all[r0:r0 + C, ls]
            vh = v_eff[r0:r0 + C, ls]
            gl = G[r0 + C - 1:r0 + C, ls]
            S = s_ref[0, hh]
            A = jnp.where(strict, _bdot_nt(aq, bk), 0.0)
            Bm = jnp.where(strict, _bdot_nt(aq, kd), 0.0)
            tm = _unit_lower_inverse(-A, masks, eye)
            rhs = _bdot_nt(aq, S) + _bdot(Bm, vh)
            Wm = _bdot(tm, rhs)
            ra = jnp.where(incl, _bdot_nt(rq, bk), 0.0)
            rk = jnp.where(incl, _bdot_nt(rq, kd), 0.0)
            y_scr[r0:r0 + C, ls] = _bdot_nt(rq, S) + _bdot(ra, Wm) + _bdot(rk, vh)
            s_ref[0, hh] = S * gl + _bdot(Wm.T, bk * gl) + _bdot(vh.T, kd * gl)

    y = y_scr[...]
    inv_n = 1.0 / N
    mu = _split_dot(y, E, 2) * inv_n
    yc = y - mu
    var = _split_dot(yc * yc, E, 2) * inv_n
    yn = yc * lax.rsqrt(var + RWKV_LN_EPS) * lng_ref[...] + lnb_ref[...]
    bonus = _split_dot(rr * kb * rk_ref[...], E, 2) * vv
    o_ref[0] = (yn + bonus) * gb


def _rwkv_call(x, s0, shift, w, *, Tt, C, t_real):
    B, T, D = x.shape
    kern = functools.partial(_rwkv_kernel, Tt=Tt, C=C, t_real=t_real, t_total=T)
    full = lambda a: pl.BlockSpec(a.shape, lambda b, t: (0,) * a.ndim)
    names = ['norm_mix_g', 'rwkv_w', 'rwkv_mu', 'rwkv_w0', 'rwkv_w2', 'rwkv_a0', 'rwkv_a2', 'rwkv_g2',
             'rwkv_k_k', 'rwkv_k_a', 'rwkv_r_k', 'rwkv_ln_g', 'rwkv_ln_b', 'rwkv_e']
    return pl.pallas_call(
        kern,
        grid=(B, T // Tt),
        in_specs=[
            pl.BlockSpec((1, Tt, D), lambda b, t: (b, t, 0)),
            pl.BlockSpec((1, RWKV_HEADS, RWKV_HEAD, RWKV_HEAD), lambda b, t: (b, 0, 0, 0)),
            pl.BlockSpec((1, 1, RW_PAD), lambda b, t: (b, 0, 0)),
        ] + [full(w[n]) for n in names],
        out_specs=[
            pl.BlockSpec((1, Tt, RWKV_W), lambda b, t: (b, t, 0)),
            pl.BlockSpec((1, RWKV_HEADS, RWKV_HEAD, RWKV_HEAD), lambda b, t: (b, 0, 0, 0)),
            pl.BlockSpec((1, 1, RW_PAD), lambda b, t: (b, 0, 0)),
        ],
        out_shape=[
            jax.ShapeDtypeStruct((B, T, RWKV_W), F32),
            jax.ShapeDtypeStruct((B, RWKV_HEADS, RWKV_HEAD, RWKV_HEAD), F32),
            jax.ShapeDtypeStruct((B, 1, RW_PAD), F32),
        ],
        scratch_shapes=[pltpu.VMEM((PRE_ROW0 + Tt, RW_PAD), F32), pltpu.VMEM((Tt, RWKV_W), F32)],
        compiler_params=pltpu.CompilerParams(
            dimension_semantics=("arbitrary", "arbitrary"), vmem_limit_bytes=VMEM_LIMIT),
        name="rwkv_mixer",
    )(x, s0, shift, *[w[n] for n in names])


def _mla_proj_kernel(x_ref, ng_ref, wm_ref, qg_ref, kvg_ref, wq_ref, cos_ref, sin_ref, cos4_ref, sin4_ref,
                     *rest, prompt):
    if prompt:
        wuk_ref, wuv_ref, ckv_ref, kr_ref, qn_ref, qr_ref, kn_ref, v_ref, krb_ref = rest
    else:
        ckv_ref, kr_ref, qn_ref, qr_ref = rest
    h = _rms(x_ref[0], ng_ref[...]).astype(BF16)
    p = jnp.dot(h, wm_ref[...], preferred_element_type=F32)
    cqn = _rms(p[:, :MLA_Q_RANK], qg_ref[...]).astype(BF16)
    ckv = _rms(p[:, MLA_Q_RANK:MLA_Q_RANK + MLA_KV_RANK], kvg_ref[...])
    o = MLA_Q_RANK + MLA_KV_RANK
    kr = p[:, o:o + MLA_ROPE] * cos_ref[...] + p[:, o + LANES:o + LANES + MLA_ROPE] * sin_ref[...]
    ckv_ref[0] = ckv
    kr_ref[0] = kr
    q = jnp.dot(cqn, wq_ref[...], preferred_element_type=F32)
    wn = MLA_HEADS * MLA_NOPE
    wr = MLA_HEADS * MLA_ROPE
    qn_ref[0] = (q[:, :wn] * MLA_SCALE).astype(qn_ref.dtype)
    qr = q[:, wn:wn + wr] * cos4_ref[...] + q[:, wn + wr:wn + 2 * wr] * sin4_ref[...]
    qr_ref[0] = (qr * MLA_SCALE).astype(qr_ref.dtype)
    if prompt:
        cb = ckv.astype(BF16)
        kn_ref[0] = jnp.dot(cb, wuk_ref[...], preferred_element_type=F32).astype(BF16)
        v_ref[0] = jnp.dot(cb, wuv_ref[...], preferred_element_type=F32).astype(BF16)
        krb_ref[0] = kr.astype(BF16)


def _mla_proj_call(x, w, rope_tabs, *, Tt, prompt):
    B, T, D = x.shape
    cos, sin, cos4, sin4 = rope_tabs
    full = lambda a: pl.BlockSpec(a.shape, lambda b, t: (0,) * a.ndim)
    tab = lambda a: pl.BlockSpec((Tt, a.shape[1]), lambda b, t: (t, 0))
    tok = lambda n: pl.BlockSpec((1, Tt, n), lambda b, t: (b, t, 0))
    wn = MLA_HEADS * MLA_NOPE
    wr = MLA_HEADS * MLA_ROPE
    qdt = BF16 if prompt else F32
    ins = [x, w['norm_mix_g'], w['mla_w'], w['mla_q_norm_g'], w['mla_kv_norm_g'], w['mla_wq'], cos, sin, cos4, sin4]
    in_specs = [tok(D), full(ins[1]), full(ins[2]), full(ins[3]), full(ins[4]), full(ins[5]),
                tab(cos), tab(sin), tab(cos4), tab(sin4)]
    out_specs = [tok(MLA_KV_RANK), tok(MLA_ROPE), tok(wn), tok(wr)]
    out_shape = [jax.ShapeDtypeStruct((B, T, MLA_KV_RANK), F32), jax.ShapeDtypeStruct((B, T, MLA_ROPE), F32),
                 jax.ShapeDtypeStruct((B, T, wn), qdt), jax.ShapeDtypeStruct((B, T, wr), qdt)]
    if prompt:
        ins += [w['mla_wuk'], w['mla_wuv']]
        in_specs += [full(w['mla_wuk']), full(w['mla_wuv'])]
        out_specs += [tok(wn), tok(MLA_HEADS * MLA_V), tok(MLA_ROPE)]
        out_shape += [jax.ShapeDtypeStruct((B, T, wn), BF16), jax.ShapeDtypeStruct((B, T, MLA_HEADS * MLA_V), BF16),
                      jax.ShapeDtypeStruct((B, T, MLA_ROPE), BF16)]
    return pl.pallas_call(
        functools.partial(_mla_proj_kernel, prompt=prompt),
        grid=(B, T // Tt),
        in_specs=in_specs, out_specs=out_specs, out_shape=out_shape,
        compiler_params=pltpu.CompilerParams(
            dimension_semantics=("arbitrary", "arbitrary"), vmem_limit_bytes=VMEM_LIMIT),
        name="mla_proj_prompt" if prompt else "mla_proj_sample",
    )(*ins)


def _flash_kernel(qn_ref, qr_ref, kn_ref, kr_ref, v_ref, o_ref, m_scr, l_scr, acc_scr, *, tq):
    qi = pl.program_id(1)
    ki = pl.program_id(2)

    @pl.when(ki == 0)
    def _():
        m_scr[...] = jnp.full(m_scr.shape, -jnp.inf, F32)
        l_scr[...] = jnp.zeros(l_scr.shape, F32)
        acc_scr[...] = jnp.zeros(acc_scr.shape, F32)

    @pl.when(ki <= qi)
    def _():
        row = lax.broadcasted_iota(jnp.int32, (tq, tq), 0)
        col = lax.broadcasted_iota(jnp.int32, (tq, tq), 1)
        keep = col <= row + (qi - ki) * tq
        kr = kr_ref[0]
        for hh in range(MLA_HEADS):
            ln = slice(hh * MLA_NOPE, (hh + 1) * MLA_NOPE)
            lr = slice(hh * MLA_ROPE, (hh + 1) * MLA_ROPE)
            lv = slice(hh * MLA_V, (hh + 1) * MLA_V)
            s = (lax.dot_general(qn_ref[0, :, ln], kn_ref[0, :, ln], (((1,), (1,)), ((), ())),
                                 preferred_element_type=F32)
                 + lax.dot_general(qr_ref[0, :, lr], kr, (((1,), (1,)), ((), ())),
                                   preferred_element_type=F32))
            s = jnp.where(keep, s, -jnp.inf)
            m_old = m_scr[hh]
            m_new = jnp.maximum(m_old, jnp.max(s, axis=-1, keepdims=True))
            alpha = jnp.exp(m_old - m_new)
            p = jnp.exp(s - m_new)
            l_scr[hh] = alpha * l_scr[hh] + jnp.sum(p, axis=-1, keepdims=True)
            acc_scr[:, lv] = alpha * acc_scr[:, lv] + jnp.dot(p.astype(BF16), v_ref[0, :, lv],
                                                              preferred_element_type=F32)
            m_scr[hh] = m_new

    @pl.when(ki == qi)
    def _():
        for hh in range(MLA_HEADS):
            lv = slice(hh * MLA_V, (hh + 1) * MLA_V)
            o_ref[0, :, lv] = acc_scr[:, lv] / l_scr[hh]


def _flash_call(qn, qr, kn, krb, v, *, tq):
    B, T, _ = qn.shape
    nq = T // tq
    qspec = lambda n: pl.BlockSpec((1, tq, n), lambda b, i, j: (b, i, 0))
    kspec = lambda n: pl.BlockSpec((1, tq, n), lambda b, i, j: (b, jnp.minimum(i, j), 0))
    return pl.pallas_call(
        functools.partial(_flash_kernel, tq=tq),
        grid=(B, nq, nq),
        in_specs=[qspec(qn.shape[2]), qspec(qr.shape[2]), kspec(kn.shape[2]), kspec(krb.shape[2]), kspec(v.shape[2])],
        out_specs=pl.BlockSpec((1, tq, v.shape[2]), lambda b, i, j: (b, i, 0)),
        out_shape=jax.ShapeDtypeStruct((B, T, v.shape[2]), F32),
        scratch_shapes=[pltpu.VMEM((MLA_HEADS, tq, 1), F32), pltpu.VMEM((MLA_HEADS, tq, 1), F32),
                        pltpu.VMEM((tq, MLA_HEADS * MLA_V), F32)],
        compiler_params=pltpu.CompilerParams(
            dimension_semantics=("arbitrary", "arbitrary", "arbitrary"), vmem_limit_bytes=VMEM_LIMIT),
        name="mla_flash",
    )(qn, qr, kn, krb, v)


def _decode_kernel(pt_ref, qn_ref, qr_ref, ckvn_ref, krn_ref, wuk_ref, wuv_ref, *rest, G):
    ckv_refs = rest[:G]
    kr_refs = rest[G:2 * G]
    o_ref = rest[2 * G]
    qlat_scr, qrm_scr, m_scr, l_scr, acc_scr = rest[2 * G + 1:]
    j = pl.program_id(1)
    rid = lax.broadcasted_iota(jnp.int32, (SUBLANES, 1), 0)

    @pl.when(j == 0)
    def _():
        qn = qn_ref[0]
        qr = qr_ref[0]
        qlat = jnp.zeros((SUBLANES, MLA_KV_RANK), F32)
        qrm = jnp.zeros((SUBLANES, MLA_ROPE), F32)
        for hh in range(MLA_HEADS):
            ql = _bdot_nt(qn[:, hh * MLA_NOPE:(hh + 1) * MLA_NOPE], wuk_ref[hh])
            qlat = jnp.where(rid == hh, ql[0:1, :], qlat)
            qrm = jnp.where(rid == hh, qr[0:1, hh * MLA_ROPE:(hh + 1) * MLA_ROPE], qrm)
        qlat_scr[...] = qlat
        qrm_scr[...] = qrm
        m_scr[...] = jnp.full(m_scr.shape, -jnp.inf, F32)
        l_scr[...] = jnp.zeros(l_scr.shape, F32)
        acc_scr[...] = jnp.zeros(acc_scr.shape, F32)

    qlat = qlat_scr[...].astype(BF16)
    qrm = qrm_scr[...].astype(BF16)
    pages = [ckv_refs[i][...].astype(BF16) for i in range(G)]
    s = jnp.concatenate(
        [_bdot_nt(qlat, pages[i]) + _bdot_nt(qrm, kr_refs[i][...]) for i in range(G)], axis=1)
    m_old = m_scr[...]
    m_new = jnp.maximum(m_old, jnp.max(s, axis=-1, keepdims=True))
    alpha = jnp.exp(m_old - m_new)
    p = jnp.exp(s - m_new)
    l_new = alpha * l_scr[...] + jnp.sum(p, axis=-1, keepdims=True)
    acc = alpha * acc_scr[...]
    page = pages[0].shape[0]
    for i in range(G):
        acc = acc + jnp.dot(p[:, i * page:(i + 1) * page].astype(BF16), pages[i], preferred_element_type=F32)
    m_scr[...] = m_new
    l_scr[...] = l_new
    acc_scr[...] = acc

    @pl.when(j == pl.num_programs(1) - 1)
    def _():
        ckvn = ckvn_ref[0][0:1, :]
        krn = krn_ref[0][0:1, :]
        ql = qlat_scr[...]
        qr_ = qrm_scr[...]
        s_new = (jnp.sum(ql * ckvn, axis=-1, keepdims=True) + jnp.sum(qr_ * krn, axis=-1, keepdims=True))
        m_fin = jnp.maximum(m_new, s_new)
        a2 = jnp.exp(m_new - m_fin)
        p_new = jnp.exp(s_new - m_fin)
        l_fin = a2 * l_new + p_new
        o_lat = (a2 * acc + p_new * ckvn) / l_fin
        outs = []
        for hh in range(MLA_HEADS):
            oh = _bdot(o_lat, wuv_ref[hh])
            outs.append(oh[hh:hh + 1, :])
        o_ref[0] = jnp.broadcast_to(jnp.concatenate(outs, axis=1), (SUBLANES, MLA_HEADS * MLA_V))


def _decode_call(page_table, qn, qr, ckvn, krn, cache_ckv, cache_krope, layer, w):
    B = qn.shape[0]
    n_pages = page_table.shape[1]
    page = cache_ckv.shape[2]
    G = 8
    while n_pages % G:
        G //= 2
    tok = lambda a: pl.BlockSpec((1,) + a.shape[1:], lambda b, j, pt: (b, 0, 0))
    full = lambda a: pl.BlockSpec(a.shape, lambda b, j, pt: (0,) * a.ndim)

    def cache_spec(width, i):
        return pl.BlockSpec((None, None, page, width), lambda b, j, pt: (layer, pt[b, j * G + i], 0, 0))

    in_specs = ([tok(qn), tok(qr), tok(ckvn), tok(krn), full(w['mla_wuk_h']), full(w['mla_wuv_h'])]
                + [cache_spec(MLA_KV_RANK, i) for i in range(G)]
                + [cache_spec(MLA_ROPE, i) for i in range(G)])
    grid_spec = pltpu.PrefetchScalarGridSpec(
        num_scalar_prefetch=1,
        grid=(B, n_pages // G),
        in_specs=in_specs,
        out_specs=pl.BlockSpec((1, SUBLANES, MLA_HEADS * MLA_V), lambda b, j, pt: (b, 0, 0)),
        scratch_shapes=[pltpu.VMEM((SUBLANES, MLA_KV_RANK), F32), pltpu.VMEM((SUBLANES, MLA_ROPE), F32),
                        pltpu.VMEM((SUBLANES, 1), F32), pltpu.VMEM((SUBLANES, 1), F32),
                        pltpu.VMEM((SUBLANES, MLA_KV_RANK), F32)],
    )
    return pl.pallas_call(
        functools.partial(_decode_kernel, G=G),
        grid_spec=grid_spec,
        out_shape=jax.ShapeDtypeStruct((B, SUBLANES, MLA_HEADS * MLA_V), F32),
        compiler_params=pltpu.CompilerParams(
            dimension_semantics=("arbitrary", "arbitrary"), vmem_limit_bytes=VMEM_LIMIT),
        name="mla_decode",
    )(page_table, qn, qr, ckvn, krn, w['mla_wuk_h'], w['mla_wuv_h'],
      *([cache_ckv] * G), *([cache_krope] * G))


def _merge_ffn_kernel(x_ref, oa_ref, ob_ref, oc_ref, ng_ref, wgt_ref, bg_ref, wa_ref, wb_ref, wc_ref, wo_ref,
                      nf_ref, wup_ref, wdn_ref, fin_ref, y_ref, *, final):
    x = x_ref[...]
    h = _rms(x, ng_ref[...]).astype(BF16)
    gates = _sigmoid(jnp.dot(h, wgt_ref[...], preferred_element_type=F32) + bg_ref[...])
    D = D_MODEL
    merged = (gates[:, 0:D] * _bdot(oa_ref[...], wa_ref[...])
              + gates[:, D:2 * D] * _bdot(ob_ref[...], wb_ref[...])
              + gates[:, 2 * D:3 * D] * _bdot(oc_ref[...], wc_ref[...]))
    x1 = x + _bdot(merged, wo_ref[...])
    h2 = _rms(x1, nf_ref[...]).astype(BF16)
    up = jnp.dot(h2, wup_ref[...], preferred_element_type=F32)
    g = up[:, :D_FF]
    x2 = x1 + _bdot(g * _sigmoid(g) * up[:, D_FF:], wdn_ref[...])
    y_ref[...] = _rms(x2, fin_ref[...]) if final else x2


def _merge_ffn_call(x, oa, ob, oc, w, fin_g, *, tm, final):
    M, D = x.shape
    const = lambda a: pl.BlockSpec(a.shape, lambda i: (0,) * a.ndim, pipeline_mode=pl.Buffered(1))
    row = lambda n: pl.BlockSpec((tm, n), lambda i: (i, 0))
    names = ['norm_mix_g', 'w_gate', 'b_gate', 'w_br_a', 'w_br_b', 'w_br_c', 'w_out', 'norm_ffn_g',
             'w_ffn_up', 'w_ffn_down']
    return pl.pallas_call(
        functools.partial(_merge_ffn_kernel, final=final),
        grid=(M // tm,),
        in_specs=[row(D), row(oa.shape[1]), row(ob.shape[1]), row(oc.shape[1])]
                 + [const(w[n]) for n in names] + [const(fin_g)],
        out_specs=row(D),
        out_shape=jax.ShapeDtypeStruct((M, D), F32),
        compiler_params=pltpu.CompilerParams(
            dimension_semantics=("arbitrary",), vmem_limit_bytes=VMEM_LIMIT),
        name="merge_ffn",
    )(x, oa, ob, oc, *[w[n] for n in names], fin_g)


def _pad_cols(a, n):
    return jnp.pad(a, ((0, 0), (0, n - a.shape[1])))


def _swap_halves(a, width):
    shp = a.shape
    a = a.reshape(shp[:-1] + (shp[-1] // width, 2, width // 2))
    return a[..., ::-1, :].reshape(shp)


def _rw_pad(a):
    W = RWKV_W
    z = jnp.zeros(a.shape[:-1] + (LANES - RWKV_DECAY_LORA,), a.dtype)
    return jnp.concatenate([a[..., :3 * W], a[..., 3 * W:3 * W + RWKV_DECAY_LORA], z,
                            a[..., 3 * W + RWKV_DECAY_LORA:3 * W + RWKV_DECAY_LORA + RWKV_AAA_LORA], z,
                            a[..., 3 * W + RWKV_DECAY_LORA + RWKV_AAA_LORA:]], axis=-1)


def _rw_unpad(a):
    W = RWKV_W
    return jnp.concatenate([a[..., :3 * W], a[..., 3 * W:3 * W + RWKV_DECAY_LORA],
                            a[..., 3 * W + LANES:3 * W + LANES + RWKV_AAA_LORA],
                            a[..., 3 * W + 2 * LANES:]], axis=-1)


def _layer_weights(l, p):
    w_in = p['w_in'][l]
    row = lambda a: a.reshape(1, -1).astype(F32)
    w = {}
    w['norm_mix_g'] = row(p['norm_mix_g'][l])
    w['norm_ffn_g'] = row(p['norm_ffn_g'][l])
    ba = w_in[:, _O_BETA:_O_Z]
    w['gdn_w'] = jnp.concatenate([w_in[:, _O_QKV:_O_BETA], w_in[:, _O_Z:_O_RW], _pad_cols(ba, LANES)],
                                 axis=1).astype(BF16)
    w['gdn_wba_t'] = ba.T.astype(BF16)
    w['gdn_conv_w'] = p['gdn_conv_w'][l]
    zero4 = jnp.zeros((GDN_HEADS,), F32)
    a_log = jnp.concatenate([zero4, p['gdn_a_log'][l]])
    dt_b = jnp.concatenate([zero4, p['gdn_dt_bias'][l]])
    w['gdn_prow'] = _pad_cols(jnp.stack([a_log, dt_b]), LANES)
    w['gdn_pcol'] = jnp.stack([a_log, dt_b], axis=1)
    w['gdn_norm_g'] = row(p['gdn_norm_g'][l])
    w['rwkv_w'] = _rw_pad(w_in[:, _O_RW:_O_CQ]).astype(BF16)
    w['rwkv_mu'] = _rw_pad(p['rwkv_mu'][l]).reshape(1, -1)
    w['rwkv_w0'] = row(p['rwkv_w0'][l])
    w['rwkv_w2'] = jnp.pad(p['rwkv_w2'][l], ((0, LANES - RWKV_DECAY_LORA), (0, 0))).astype(BF16)
    w['rwkv_a0'] = row(p['rwkv_a0'][l])
    w['rwkv_a2'] = jnp.pad(p['rwkv_a2'][l], ((0, LANES - RWKV_AAA_LORA), (0, 0))).astype(BF16)
    w['rwkv_g2'] = p['rwkv_g2'][l].astype(BF16)
    w['rwkv_k_k'] = row(p['rwkv_k_k'][l])
    w['rwkv_k_a'] = row(p['rwkv_k_a'][l])
    w['rwkv_r_k'] = row(p['rwkv_r_k'][l])
    w['rwkv_ln_g'] = row(p['rwkv_ln_g'][l])
    w['rwkv_ln_b'] = row(p['rwkv_ln_b'][l])
    hid = jnp.arange(RWKV_W) // RWKV_HEAD
    w['rwkv_e'] = (hid[:, None] == hid[None, :]).astype(BF16)
    w_kr = w_in[:, _O_KR:_O_GATE]
    w['mla_w'] = jnp.concatenate([w_in[:, _O_CQ:_O_KR], _pad_cols(w_kr, LANES),
                                  _pad_cols(_swap_halves(w_kr, MLA_ROPE), LANES)], axis=1).astype(BF16)
    w['mla_q_norm_g'] = row(p['mla_q_norm_g'][l])
    w['mla_kv_norm_g'] = row(p['mla_kv_norm_g'][l])
    w_uq = p['mla_w_uq'][l]
    wq_n = w_uq[:, :, :MLA_NOPE].reshape(MLA_Q_RANK, -1)
    wq_r = w_uq[:, :, MLA_NOPE:].reshape(MLA_Q_RANK, -1)
    w['mla_wq'] = jnp.concatenate([wq_n, wq_r, _swap_halves(wq_r, MLA_ROPE)], axis=1).astype(BF16)
    w['mla_wuk'] = p['mla_w_uk'][l].reshape(MLA_KV_RANK, -1).astype(BF16)
    w['mla_wuv'] = p['mla_w_uv'][l].reshape(MLA_KV_RANK, -1).astype(BF16)
    w['mla_wuk_h'] = jnp.transpose(p['mla_w_uk'][l], (1, 0, 2)).astype(BF16)
    w['mla_wuv_h'] = jnp.transpose(p['mla_w_uv'][l], (1, 0, 2)).astype(BF16)
    w['w_gate'] = w_in[:, _O_GATE:].astype(BF16)
    w['b_gate'] = row(p['b_gate'][l])
    for n in ('w_br_a', 'w_br_b', 'w_br_c', 'w_out', 'w_ffn_up', 'w_ffn_down'):
        w[n] = p[n][l].astype(BF16)
    return w


def _rope_tables(pos):
    half = MLA_ROPE // 2
    freq = ROPE_THETA ** (-jnp.arange(half, dtype=F32) / half)
    ang = pos.astype(F32)[:, None] * freq
    cos = jnp.cos(ang)
    sin = jnp.sin(ang)
    cos2 = jnp.concatenate([cos, cos], axis=1)
    sin2 = jnp.concatenate([-sin, sin], axis=1)
    return cos2, sin2, jnp.tile(cos2, (1, MLA_HEADS)), jnp.tile(sin2, (1, MLA_HEADS))


def _pick_tile(T, pref):
    t = min(T, pref)
    while T % t:
        t //= 2
    return t


def kernel(x_prompt, x_sample, cache_ckv, cache_krope, page_table, state_gdn, state_gdn_conv, state_rwkv, state_rwkv_shift, norm_mix_g, norm_ffn_g, norm_final_g, w_in, b_gate, gdn_conv_w, gdn_a_log, gdn_dt_bias, gdn_norm_g, rwkv_mu, rwkv_w0, rwkv_w2, rwkv_a0, rwkv_a2, rwkv_g2, rwkv_k_k, rwkv_k_a, rwkv_r_k, rwkv_ln_g, rwkv_ln_b, mla_q_norm_g, mla_kv_norm_g, mla_w_uq, mla_w_uk, mla_w_uv, w_br_a, w_br_b, w_br_c, w_out, w_ffn_up, w_ffn_down):
    p = dict(norm_mix_g=norm_mix_g, norm_ffn_g=norm_ffn_g, w_in=w_in, b_gate=b_gate, gdn_conv_w=gdn_conv_w,
             gdn_a_log=gdn_a_log, gdn_dt_bias=gdn_dt_bias, gdn_norm_g=gdn_norm_g, rwkv_mu=rwkv_mu,
             rwkv_w0=rwkv_w0, rwkv_w2=rwkv_w2, rwkv_a0=rwkv_a0, rwkv_a2=rwkv_a2, rwkv_g2=rwkv_g2,
             rwkv_k_k=rwkv_k_k, rwkv_k_a=rwkv_k_a, rwkv_r_k=rwkv_r_k, rwkv_ln_g=rwkv_ln_g,
             rwkv_ln_b=rwkv_ln_b, mla_q_norm_g=mla_q_norm_g, mla_kv_norm_g=mla_kv_norm_g,
             mla_w_uq=mla_w_uq, mla_w_uk=mla_w_uk, mla_w_uv=mla_w_uv, w_br_a=w_br_a, w_br_b=w_br_b,
             w_br_c=w_br_c, w_out=w_out, w_ffn_up=w_ffn_up, w_ffn_down=w_ffn_down)
    depth = w_in.shape[0]
    bp, sp, D = x_prompt.shape
    bs, ss, _ = x_sample.shape
    assert ss == 1, "the sample group decodes one new token per sequence"
    past_len = page_table.shape[1] * cache_ckv.shape[2]
    fin_g = norm_final_g.reshape(1, -1).astype(F32)

    tp = _pick_tile(sp, 256)
    cp = _pick_tile(tp, 64)
    ts = SUBLANES * ((ss + SUBLANES - 1) // SUBLANES)
    tabs_p = _rope_tables(jnp.arange(sp, dtype=jnp.int32))
    tabs_s = _rope_tables(past_len + jnp.arange(ts, dtype=jnp.int32))
    zeros_p = dict(
        gdn=jnp.zeros((bp, GDN_HEADS, GDN_DK, GDN_DV), F32),
        conv=jnp.zeros((bp, GDN_CONV - 1, GDN_CONV_CH), F32),
        rwkv=jnp.zeros((bp, RWKV_HEADS, RWKV_HEAD, RWKV_HEAD), F32),
        shift=jnp.zeros((bp, 1, RW_PAD), F32))

    xp = x_prompt
    xs = jnp.pad(x_sample, ((0, 0), (0, ts - ss), (0, 0)))
    new_p = [[] for _ in range(6)]
    new_s = [[] for _ in range(6)]
    tm_p = _pick_tile(bp * sp, 256)
    for l in range(depth):
        w = _layer_weights(l, p)
        final = l == depth - 1
        oa, s_g, cbuf = _gdn_call(xp, zeros_p['gdn'], zeros_p['conv'], w, Tt=tp, C=cp, t_real=sp)
        ob, s_r, sh = _rwkv_call(xp, zeros_p['rwkv'], zeros_p['shift'], w, Tt=tp, C=cp, t_real=sp)
        ckv, kr, qn, qr, kn, v, krb = _mla_proj_call(xp, w, tabs_p, Tt=tp, prompt=True)
        oc = _flash_call(qn, qr, kn, krb, v, tq=tp)
        for lst, arr in zip(new_p, (ckv, kr, s_g, cbuf, s_r, _rw_unpad(sh[:, 0]))):
            lst.append(arr)
        xp = _merge_ffn_call(xp.reshape(bp * sp, D), oa.reshape(bp * sp, -1), ob.reshape(bp * sp, -1),
                             oc.reshape(bp * sp, -1), w, fin_g, tm=tm_p, final=final).reshape(bp, sp, D)
        oa, s_g, cbuf = _gdn_call(xs, state_gdn[l], state_gdn_conv[l], w, Tt=ts, C=ts, t_real=ss)
        ob, s_r, sh = _rwkv_call(xs, state_rwkv[l], _rw_pad(state_rwkv_shift[l])[:, None], w,
                                 Tt=ts, C=ts, t_real=ss)
        ckv, kr, qn, qr = _mla_proj_call(xs, w, tabs_s, Tt=ts, prompt=False)
        oc = _decode_call(page_table, qn, qr, ckv, kr, cache_ckv, cache_krope, l, w)
        for lst, arr in zip(new_s, (ckv[:, :ss], kr[:, :ss], s_g, cbuf, s_r, _rw_unpad(sh[:, 0]))):
            lst.append(arr)
        xs_real = _merge_ffn_call(xs[:, 0], oa[:, 0], ob[:, 0], oc[:, 0], w, fin_g, tm=bs, final=final)
        xs = jnp.pad(xs_real[:, None], ((0, 0), (0, ts - ss), (0, 0)))
    y_prompt = xp
    y_sample = xs[:, :ss]
    outs_p = [jnp.stack(a) for a in new_p]
    outs_s = [jnp.stack(a) for a in new_s]
    return (y_prompt, y_sample, *outs_p, *outs_s)
```

```python
import functools
import math

import jax
import jax.numpy as jnp
from jax import lax
from jax.experimental import pallas as pl
from jax.experimental.pallas import tpu as pltpu

F32 = jnp.float32
BF16 = jnp.bfloat16

D_MODEL = 1024
GDN_HEADS = 4
GDN_DK = 128
GDN_DV = 128
GDN_CONV = 4
GDN_KEY_W = GDN_HEADS * GDN_DK
GDN_VAL_W = GDN_HEADS * GDN_DV
GDN_CONV_CH = 2 * GDN_KEY_W + GDN_VAL_W
RWKV_HEADS = 8
RWKV_HEAD = 64
RWKV_W = RWKV_HEADS * RWKV_HEAD
RWKV_DECAY_LORA = 64
RWKV_AAA_LORA = 64
RWKV_GATE_LORA = 128
RWKV_COLS = 3 * RWKV_W + RWKV_DECAY_LORA + RWKV_AAA_LORA + RWKV_GATE_LORA
RWKV_LN_EPS = 64e-5
MLA_HEADS = 4
MLA_Q_RANK = 256
MLA_KV_RANK = 256
MLA_NOPE = 128
MLA_ROPE = 64
MLA_V = 128
MLA_SCALE = 1.0 / math.sqrt(MLA_NOPE + MLA_ROPE)
ROPE_THETA = 10000.0
D_FF = ((8 * D_MODEL // 3 + 255) // 256) * 256
NORM_EPS = 1e-6
L2_EPS = 1e-6

LANES = 128
SUBLANES = 8
VMEM_LIMIT = 56 * 1024 * 1024

_O_QKV = 0
_O_BETA = _O_QKV + GDN_CONV_CH
_O_ALPHA = _O_BETA + GDN_HEADS
_O_Z = _O_ALPHA + GDN_HEADS
_O_RW = _O_Z + GDN_VAL_W
_O_CQ = _O_RW + RWKV_COLS
_O_CKV = _O_CQ + MLA_Q_RANK
_O_KR = _O_CKV + MLA_KV_RANK
_O_GATE = _O_KR + MLA_ROPE

RW_PAD = 3 * RWKV_W + 3 * LANES
PRE_ROW0 = SUBLANES
QK_HEAD = MLA_NOPE + LANES


def _bdot(a, b):
    return jnp.dot(a.astype(BF16), b.astype(BF16), preferred_element_type=F32)


def _bdot_nt(a, b):
    return lax.dot_general(a.astype(BF16), b.astype(BF16), (((1,), (1,)), ((), ())),
                           preferred_element_type=F32)


def _hdot(a, b):
    return jnp.dot(a, b, precision=lax.Precision.HIGHEST, preferred_element_type=F32)


def _split_dot(a, b_exact, parts):
    acc = None
    rem = a
    for _ in range(parts):
        hi = rem.astype(BF16)
        t = jnp.dot(hi, b_exact, preferred_element_type=F32)
        acc = t if acc is None else acc + t
        rem = rem - hi.astype(F32)
    return acc


def _rms(x, g, eps=NORM_EPS):
    return x * lax.rsqrt(jnp.mean(x * x, axis=-1, keepdims=True) + eps) * g


def _sigmoid(x):
    return 1.0 / (1.0 + jnp.exp(-x))


def _softplus(x):
    return jnp.maximum(x, 0.0) + jnp.log(1.0 + jnp.exp(-jnp.abs(x)))


def _tri_masks(C):
    r = lax.broadcasted_iota(jnp.int32, (C, C), 0)
    c = lax.broadcasted_iota(jnp.int32, (C, C), 1)
    masks = []
    k = 0
    while (1 << k) < C:
        rr = r >> k
        cc = c >> k
        m = jnp.where((rr ^ cc) == 1, jnp.where((rr & 1) == 1, 1.0, 0.0), 0.0)
        masks.append(m.astype(F32))
        k += 1
    return r, c, masks


def _unit_lower_inverses(Ls, masks, eye):
    Ts = [eye - L * masks[0] for L in Ls]
    for m in masks[1:]:
        tmp = [_bdot(L * m, T) for L, T in zip(Ls, Ts)]
        Ts = [T - _bdot(T, t) for T, t in zip(Ts, tmp)]
    return Ts


def _chunk_tri(R, C):
    rt = lax.broadcasted_iota(jnp.int32, (R, R), 0)
    ct = lax.broadcasted_iota(jnp.int32, (R, R), 1)
    sh = C.bit_length() - 1
    same = (rt >> sh) == (ct >> sh)
    tri = jnp.where(same, jnp.where(rt >= ct, 1.0, 0.0), 0.0).astype(F32)
    tri_t = jnp.where(same, jnp.where(ct >= rt, 1.0, 0.0), 0.0).astype(F32)
    return tri, tri_t


def _gdn_kernel(x_ref, s0_ref, hist_ref, ng_ref, wg_ref, wbat_ref, cw_ref, prow_ref, pcol_ref, gng_ref,
                o_ref, s_ref, cb_ref, pre_scr, *, Bb, Tt, C, t_real, t_total):
    ti = pl.program_id(1)
    R = Bb * Tt

    @pl.when(ti == 0)
    def _():
        s_ref[...] = s0_ref[...]
        pre_scr[:, PRE_ROW0 - 3:PRE_ROW0, :] = hist_ref[...]

    h = _rms(x_ref[...].reshape(R, D_MODEL), ng_ref[...]).astype(BF16)
    proj = jnp.dot(h, wg_ref[...], preferred_element_type=F32)
    ba_row = lax.dot_general(wbat_ref[...], h, (((1,), (1,)), ((), ())),
                             preferred_element_type=F32)

    cw = cw_ref[...]
    real_rows = min(Tt, t_real)
    ys = []
    for bb in range(Bb):
        pre_scr[bb, PRE_ROW0:PRE_ROW0 + Tt, :] = proj[bb * Tt:(bb + 1) * Tt, :GDN_CONV_CH]
        y = pre_scr[bb, PRE_ROW0 - 3:PRE_ROW0 - 3 + Tt, :] * cw[0:1]
        for i in range(1, GDN_CONV):
            y = y + pre_scr[bb, PRE_ROW0 - 3 + i:PRE_ROW0 - 3 + i + Tt, :] * cw[i:i + 1]
        ys.append(y)
        carry = pre_scr[bb, PRE_ROW0 - 3 + real_rows:PRE_ROW0 + real_rows, :]
        pre_scr[bb, PRE_ROW0 - 3:PRE_ROW0, :] = carry
        cb_ref[bb] = carry
    y = ys[0] if Bb == 1 else jnp.concatenate(ys, axis=0)
    qkv = y * _sigmoid(y)

    ba = proj[:, GDN_CONV_CH + GDN_VAL_W:]
    prow = prow_ref[...]
    beta_col = _sigmoid(ba)
    loga_col = -jnp.exp(prow[0:1]) * _softplus(ba + prow[1:2])
    pcol = pcol_ref[...]
    loga_row = -jnp.exp(pcol[:, 0:1]) * _softplus(ba_row + pcol[:, 1:2])
    if t_real < t_total:
        tcol = (lax.broadcasted_iota(jnp.int32, (R, 1), 0) & (Tt - 1)) + ti * Tt
        trow = (lax.broadcasted_iota(jnp.int32, (1, R), 1) & (Tt - 1)) + ti * Tt
        beta_col = jnp.where(tcol < t_real, beta_col, 0.0)
        loga_col = jnp.where(tcol < t_real, loga_col, 0.0)
        loga_row = jnp.where(trow < t_real, loga_row, 0.0)

    tri, tri_t = _chunk_tri(R, C)
    gcol = _hdot(tri, loga_col)
    grow = _hdot(loga_row, tri_t)

    r, c, masks = _tri_masks(C)
    eye = jnp.where(r == c, 1.0, 0.0).astype(F32)
    incl = r >= c
    strict = r > c
    z = proj[:, GDN_CONV_CH:GDN_CONV_CH + GDN_VAL_W]
    gng = gng_ref[...]
    scale = GDN_DK ** -0.5

    qs, ks = [], []
    for hh in range(GDN_HEADS):
        qh = qkv[:, hh * GDN_DK:(hh + 1) * GDN_DK]
        kh = qkv[:, GDN_KEY_W + hh * GDN_DK:GDN_KEY_W + (hh + 1) * GDN_DK]
        qs.append(qh * lax.rsqrt(jnp.sum(qh * qh, axis=-1, keepdims=True) + L2_EPS) * scale)
        ks.append(kh * lax.rsqrt(jnp.sum(kh * kh, axis=-1, keepdims=True) + L2_EPS))

    nchunk = Tt // C
    bodies = [(bb, n, hh) for n in range(nchunk) for bb in range(Bb) for hh in range(GDN_HEADS)]
    Ls, pre = [], []
    for (bb, n, hh) in bodies:
        r0 = bb * Tt + n * C
        q = qs[hh][r0:r0 + C]
        k = ks[hh][r0:r0 + C]
        v = qkv[r0:r0 + C, 2 * GDN_KEY_W + hh * GDN_DV:2 * GDN_KEY_W + (hh + 1) * GDN_DV]
        bc = beta_col[r0:r0 + C, hh:hh + 1]
        gc = gcol[r0:r0 + C, GDN_HEADS + hh:GDN_HEADS + hh + 1]
        gr = grow[GDN_HEADS + hh:GDN_HEADS + hh + 1, r0:r0 + C]
        decay = jnp.where(incl, jnp.exp(jnp.where(incl, gc - gr, 0.0)), 0.0)
        eg = jnp.exp(gc)
        kb = k * bc
        kq = _bdot_nt(jnp.concatenate([kb, q], axis=0), k)
        Ls.append(jnp.where(strict, kq[:C] * decay, 0.0))
        qk = jnp.where(incl, kq[C:] * decay, 0.0)
        g_last = gc[C - 1:C, :]
        kd = k * jnp.exp(g_last - gc)
        pre.append((q * eg, qk, jnp.concatenate([v * bc, kb * eg], axis=1), kd, jnp.exp(g_last)))
    Ts = _unit_lower_inverses(Ls, masks, eye)
    uws = [_bdot(T, p[2]) for T, p in zip(Ts, pre)]
    kuws = [_bdot(p[3].T, uw) for p, uw in zip(pre, uws)]
    quws = [_bdot(p[1], uw) for p, uw in zip(pre, uws)]

    for i, (bb, n, hh) in enumerate(bodies):
        S = s_ref[bb, hh]
        qeg, _, _, _, egl = pre[i]
        o = _bdot(qeg - quws[i][:, GDN_DV:], S) + quws[i][:, :GDN_DV]
        s_ref[bb, hh] = S * egl - _bdot(kuws[i][:, GDN_DV:], S) + kuws[i][:, :GDN_DV]
        r0 = bb * Tt + n * C
        zh = z[r0:r0 + C, hh * GDN_DV:(hh + 1) * GDN_DV]
        o_ref[bb, n * C:(n + 1) * C, hh * GDN_DV:(hh + 1) * GDN_DV] = _rms(o, gng) * (zh * _sigmoid(zh))


def _gdn_call(x, s0, hist, w, *, Bb, Tt, C, t_real):
    B, T, D = x.shape
    assert Tt & (Tt - 1) == 0 and C & (C - 1) == 0 and B % Bb == 0 and T % Tt == 0 and Tt % C == 0
    kern = functools.partial(_gdn_kernel, Bb=Bb, Tt=Tt, C=C, t_real=t_real, t_total=T)
    full = lambda a: pl.BlockSpec(a.shape, lambda b, t: (0,) * a.ndim)
    return pl.pallas_call(
        kern,
        grid=(B // Bb, T // Tt),
        in_specs=[
            pl.BlockSpec((Bb, Tt, D), lambda b, t: (b, t, 0)),
            pl.BlockSpec((Bb, GDN_HEADS, GDN_DK, GDN_DV), lambda b, t: (b, 0, 0, 0)),
            pl.BlockSpec((Bb, GDN_CONV - 1, GDN_CONV_CH), lambda b, t: (b, 0, 0)),
            full(w['norm_mix_g']), full(w['gdn_w']), full(w['gdn_wba_t']), full(w['gdn_conv_w']),
            full(w['gdn_prow']), full(w['gdn_pcol']), full(w['gdn_norm_g']),
        ],
        out_specs=[
            pl.BlockSpec((Bb, Tt, GDN_VAL_W), lambda b, t: (b, t, 0)),
            pl.BlockSpec((Bb, GDN_HEADS, GDN_DK, GDN_DV), lambda b, t: (b, 0, 0, 0)),
            pl.BlockSpec((Bb, GDN_CONV - 1, GDN_CONV_CH), lambda b, t: (b, 0, 0)),
        ],
        out_shape=[
            jax.ShapeDtypeStruct((B, T, GDN_VAL_W), F32),
            jax.ShapeDtypeStruct((B, GDN_HEADS, GDN_DK, GDN_DV), F32),
            jax.ShapeDtypeStruct((B, GDN_CONV - 1, GDN_CONV_CH), F32),
        ],
        scratch_shapes=[pltpu.VMEM((Bb, PRE_ROW0 + Tt, GDN_CONV_CH), F32)],
        compiler_params=pltpu.CompilerParams(
            dimension_semantics=("arbitrary", "arbitrary"), vmem_limit_bytes=VMEM_LIMIT),
        name="gdn_mixer",
    )(x, s0, hist, w['norm_mix_g'], w['gdn_w'], w['gdn_wba_t'], w['gdn_conv_w'],
      w['gdn_prow'], w['gdn_pcol'], w['gdn_norm_g'])


def _rwkv_kernel(x_ref, s0_ref, sh_ref, ng_ref, wr_ref, mu_ref, w0_ref, w2_ref, a0_ref, a2_ref, g2_ref,
                 kk_ref, ka_ref, rk_ref, lng_ref, lnb_ref, e_ref,
                 o_ref, s_ref, sho_ref, pre_scr, y_scr, *, Bb, Tt, C, t_real, t_total):
    ti = pl.program_id(1)
    N = RWKV_HEAD
    R = Bb * Tt

    @pl.when(ti == 0)
    def _():
        s_ref[...] = s0_ref[...]
        pre_scr[:, PRE_ROW0 - 1:PRE_ROW0, :] = sh_ref[...]

    h = _rms(x_ref[...].reshape(R, D_MODEL), ng_ref[...]).astype(BF16)
    pre = jnp.dot(h, wr_ref[...], preferred_element_type=F32)
    real_rows = min(Tt, t_real)
    prevs = []
    for bb in range(Bb):
        pre_scr[bb, PRE_ROW0:PRE_ROW0 + Tt, :] = pre[bb * Tt:(bb + 1) * Tt]
        prevs.append(pre_scr[bb, PRE_ROW0 - 1:PRE_ROW0 - 1 + Tt, :])
        last = pre_scr[bb, PRE_ROW0 - 1 + real_rows:PRE_ROW0 + real_rows, :]
        pre_scr[bb, PRE_ROW0 - 1:PRE_ROW0, :] = last
        sho_ref[bb] = last
    prev = prevs[0] if Bb == 1 else jnp.concatenate(prevs, axis=0)

    xm = pre + (prev - pre) * mu_ref[...]
    W = RWKV_W
    rr = xm[:, 0:W]
    kx = xm[:, W:2 * W]
    vv = xm[:, 2 * W:3 * W]
    wlo = xm[:, 3 * W:3 * W + LANES]
    alo = xm[:, 3 * W + LANES:3 * W + 2 * LANES]
    glo = xm[:, 3 * W + 2 * LANES:3 * W + 3 * LANES]
    w_log = -_softplus(-(w0_ref[...] + _bdot(jnp.tanh(wlo), w2_ref[...]))) - 0.5
    logd = -jnp.exp(w_log)
    a = _sigmoid(a0_ref[...] + _bdot(alo, a2_ref[...]))
    gb = _bdot(_sigmoid(glo), g2_ref[...])
    E = e_ref[...]
    kkr = kx * kk_ref[...]
    kk = kkr * lax.rsqrt(_split_dot(kkr * kkr, E, 2) + L2_EPS)
    kb = kx * (1.0 + (a - 1.0) * ka_ref[...])
    a_eff = a
    v_eff = vv
    if t_real < t_total:
        tcol = (lax.broadcasted_iota(jnp.int32, (R, 1), 0) & (Tt - 1)) + ti * Tt
        valid = tcol < t_real
        logd = jnp.where(valid, logd, 0.0)
        a_eff = jnp.where(valid, a, 0.0)
        v_eff = jnp.where(valid, vv, 0.0)

    tri, _ = _chunk_tri(R, C)
    logG = _hdot(tri, logd)
    G = jnp.exp(logG)
    Ginv = jnp.exp(-logG)
    aq_all = kk * jnp.exp(logG - logd)
    bk_all = -(a_eff * kk) * Ginv
    kd_all = kb * Ginv
    rq_all = rr * G

    r, c, masks = _tri_masks(C)
    eye = jnp.where(r == c, 1.0, 0.0).astype(F32)
    incl = r >= c
    strict = r > c

    nchunk = Tt // C
    bodies = [(bb, n, hh) for n in range(nchunk) for bb in range(Bb) for hh in range(RWKV_HEADS)]
    nb = len(bodies)
    aqs, bks, kds, rqs, vhs, gls = [], [], [], [], [], []
    for (bb, n, hh) in bodies:
        r0 = bb * Tt + n * C
        ls = slice(hh * N, (hh + 1) * N)
        aqs.append(aq_all[r0:r0 + C, ls])
        bks.append(bk_all[r0:r0 + C, ls])
        kds.append(kd_all[r0:r0 + C, ls])
        rqs.append(rq_all[r0:r0 + C, ls])
        vhs.append(v_eff[r0:r0 + C, ls])
        gls.append(G[r0 + C - 1:r0 + C, ls])
    ars = [jnp.concatenate([aqs[i], rqs[i]], axis=0) for i in range(nb)]
    sbs = [_bdot_nt(ars[i], bks[i]) for i in range(nb)]
    sks = [_bdot_nt(ars[i], kds[i]) for i in range(nb)]
    Ls = [jnp.where(strict, -sbs[i][:C], 0.0) for i in range(nb)]
    ras = [jnp.where(incl, sbs[i][C:], 0.0) for i in range(nb)]
    rks = [jnp.where(incl, sks[i][C:], 0.0) for i in range(nb)]
    bmvs = [_bdot(jnp.where(strict, sks[i][:C], 0.0), vhs[i]) for i in range(nb)]
    Ts = _unit_lower_inverses(Ls, masks, eye)
    TAs = [_bdot(Ts[i], aqs[i]) for i in range(nb)]
    TBVs = [_bdot(Ts[i], bmvs[i]) for i in range(nb)]
    Xs = [bks[i] * gls[i] for i in range(nb)]
    Zs = [kds[i] * gls[i] for i in range(nb)]
    Q1s = [_bdot(TAs[i].T, Xs[i]) for i in range(nb)]
    M0s = [_bdot(jnp.concatenate([TBVs[i], vhs[i]], axis=0).T, jnp.concatenate([Xs[i], Zs[i]], axis=0))
           for i in range(nb)]
    rqps = [rqs[i] + _bdot(ras[i], TAs[i]) for i in range(nb)]
    y0s = [_bdot(jnp.concatenate([ras[i], rks[i]], axis=1), jnp.concatenate([TBVs[i], vhs[i]], axis=0))
           for i in range(nb)]

    per_chunk = Bb * RWKV_HEADS
    for n in range(nchunk):
        idx = range(n * per_chunk, (n + 1) * per_chunk)
        Ss = {i: s_ref[bodies[i][0], bodies[i][2]] for i in idx}
        for i in idx:
            bb, _, hh = bodies[i]
            r0 = bb * Tt + n * C
            y_scr[r0:r0 + C, hh * N:(hh + 1) * N] = _bdot_nt(rqps[i], Ss[i]) + y0s[i]
        for i in idx:
            bb, _, hh = bodies[i]
            s_ref[bb, hh] = Ss[i] * gls[i] + _bdot(Ss[i], Q1s[i]) + M0s[i]

    y = y_scr[...]
    inv_n = 1.0 / N
    mu = _split_dot(y, E, 2) * inv_n
    yc = y - mu
    var = _split_dot(yc * yc, E, 2) * inv_n
    yn = yc * lax.rsqrt(var + RWKV_LN_EPS) * lng_ref[...] + lnb_ref[...]
    bonus = _split_dot(rr * kb * rk_ref[...], E, 2) * vv
    o_ref[...] = ((yn + bonus) * gb).reshape(Bb, Tt, RWKV_W)


def _rwkv_call(x, s0, shift, w, *, Bb, Tt, C, t_real):
    B, T, D = x.shape
    assert Tt & (Tt - 1) == 0 and C & (C - 1) == 0 and B % Bb == 0 and T % Tt == 0 and Tt % C == 0
    kern = functools.partial(_rwkv_kernel, Bb=Bb, Tt=Tt, C=C, t_real=t_real, t_total=T)
    full = lambda a: pl.BlockSpec(a.shape, lambda b, t: (0,) * a.ndim)
    names = ['norm_mix_g', 'rwkv_w', 'rwkv_mu', 'rwkv_w0', 'rwkv_w2', 'rwkv_a0', 'rwkv_a2', 'rwkv_g2',
             'rwkv_k_k', 'rwkv_k_a', 'rwkv_r_k', 'rwkv_ln_g', 'rwkv_ln_b', 'rwkv_e']
    return pl.pallas_call(
        kern,
        grid=(B // Bb, T // Tt),
        in_specs=[
            pl.BlockSpec((Bb, Tt, D), lambda b, t: (b, t, 0)),
            pl.BlockSpec((Bb, RWKV_HEADS, RWKV_HEAD, RWKV_HEAD), lambda b, t: (b, 0, 0, 0)),
            pl.BlockSpec((Bb, 1, RW_PAD), lambda b, t: (b, 0, 0)),
        ] + [full(w[n]) for n in names],
        out_specs=[
            pl.BlockSpec((Bb, Tt, RWKV_W), lambda b, t: (b, t, 0)),
            pl.BlockSpec((Bb, RWKV_HEADS, RWKV_HEAD, RWKV_HEAD), lambda b, t: (b, 0, 0, 0)),
            pl.BlockSpec((Bb, 1, RW_PAD), lambda b, t: (b, 0, 0)),
        ],
        out_shape=[
            jax.ShapeDtypeStruct((B, T, RWKV_W), F32),
            jax.ShapeDtypeStruct((B, RWKV_HEADS, RWKV_HEAD, RWKV_HEAD), F32),
            jax.ShapeDtypeStruct((B, 1, RW_PAD), F32),
        ],
        scratch_shapes=[pltpu.VMEM((Bb, PRE_ROW0 + Tt, RW_PAD), F32), pltpu.VMEM((Bb * Tt, RWKV_W), F32)],
        compiler_params=pltpu.CompilerParams(
            dimension_semantics=("arbitrary", "arbitrary"), vmem_limit_bytes=VMEM_LIMIT),
        name="rwkv_mixer",
    )(x, s0, shift, *[w[n] for n in names])


def _mla_proj_kernel(x_ref, ng_ref, wm_ref, qg_ref, kvg_ref, wq_ref, cos_ref, sin_ref, *rest, prompt):
    if prompt:
        wuk_ref, wuv_ref, ckv_ref, kr_ref, q_ref, k_ref, v_ref = rest
    else:
        ckv_ref, kr_ref, qn_ref, qr_ref = rest
    h = _rms(x_ref[...], ng_ref[...]).astype(BF16)
    p = jnp.dot(h, wm_ref[...], preferred_element_type=F32)
    cqn = _rms(p[:, :MLA_Q_RANK], qg_ref[...]).astype(BF16)
    ckv = _rms(p[:, MLA_Q_RANK:MLA_Q_RANK + MLA_KV_RANK], kvg_ref[...])
    o = MLA_Q_RANK + MLA_KV_RANK
    cos = cos_ref[...]
    sin = sin_ref[...]
    krp = p[:, o:o + LANES] * cos + p[:, o + LANES:o + 2 * LANES] * sin
    ckv_ref[...] = ckv
    kr_ref[...] = krp[:, :MLA_ROPE]
    q = jnp.dot(cqn, wq_ref[...], preferred_element_type=F32)
    wn = MLA_HEADS * MLA_NOPE
    wr = MLA_HEADS * LANES
    cos4 = jnp.concatenate([cos] * MLA_HEADS, axis=1)
    sin4 = jnp.concatenate([sin] * MLA_HEADS, axis=1)
    qn = q[:, :wn] * MLA_SCALE
    qr = (q[:, wn:wn + wr] * cos4 + q[:, wn + wr:wn + 2 * wr] * sin4) * MLA_SCALE
    if prompt:
        cb = ckv.astype(BF16)
        kn = jnp.dot(cb, wuk_ref[...], preferred_element_type=F32)
        qparts, kparts = [], []
        for hh in range(MLA_HEADS):
            qparts += [qn[:, hh * MLA_NOPE:(hh + 1) * MLA_NOPE], qr[:, hh * LANES:(hh + 1) * LANES]]
            kparts += [kn[:, hh * MLA_NOPE:(hh + 1) * MLA_NOPE], krp]
        q_ref[...] = jnp.concatenate(qparts, axis=1).astype(BF16)
        k_ref[...] = jnp.concatenate(kparts, axis=1).astype(BF16)
        v_ref[...] = jnp.dot(cb, wuv_ref[...], preferred_element_type=F32).astype(BF16)
    else:
        qn_ref[...] = qn
        qr_ref[...] = qr


def _mla_proj_call(x2, w, cos, sin, *, tr, prompt):
    M, D = x2.shape
    nt = cos.shape[0] // tr
    full = lambda a: pl.BlockSpec(a.shape, lambda i: (0,) * a.ndim)
    tab = pl.BlockSpec((tr, LANES), lambda i: (i % nt, 0))
    tok = lambda n: pl.BlockSpec((tr, n), lambda i: (i, 0))
    ins = [x2, w['norm_mix_g'], w['mla_w'], w['mla_q_norm_g'], w['mla_kv_norm_g'], w['mla_wq'], cos, sin]
    in_specs = [tok(D)] + [full(a) for a in ins[1:6]] + [tab, tab]
    out_specs = [tok(MLA_KV_RANK), tok(MLA_ROPE)]
    out_shape = [jax.ShapeDtypeStruct((M, MLA_KV_RANK), F32), jax.ShapeDtypeStruct((M, MLA_ROPE), F32)]
    if prompt:
        ins += [w['mla_wuk'], w['mla_wuv']]
        in_specs += [full(w['mla_wuk']), full(w['mla_wuv'])]
        out_specs += [tok(MLA_HEADS * QK_HEAD), tok(MLA_HEADS * QK_HEAD), tok(MLA_HEADS * MLA_V)]
        out_shape += [jax.ShapeDtypeStruct((M, MLA_HEADS * QK_HEAD), BF16),
                      jax.ShapeDtypeStruct((M, MLA_HEADS * QK_HEAD), BF16),
                      jax.ShapeDtypeStruct((M, MLA_HEADS * MLA_V), BF16)]
    else:
        out_specs += [tok(MLA_HEADS * MLA_NOPE), tok(MLA_HEADS * LANES)]
        out_shape += [jax.ShapeDtypeStruct((M, MLA_HEADS * MLA_NOPE), F32),
                      jax.ShapeDtypeStruct((M, MLA_HEADS * LANES), F32)]
    return pl.pallas_call(
        functools.partial(_mla_proj_kernel, prompt=prompt),
        grid=(M // tr,),
        in_specs=in_specs, out_specs=out_specs, out_shape=out_shape,
        compiler_params=pltpu.CompilerParams(dimension_semantics=("arbitrary",), vmem_limit_bytes=VMEM_LIMIT),
        name="mla_proj_prompt" if prompt else "mla_proj_sample",
    )(*ins)


def _flash_kernel(q_ref, k_ref, v_ref, o_ref, m_scr, l_scr, acc_scr, *, tq):
    qi = pl.program_id(1)
    ki = pl.program_id(2)

    @pl.when(ki == 0)
    def _():
        m_scr[...] = jnp.full(m_scr.shape, -jnp.inf, F32)
        l_scr[...] = jnp.zeros(l_scr.shape, F32)
        acc_scr[...] = jnp.zeros(acc_scr.shape, F32)

    @pl.when(ki <= qi)
    def _():
        row = lax.broadcasted_iota(jnp.int32, (tq, tq), 0)
        col = lax.broadcasted_iota(jnp.int32, (tq, tq), 1)
        keep = col <= row + (qi - ki) * tq
        for hh in range(MLA_HEADS):
            lq = slice(hh * QK_HEAD, (hh + 1) * QK_HEAD)
            lv = slice(hh * MLA_V, (hh + 1) * MLA_V)
            s = lax.dot_general(q_ref[0, :, lq], k_ref[0, :, lq], (((1,), (1,)), ((), ())),
                                preferred_element_type=F32)
            s = jnp.where(keep, s, -jnp.inf)
            m_old = m_scr[hh]
            m_new = jnp.maximum(m_old, jnp.max(s, axis=-1, keepdims=True))
            alpha = jnp.exp(m_old - m_new)
            p = jnp.exp(s - m_new)
            l_scr[hh] = alpha * l_scr[hh] + jnp.sum(p, axis=-1, keepdims=True)
            acc_scr[:, lv] = alpha * acc_scr[:, lv] + jnp.dot(p.astype(BF16), v_ref[0, :, lv],
                                                              preferred_element_type=F32)
            m_scr[hh] = m_new

    @pl.when(ki == qi)
    def _():
        for hh in range(MLA_HEADS):
            lv = slice(hh * MLA_V, (hh + 1) * MLA_V)
            o_ref[0, :, lv] = acc_scr[:, lv] / l_scr[hh]


def _flash_call(q, k, v, *, tq):
    B, T, _ = q.shape
    nq = T // tq
    qspec = lambda n: pl.BlockSpec((1, tq, n), lambda b, i, j: (b, i, 0))
    kspec = lambda n: pl.BlockSpec((1, tq, n), lambda b, i, j: (b, jnp.minimum(i, j), 0))
    return pl.pallas_call(
        functools.partial(_flash_kernel, tq=tq),
        grid=(B, nq, nq),
        in_specs=[qspec(q.shape[2]), kspec(k.shape[2]), kspec(v.shape[2])],
        out_specs=pl.BlockSpec((1, tq, v.shape[2]), lambda b, i, j: (b, i, 0)),
        out_shape=jax.ShapeDtypeStruct((B, T, v.shape[2]), F32),
        scratch_shapes=[pltpu.VMEM((MLA_HEADS, tq, 1), F32), pltpu.VMEM((MLA_HEADS, tq, 1), F32),
                        pltpu.VMEM((tq, MLA_HEADS * MLA_V), F32)],
        compiler_params=pltpu.CompilerParams(
            dimension_semantics=("arbitrary", "arbitrary", "arbitrary"), vmem_limit_bytes=VMEM_LIMIT),
        name="mla_flash",
    )(q, k, v)


def _decode_kernel(pt_ref, qn_ref, qr_ref, ckvn_ref, krn_ref, wuk_ref, wuv_ref, *rest, G):
    ckv_refs = rest[:G]
    krt_refs = rest[G:2 * G]
    o_ref = rest[2 * G]
    qlat_scr, qrm_scr, m_scr, l_scr, acc_scr = rest[2 * G + 1:]
    j = pl.program_id(1)
    rid = lax.broadcasted_iota(jnp.int32, (SUBLANES, 1), 0)

    @pl.when(j == 0)
    def _():
        qn = qn_ref[0]
        qr = qr_ref[0]
        qlat = jnp.zeros((SUBLANES, MLA_KV_RANK), F32)
        qrm = jnp.zeros((SUBLANES, MLA_ROPE), F32)
        for hh in range(MLA_HEADS):
            ql = _bdot_nt(qn[:, hh * MLA_NOPE:(hh + 1) * MLA_NOPE], wuk_ref[hh])
            qlat = jnp.where(rid == hh, ql[0:1, :], qlat)
            qrm = jnp.where(rid == hh, qr[0:1, hh * LANES:hh * LANES + MLA_ROPE], qrm)
        qlat_scr[...] = qlat
        qrm_scr[...] = qrm
        m_scr[...] = jnp.full(m_scr.shape, -jnp.inf, F32)
        l_scr[...] = jnp.zeros(l_scr.shape, F32)
        acc_scr[...] = jnp.zeros(acc_scr.shape, F32)

    qlat = qlat_scr[...].astype(BF16)
    qrm = qrm_scr[...].astype(BF16)
    pages = [ckv_refs[i][...].astype(BF16) for i in range(G)]
    s = jnp.concatenate(
        [_bdot_nt(qlat, pages[i]) + _bdot(qrm, krt_refs[i][...]) for i in range(G)], axis=1)
    m_old = m_scr[...]
    m_new = jnp.maximum(m_old, jnp.max(s, axis=-1, keepdims=True))
    alpha = jnp.exp(m_old - m_new)
    p = jnp.exp(s - m_new)
    l_new = alpha * l_scr[...] + jnp.sum(p, axis=-1, keepdims=True)
    acc = alpha * acc_scr[...]
    page = pages[0].shape[0]
    for i in range(G):
        acc = acc + jnp.dot(p[:, i * page:(i + 1) * page].astype(BF16), pages[i], preferred_element_type=F32)
    m_scr[...] = m_new
    l_scr[...] = l_new
    acc_scr[...] = acc

    @pl.when(j == pl.num_programs(1) - 1)
    def _():
        ckvn = ckvn_ref[0][0:1, :]
        krn = krn_ref[0][0:1, :]
        ql = qlat_scr[...]
        qr_ = qrm_scr[...]
        s_new = (jnp.sum(ql * ckvn, axis=-1, keepdims=True) + jnp.sum(qr_ * krn, axis=-1, keepdims=True))
        m_fin = jnp.maximum(m_new, s_new)
        a2 = jnp.exp(m_new - m_fin)
        p_new = jnp.exp(s_new - m_fin)
        l_fin = a2 * l_new + p_new
        o_lat = (a2 * acc + p_new * ckvn) / l_fin
        outs = []
        for hh in range(MLA_HEADS):
            oh = _bdot(o_lat, wuv_ref[hh])
            outs.append(oh[hh:hh + 1, :])
        o_ref[0] = jnp.broadcast_to(jnp.concatenate(outs, axis=1), (SUBLANES, MLA_HEADS * MLA_V))


def _decode_call(page_table, qn, qr, ckvn, krn, cache_ckv, cache_krope_t, layer, w):
    B = qn.shape[0]
    n_pages = page_table.shape[1]
    page = cache_ckv.shape[2]
    G = 16
    while n_pages % G:
        G //= 2
    tok = lambda a: pl.BlockSpec((1,) + a.shape[1:], lambda b, j, pt: (b, 0, 0))
    full = lambda a: pl.BlockSpec(a.shape, lambda b, j, pt: (0,) * a.ndim)

    def cache_spec(shape, i):
        return pl.BlockSpec((None, None) + shape, lambda b, j, pt: (layer, pt[b, j * G + i], 0, 0))

    in_specs = ([tok(qn), tok(qr), tok(ckvn), tok(krn), full(w['mla_wuk_h']), full(w['mla_wuv_h'])]
                + [cache_spec((page, MLA_KV_RANK), i) for i in range(G)]
                + [cache_spec((MLA_ROPE, page), i) for i in range(G)])
    grid_spec = pltpu.PrefetchScalarGridSpec(
        num_scalar_prefetch=1,
        grid=(B, n_pages // G),
        in_specs=in_specs,
        out_specs=pl.BlockSpec((1, SUBLANES, MLA_HEADS * MLA_V), lambda b, j, pt: (b, 0, 0)),
        scratch_shapes=[pltpu.VMEM((SUBLANES, MLA_KV_RANK), F32), pltpu.VMEM((SUBLANES, MLA_ROPE), F32),
                        pltpu.VMEM((SUBLANES, 1), F32), pltpu.VMEM((SUBLANES, 1), F32),
                        pltpu.VMEM((SUBLANES, MLA_KV_RANK), F32)],
    )
    return pl.pallas_call(
        functools.partial(_decode_kernel, G=G),
        grid_spec=grid_spec,
        out_shape=jax.ShapeDtypeStruct((B, SUBLANES, MLA_HEADS * MLA_V), F32),
        compiler_params=pltpu.CompilerParams(
            dimension_semantics=("arbitrary", "arbitrary"), vmem_limit_bytes=VMEM_LIMIT),
        name="mla_decode",
    )(page_table, qn, qr, ckvn, krn, w['mla_wuk_h'], w['mla_wuv_h'],
      *([cache_ckv] * G), *([cache_krope_t] * G))


def _merge_ffn_kernel(x_ref, oa_ref, ob_ref, oc_ref, ng_ref, wgt_ref, bg_ref, wa_ref, wb_ref, wc_ref, wo_ref,
                      nf_ref, wup_ref, wdn_ref, fin_ref, y_ref, *, final):
    x = x_ref[...]
    h = _rms(x, ng_ref[...]).astype(BF16)
    gates = _sigmoid(jnp.dot(h, wgt_ref[...], preferred_element_type=F32) + bg_ref[...])
    D = D_MODEL
    merged = (gates[:, 0:D] * _bdot(oa_ref[...], wa_ref[...])
              + gates[:, D:2 * D] * _bdot(ob_ref[...], wb_ref[...])
              + gates[:, 2 * D:3 * D] * _bdot(oc_ref[...], wc_ref[...]))
    x1 = x + _bdot(merged, wo_ref[...])
    h2 = _rms(x1, nf_ref[...]).astype(BF16)
    up = jnp.dot(h2, wup_ref[...], preferred_element_type=F32)
    g = up[:, :D_FF]
    x2 = x1 + _bdot(g * _sigmoid(g) * up[:, D_FF:], wdn_ref[...])
    y_ref[...] = _rms(x2, fin_ref[...]) if final else x2


def _merge_ffn_call(x, oa, ob, oc, w, fin_g, *, tm, final):
    M, D = x.shape
    const = lambda a: pl.BlockSpec(a.shape, lambda i: (0,) * a.ndim, pipeline_mode=pl.Buffered(1))
    row = lambda n: pl.BlockSpec((tm, n), lambda i: (i, 0))
    names = ['norm_mix_g', 'w_gate', 'b_gate', 'w_br_a', 'w_br_b', 'w_br_c', 'w_out', 'norm_ffn_g',
             'w_ffn_up', 'w_ffn_down']
    return pl.pallas_call(
        functools.partial(_merge_ffn_kernel, final=final),
        grid=(M // tm,),
        in_specs=[row(D), row(oa.shape[1]), row(ob.shape[1]), row(oc.shape[1])]
                 + [const(w[n]) for n in names] + [const(fin_g)],
        out_specs=row(D),
        out_shape=jax.ShapeDtypeStruct((M, D), F32),
        compiler_params=pltpu.CompilerParams(
            dimension_semantics=("arbitrary",), vmem_limit_bytes=VMEM_LIMIT),
        name="merge_ffn",
    )(x, oa, ob, oc, *[w[n] for n in names], fin_g)


def _pad_cols(a, n):
    return jnp.pad(a, ((0, 0), (0, n - a.shape[1])))


def _swap_halves(a, width):
    shp = a.shape
    a = a.reshape(shp[:-1] + (shp[-1] // width, 2, width // 2))
    return a[..., ::-1, :].reshape(shp)


def _pad_groups(a, width, to):
    shp = a.shape
    a = a.reshape(shp[:-1] + (shp[-1] // width, width))
    a = jnp.pad(a, [(0, 0)] * (a.ndim - 1) + [(0, to - width)])
    return a.reshape(shp[:-1] + (-1,))


def _rw_pad(a):
    W = RWKV_W
    z = jnp.zeros(a.shape[:-1] + (LANES - RWKV_DECAY_LORA,), a.dtype)
    return jnp.concatenate([a[..., :3 * W], a[..., 3 * W:3 * W + RWKV_DECAY_LORA], z,
                            a[..., 3 * W + RWKV_DECAY_LORA:3 * W + RWKV_DECAY_LORA + RWKV_AAA_LORA], z,
                            a[..., 3 * W + RWKV_DECAY_LORA + RWKV_AAA_LORA:]], axis=-1)


def _rw_unpad(a):
    W = RWKV_W
    return jnp.concatenate([a[..., :3 * W], a[..., 3 * W:3 * W + RWKV_DECAY_LORA],
                            a[..., 3 * W + LANES:3 * W + LANES + RWKV_AAA_LORA],
                            a[..., 3 * W + 2 * LANES:]], axis=-1)


def _layer_weights(l, p):
    w_in = p['w_in'][l]
    row = lambda a: a.reshape(1, -1).astype(F32)
    w = {}
    w['norm_mix_g'] = row(p['norm_mix_g'][l])
    w['norm_ffn_g'] = row(p['norm_ffn_g'][l])
    ba = w_in[:, _O_BETA:_O_Z]
    w['gdn_w'] = jnp.concatenate([w_in[:, _O_QKV:_O_BETA], w_in[:, _O_Z:_O_RW], _pad_cols(ba, LANES)],
                                 axis=1).astype(BF16)
    w['gdn_wba_t'] = ba.T.astype(BF16)
    w['gdn_conv_w'] = p['gdn_conv_w'][l]
    zero4 = jnp.zeros((GDN_HEADS,), F32)
    a_log = jnp.concatenate([zero4, p['gdn_a_log'][l]])
    dt_b = jnp.concatenate([zero4, p['gdn_dt_bias'][l]])
    w['gdn_prow'] = _pad_cols(jnp.stack([a_log, dt_b]), LANES)
    w['gdn_pcol'] = jnp.stack([a_log, dt_b], axis=1)
    w['gdn_norm_g'] = row(p['gdn_norm_g'][l])
    w['rwkv_w'] = _rw_pad(w_in[:, _O_RW:_O_CQ]).astype(BF16)
    w['rwkv_mu'] = _rw_pad(p['rwkv_mu'][l]).reshape(1, -1)
    w['rwkv_w0'] = row(p['rwkv_w0'][l])
    w['rwkv_w2'] = jnp.pad(p['rwkv_w2'][l], ((0, LANES - RWKV_DECAY_LORA), (0, 0))).astype(BF16)
    w['rwkv_a0'] = row(p['rwkv_a0'][l])
    w['rwkv_a2'] = jnp.pad(p['rwkv_a2'][l], ((0, LANES - RWKV_AAA_LORA), (0, 0))).astype(BF16)
    w['rwkv_g2'] = p['rwkv_g2'][l].astype(BF16)
    w['rwkv_k_k'] = row(p['rwkv_k_k'][l])
    w['rwkv_k_a'] = row(p['rwkv_k_a'][l])
    w['rwkv_r_k'] = row(p['rwkv_r_k'][l])
    w['rwkv_ln_g'] = row(p['rwkv_ln_g'][l])
    w['rwkv_ln_b'] = row(p['rwkv_ln_b'][l])
    hid = jnp.arange(RWKV_W) // RWKV_HEAD
    w['rwkv_e'] = (hid[:, None] == hid[None, :]).astype(BF16)
    w_kr = w_in[:, _O_KR:_O_GATE]
    w['mla_w'] = jnp.concatenate([w_in[:, _O_CQ:_O_KR], _pad_cols(w_kr, LANES),
                                  _pad_cols(_swap_halves(w_kr, MLA_ROPE), LANES)], axis=1).astype(BF16)
    w['mla_q_norm_g'] = row(p['mla_q_norm_g'][l])
    w['mla_kv_norm_g'] = row(p['mla_kv_norm_g'][l])
    w_uq = p['mla_w_uq'][l]
    wq_n = w_uq[:, :, :MLA_NOPE].reshape(MLA_Q_RANK, -1)
    wq_r = w_uq[:, :, MLA_NOPE:].reshape(MLA_Q_RANK, -1)
    w['mla_wq'] = jnp.concatenate([wq_n, _pad_groups(wq_r, MLA_ROPE, LANES),
                                   _pad_groups(_swap_halves(wq_r, MLA_ROPE), MLA_ROPE, LANES)],
                                  axis=1).astype(BF16)
    w['mla_wuk'] = p['mla_w_uk'][l].reshape(MLA_KV_RANK, -1).astype(BF16)
    w['mla_wuv'] = p['mla_w_uv'][l].reshape(MLA_KV_RANK, -1).astype(BF16)
    w['mla_wuk_h'] = jnp.transpose(p['mla_w_uk'][l], (1, 0, 2)).astype(BF16)
    w['mla_wuv_h'] = jnp.transpose(p['mla_w_uv'][l], (1, 0, 2)).astype(BF16)
    w['w_gate'] = w_in[:, _O_GATE:].astype(BF16)
    w['b_gate'] = row(p['b_gate'][l])
    for n in ('w_br_a', 'w_br_b', 'w_br_c', 'w_out', 'w_ffn_up', 'w_ffn_down'):
        w[n] = p[n][l].astype(BF16)
    return w


def _rope_tables(pos):
    half = MLA_ROPE // 2
    freq = ROPE_THETA ** (-jnp.arange(half, dtype=F32) / half)
    ang = pos.astype(F32)[:, None] * freq
    cos = jnp.cos(ang)
    sin = jnp.sin(ang)
    return (_pad_cols(jnp.concatenate([cos, cos], axis=1), LANES),
            _pad_cols(jnp.concatenate([-sin, sin], axis=1), LANES))


def _pick_tile(T, pref):
    t = min(T, pref)
    while T % t:
        t //= 2
    return t


def kernel(x_prompt, x_sample, cache_ckv, cache_krope, page_table, state_gdn, state_gdn_conv, state_rwkv, state_rwkv_shift, norm_mix_g, norm_ffn_g, norm_final_g, w_in, b_gate, gdn_conv_w, gdn_a_log, gdn_dt_bias, gdn_norm_g, rwkv_mu, rwkv_w0, rwkv_w2, rwkv_a0, rwkv_a2, rwkv_g2, rwkv_k_k, rwkv_k_a, rwkv_r_k, rwkv_ln_g, rwkv_ln_b, mla_q_norm_g, mla_kv_norm_g, mla_w_uq, mla_w_uk, mla_w_uv, w_br_a, w_br_b, w_br_c, w_out, w_ffn_up, w_ffn_down):
    p = dict(norm_mix_g=norm_mix_g, norm_ffn_g=norm_ffn_g, w_in=w_in, b_gate=b_gate, gdn_conv_w=gdn_conv_w,
             gdn_a_log=gdn_a_log, gdn_dt_bias=gdn_dt_bias, gdn_norm_g=gdn_norm_g, rwkv_mu=rwkv_mu,
             rwkv_w0=rwkv_w0, rwkv_w2=rwkv_w2, rwkv_a0=rwkv_a0, rwkv_a2=rwkv_a2, rwkv_g2=rwkv_g2,
             rwkv_k_k=rwkv_k_k, rwkv_k_a=rwkv_k_a, rwkv_r_k=rwkv_r_k, rwkv_ln_g=rwkv_ln_g,
             rwkv_ln_b=rwkv_ln_b, mla_q_norm_g=mla_q_norm_g, mla_kv_norm_g=mla_kv_norm_g,
             mla_w_uq=mla_w_uq, mla_w_uk=mla_w_uk, mla_w_uv=mla_w_uv, w_br_a=w_br_a, w_br_b=w_br_b,
             w_br_c=w_br_c, w_out=w_out, w_ffn_up=w_ffn_up, w_ffn_down=w_ffn_down)
    depth = w_in.shape[0]
    bp, sp, D = x_prompt.shape
    bs, ss, _ = x_sample.shape
    assert ss == 1, "the sample group decodes one new token per sequence"
    past_len = page_table.shape[1] * cache_ckv.shape[2]
    fin_g = norm_final_g.reshape(1, -1).astype(F32)
    cache_krope_t = jnp.swapaxes(cache_krope, 2, 3)

    tp = _pick_tile(sp, 256)
    cp = _pick_tile(tp, 64)
    tq = _pick_tile(sp, 512)
    ts = SUBLANES
    bb_s = _pick_tile(bs, 8)
    cos_p, sin_p = _rope_tables(jnp.arange(sp, dtype=jnp.int32))
    cos_s, sin_s = _rope_tables(jnp.full((bs * ts,), past_len, dtype=jnp.int32))
    zeros_p = dict(
        gdn=jnp.zeros((bp, GDN_HEADS, GDN_DK, GDN_DV), F32),
        conv=jnp.zeros((bp, GDN_CONV - 1, GDN_CONV_CH), F32),
        rwkv=jnp.zeros((bp, RWKV_HEADS, RWKV_HEAD, RWKV_HEAD), F32),
        shift=jnp.zeros((bp, 1, RW_PAD), F32))

    xp = x_prompt
    xs = jnp.pad(x_sample, ((0, 0), (0, ts - ss), (0, 0)))
    new_p = [[] for _ in range(6)]
    new_s = [[] for _ in range(6)]
    mp = bp * sp
    tm_p = _pick_tile(mp, 256)
    for l in range(depth):
        w = _layer_weights(l, p)
        final = l == depth - 1
        oa, s_g, cbuf = _gdn_call(xp, zeros_p['gdn'], zeros_p['conv'], w, Bb=1, Tt=tp, C=cp, t_real=sp)
        ob, s_r, sh = _rwkv_call(xp, zeros_p['rwkv'], zeros_p['shift'], w, Bb=1, Tt=tp, C=cp, t_real=sp)
        ckv, kr, q, k, v = _mla_proj_call(xp.reshape(mp, D), w, cos_p, sin_p, tr=tp, prompt=True)
        oc = _flash_call(q.reshape(bp, sp, -1), k.reshape(bp, sp, -1), v.reshape(bp, sp, -1), tq=tq)
        for lst, arr in zip(new_p, (ckv.reshape(bp, sp, -1), kr.reshape(bp, sp, -1), s_g, cbuf, s_r,
                                    _rw_unpad(sh[:, 0]))):
            lst.append(arr)
        xp = _merge_ffn_call(xp.reshape(mp, D), oa.reshape(mp, -1), ob.reshape(mp, -1),
                             oc.reshape(mp, -1), w, fin_g, tm=tm_p, final=final).reshape(bp, sp, D)
        oa, s_g, cbuf = _gdn_call(xs, state_gdn[l], state_gdn_conv[l], w, Bb=bb_s, Tt=ts, C=ts, t_real=ss)
        ob, s_r, sh = _rwkv_call(xs, state_rwkv[l], _rw_pad(state_rwkv_shift[l])[:, None], w,
                                 Bb=bb_s, Tt=ts, C=ts, t_real=ss)
        ckv, kr, qn, qr = _mla_proj_call(xs.reshape(bs * ts, D), w, cos_s, sin_s, tr=bs * ts, prompt=False)
        ckv = ckv.reshape(bs, ts, -1)
        kr = kr.reshape(bs, ts, -1)
        oc = _decode_call(page_table, qn.reshape(bs, ts, -1), qr.reshape(bs, ts, -1), ckv, kr,
                          cache_ckv, cache_krope_t, l, w)
        for lst, arr in zip(new_s, (ckv[:, :ss], kr[:, :ss], s_g, cbuf, s_r, _rw_unpad(sh[:, 0]))):
            lst.append(arr)
        xs_real = _merge_ffn_call(xs[:, 0], oa[:, 0], ob[:, 0], oc[:, 0], w, fin_g, tm=bs, final=final)
        xs = jnp.pad(xs_real[:, None], ((0, 0), (0, ts - ss), (0, 0)))
    y_prompt = xp
    y_sample = xs[:, :ss]
    outs_p = [jnp.stack(a) for a in new_p]
    outs_s = [jnp.stack(a) for a in new_s]
    return (y_prompt, y_sample, *outs_p, *outs_s)
```

```python
import functools
import math

import jax
import jax.numpy as jnp
from jax import lax
from jax.experimental import pallas as pl
from jax.experimental.pallas import tpu as pltpu

F32 = jnp.float32
BF16 = jnp.bfloat16

D_MODEL = 1024
GDN_HEADS = 4
GDN_DK = 128
GDN_DV = 128
GDN_CONV = 4
GDN_KEY_W = GDN_HEADS * GDN_DK
GDN_VAL_W = GDN_HEADS * GDN_DV
GDN_CONV_CH = 2 * GDN_KEY_W + GDN_VAL_W
RWKV_HEADS = 8
RWKV_HEAD = 64
RWKV_W = RWKV_HEADS * RWKV_HEAD
RWKV_DECAY_LORA = 64
RWKV_AAA_LORA = 64
RWKV_GATE_LORA = 128
RWKV_COLS = 3 * RWKV_W + RWKV_DECAY_LORA + RWKV_AAA_LORA + RWKV_GATE_LORA
RWKV_LN_EPS = 64e-5
MLA_HEADS = 4
MLA_Q_RANK = 256
MLA_KV_RANK = 256
MLA_NOPE = 128
MLA_ROPE = 64
MLA_V = 128
MLA_SCALE = 1.0 / math.sqrt(MLA_NOPE + MLA_ROPE)
ROPE_THETA = 10000.0
D_FF = ((8 * D_MODEL // 3 + 255) // 256) * 256
NORM_EPS = 1e-6
L2_EPS = 1e-6

LANES = 128
SUBLANES = 8
VMEM_LIMIT = 56 * 1024 * 1024

_O_QKV = 0
_O_BETA = _O_QKV + GDN_CONV_CH
_O_ALPHA = _O_BETA + GDN_HEADS
_O_Z = _O_ALPHA + GDN_HEADS
_O_RW = _O_Z + GDN_VAL_W
_O_CQ = _O_RW + RWKV_COLS
_O_CKV = _O_CQ + MLA_Q_RANK
_O_KR = _O_CKV + MLA_KV_RANK
_O_GATE = _O_KR + MLA_ROPE

RW_PAD = 3 * RWKV_W + 3 * LANES
PRE_ROW0 = SUBLANES
QK_HEAD = MLA_NOPE + LANES


def _bdot(a, b):
    return jnp.dot(a.astype(BF16), b.astype(BF16), preferred_element_type=F32)


def _bdot_nt(a, b):
    return lax.dot_general(a.astype(BF16), b.astype(BF16), (((1,), (1,)), ((), ())),
                           preferred_element_type=F32)


def _hdot(a, b):
    return jnp.dot(a, b, precision=lax.Precision.HIGHEST, preferred_element_type=F32)


def _split_dot(a, b_exact, parts):
    acc = None
    rem = a
    for _ in range(parts):
        hi = rem.astype(BF16)
        t = jnp.dot(hi, b_exact, preferred_element_type=F32)
        acc = t if acc is None else acc + t
        rem = rem - hi.astype(F32)
    return acc


def _split_dot_rhs(a_exact, b, parts):
    acc = None
    rem = b
    for _ in range(parts):
        hi = rem.astype(BF16)
        t = jnp.dot(a_exact, hi, preferred_element_type=F32)
        acc = t if acc is None else acc + t
        rem = rem - hi.astype(F32)
    return acc


def _rms(x, g, eps=NORM_EPS):
    return x * lax.rsqrt(jnp.mean(x * x, axis=-1, keepdims=True) + eps) * g


def _sigmoid(x):
    return 1.0 / (1.0 + jnp.exp(-x))


def _softplus(x):
    return jnp.maximum(x, 0.0) + jnp.log(1.0 + jnp.exp(-jnp.abs(x)))


def _tri_masks(C):
    r = lax.broadcasted_iota(jnp.int32, (C, C), 0)
    c = lax.broadcasted_iota(jnp.int32, (C, C), 1)
    masks = []
    k = 0
    while (1 << k) < C:
        rr = r >> k
        cc = c >> k
        m = jnp.where((rr ^ cc) == 1, jnp.where((rr & 1) == 1, 1.0, 0.0), 0.0)
        masks.append(m.astype(F32))
        k += 1
    return r, c, masks


def _unit_lower_inverses(Ls, masks, eye):
    Ts = [eye - L * masks[0] for L in Ls]
    for m in masks[1:]:
        tmp = [_bdot(L * m, T) for L, T in zip(Ls, Ts)]
        Ts = [T - _bdot(T, t) for T, t in zip(Ts, tmp)]
    return Ts


def _chunk_tri(R, C):
    rt = lax.broadcasted_iota(jnp.int32, (R, R), 0)
    ct = lax.broadcasted_iota(jnp.int32, (R, R), 1)
    sh = C.bit_length() - 1
    same = (rt >> sh) == (ct >> sh)
    tri = jnp.where(same, jnp.where(rt >= ct, 1.0, 0.0), 0.0).astype(BF16)
    tri_t = jnp.where(same, jnp.where(ct >= rt, 1.0, 0.0), 0.0).astype(BF16)
    return tri, tri_t


def _gdn_kernel(x_ref, s0_ref, hist_ref, ng_ref, wg_ref, wbat_ref, cw_ref, prow_ref, pcol_ref, gng_ref, rep_ref,
                o_ref, s_ref, cb_ref, pre_scr, *, Bb, Tt, C, t_real, t_total):
    ti = pl.program_id(1)
    R = Bb * Tt

    @pl.when(ti == 0)
    def _():
        s_ref[...] = s0_ref[...]
        pre_scr[:, PRE_ROW0 - 3:PRE_ROW0, :] = hist_ref[...]

    h = _rms(x_ref[...].reshape(R, D_MODEL), ng_ref[...]).astype(BF16)
    proj = jnp.dot(h, wg_ref[...], preferred_element_type=F32)
    ba_row = lax.dot_general(wbat_ref[...], h, (((1,), (1,)), ((), ())),
                             preferred_element_type=F32)

    cw = cw_ref[...]
    pre_all = proj[:, :GDN_CONV_CH]
    real_rows = min(Tt, t_real)
    heads = []
    for bb in range(Bb):
        pre_scr[bb, PRE_ROW0:2 * PRE_ROW0, :] = pre_all[bb * Tt:bb * Tt + PRE_ROW0]
        yh = pre_scr[bb, PRE_ROW0 - 3:2 * PRE_ROW0 - 3, :] * cw[0:1]
        for i in range(1, GDN_CONV):
            yh = yh + pre_scr[bb, PRE_ROW0 - 3 + i:2 * PRE_ROW0 - 3 + i, :] * cw[i:i + 1]
        heads.append(yh)
        if real_rows >= PRE_ROW0:
            carry = pre_all[(bb + 1) * Tt - 3:(bb + 1) * Tt]
        else:
            carry = pre_scr[bb, PRE_ROW0 - 3 + real_rows:PRE_ROW0 + real_rows, :]
        pre_scr[bb, PRE_ROW0 - 3:PRE_ROW0, :] = carry
        cb_ref[bb] = carry
    if Tt == PRE_ROW0:
        y = heads[0] if Bb == 1 else jnp.concatenate(heads, axis=0)
    else:
        y = pre_all * cw[GDN_CONV - 1:GDN_CONV]
        for k in range(1, GDN_CONV):
            y = y + pltpu.roll(pre_all, k, axis=0) * cw[GDN_CONV - 1 - k:GDN_CONV - k]
        pieces = []
        for bb in range(Bb):
            pieces += [heads[bb], y[bb * Tt + PRE_ROW0:(bb + 1) * Tt]]
        y = jnp.concatenate(pieces, axis=0)
    qkv = y * _sigmoid(y)

    ba = proj[:, GDN_CONV_CH + GDN_VAL_W:]
    prow = prow_ref[...]
    beta_col = _sigmoid(ba)
    loga_col = -jnp.exp(prow[0:1]) * _softplus(ba + prow[1:2])
    pcol = pcol_ref[...]
    loga_row = -jnp.exp(pcol[:, 0:1]) * _softplus(ba_row + pcol[:, 1:2])
    if t_real < t_total:
        tcol = (lax.broadcasted_iota(jnp.int32, (R, 1), 0) & (Tt - 1)) + ti * Tt
        trow = (lax.broadcasted_iota(jnp.int32, (1, R), 1) & (Tt - 1)) + ti * Tt
        beta_col = jnp.where(tcol < t_real, beta_col, 0.0)
        loga_col = jnp.where(tcol < t_real, loga_col, 0.0)
        loga_row = jnp.where(trow < t_real, loga_row, 0.0)

    rep = rep_ref[...]
    beta_rep = _split_dot(beta_col, rep[:, :GDN_VAL_W], 2)
    loga_rep = _split_dot(loga_col, rep[:, GDN_VAL_W:], 3)
    tri, tri_t = _chunk_tri(R, C)
    gcol_rep = _split_dot_rhs(tri, loga_rep, 3)
    grow = _split_dot(loga_row, tri_t, 3)

    r, c, masks = _tri_masks(C)
    eye = jnp.where(r == c, 1.0, 0.0).astype(F32)
    incl = r >= c
    strict = r > c
    z = proj[:, GDN_CONV_CH:GDN_CONV_CH + GDN_VAL_W]
    gng = gng_ref[...]
    scale = GDN_DK ** -0.5

    qs, ks = [], []
    for hh in range(GDN_HEADS):
        qh = qkv[:, hh * GDN_DK:(hh + 1) * GDN_DK]
        kh = qkv[:, GDN_KEY_W + hh * GDN_DK:GDN_KEY_W + (hh + 1) * GDN_DK]
        qs.append(qh * lax.rsqrt(jnp.sum(qh * qh, axis=-1, keepdims=True) + L2_EPS) * scale)
        ks.append(kh * lax.rsqrt(jnp.sum(kh * kh, axis=-1, keepdims=True) + L2_EPS))

    nchunk = Tt // C
    bodies = [(bb, n, hh) for n in range(nchunk) for bb in range(Bb) for hh in range(GDN_HEADS)]
    Ls, pre = [], []
    for (bb, n, hh) in bodies:
        r0 = bb * Tt + n * C
        q = qs[hh][r0:r0 + C]
        k = ks[hh][r0:r0 + C]
        v = qkv[r0:r0 + C, 2 * GDN_KEY_W + hh * GDN_DV:2 * GDN_KEY_W + (hh + 1) * GDN_DV]
        bc = beta_rep[r0:r0 + C, hh * LANES:(hh + 1) * LANES]
        gc = gcol_rep[r0:r0 + C, hh * LANES:(hh + 1) * LANES]
        gr = grow[GDN_HEADS + hh:GDN_HEADS + hh + 1, r0:r0 + C]
        decay = jnp.where(incl, jnp.exp(jnp.where(incl, gc[:, :C] - gr, 0.0)), 0.0)
        eg = jnp.exp(gc)
        kb = k * bc
        kq = _bdot_nt(jnp.concatenate([kb, q], axis=0), k)
        Ls.append(jnp.where(strict, kq[:C] * decay, 0.0))
        qk = jnp.where(incl, kq[C:] * decay, 0.0)
        g_last = gc[C - 1:C, :]
        kd = k * jnp.exp(g_last - gc)
        pre.append((q * eg, qk, jnp.concatenate([v * bc, kb * eg], axis=1), kd, jnp.exp(g_last)))
    Ts = _unit_lower_inverses(Ls, masks, eye)
    uws = [_bdot(T, p[2]) for T, p in zip(Ts, pre)]
    kuws = [_bdot(p[3].T, uw) for p, uw in zip(pre, uws)]
    quws = [_bdot(p[1], uw) for p, uw in zip(pre, uws)]

    for i, (bb, n, hh) in enumerate(bodies):
        S = s_ref[bb, hh]
        qeg, _, _, _, egl = pre[i]
        o = _bdot(qeg - quws[i][:, GDN_DV:], S) + quws[i][:, :GDN_DV]
        s_ref[bb, hh] = S * egl - _bdot(kuws[i][:, GDN_DV:], S) + kuws[i][:, :GDN_DV]
        r0 = bb * Tt + n * C
        zh = z[r0:r0 + C, hh * GDN_DV:(hh + 1) * GDN_DV]
        o_ref[bb, n * C:(n + 1) * C, hh * GDN_DV:(hh + 1) * GDN_DV] = _rms(o, gng) * (zh * _sigmoid(zh))


def _wspec(a, l, **kw):
    if a.ndim == 2:
        return pl.BlockSpec(a.shape, lambda *_: (0, 0), **kw)
    return pl.BlockSpec((None,) + a.shape[1:], lambda *_: (l,) + (0,) * (a.ndim - 1), **kw)


def _gdn_call(x, s0, hist, ls, w, l, *, Bb, Tt, C, t_real):
    B, T, D = x.shape
    assert Tt & (Tt - 1) == 0 and C & (C - 1) == 0 and B % Bb == 0 and T % Tt == 0 and Tt % C == 0
    kern = functools.partial(_gdn_kernel, Bb=Bb, Tt=Tt, C=C, t_real=t_real, t_total=T)
    names = ['norm_mix_g', 'gdn_w', 'gdn_wba_t', 'gdn_conv_w', 'gdn_prow', 'gdn_pcol', 'gdn_norm_g', 'gdn_rep']
    return pl.pallas_call(
        kern,
        grid=(B // Bb, T // Tt),
        in_specs=[
            pl.BlockSpec((Bb, Tt, D), lambda b, t: (b, t, 0)),
            pl.BlockSpec((None, Bb, GDN_HEADS, GDN_DK, GDN_DV), lambda b, t: (ls, b, 0, 0, 0)),
            pl.BlockSpec((None, Bb, GDN_CONV - 1, GDN_CONV_CH), lambda b, t: (ls, b, 0, 0)),
        ] + [_wspec(w[n], l) for n in names],
        out_specs=[
            pl.BlockSpec((Bb, Tt, GDN_VAL_W), lambda b, t: (b, t, 0)),
            pl.BlockSpec((Bb, GDN_HEADS, GDN_DK, GDN_DV), lambda b, t: (b, 0, 0, 0)),
            pl.BlockSpec((Bb, GDN_CONV - 1, GDN_CONV_CH), lambda b, t: (b, 0, 0)),
        ],
        out_shape=[
            jax.ShapeDtypeStruct((B, T, GDN_VAL_W), F32),
            jax.ShapeDtypeStruct((B, GDN_HEADS, GDN_DK, GDN_DV), F32),
            jax.ShapeDtypeStruct((B, GDN_CONV - 1, GDN_CONV_CH), F32),
        ],
        scratch_shapes=[pltpu.VMEM((Bb, 2 * PRE_ROW0, GDN_CONV_CH), F32)],
        compiler_params=pltpu.CompilerParams(
            dimension_semantics=("arbitrary", "arbitrary"), vmem_limit_bytes=VMEM_LIMIT),
        name="gdn_mixer",
    )(x, s0, hist, *[w[n] for n in names])


def _rwkv_kernel(x_ref, s0_ref, sh_ref, ng_ref, wr_ref, mu_ref, w0_ref, w2_ref, a0_ref, a2_ref, g2_ref,
                 kk_ref, ka_ref, rk_ref, lng_ref, lnb_ref, e_ref,
                 o_ref, s_ref, sho_ref, pre_scr, y_scr, *, Bb, Tt, C, t_real, t_total):
    ti = pl.program_id(1)
    N = RWKV_HEAD
    R = Bb * Tt

    @pl.when(ti == 0)
    def _():
        s_ref[...] = s0_ref[...]
        pre_scr[:, 0:1, :] = sh_ref[...]

    h = _rms(x_ref[...].reshape(R, D_MODEL), ng_ref[...]).astype(BF16)
    pre = jnp.dot(h, wr_ref[...], preferred_element_type=F32)
    real_rows = min(Tt, t_real)
    rolled = pltpu.roll(pre, 1, axis=0)
    first = lax.broadcasted_iota(jnp.int32, (Tt, 1), 0) == 0
    prevs = []
    for bb in range(Bb):
        prevs.append(jnp.where(first, pre_scr[bb, 0:1, :], rolled[bb * Tt:(bb + 1) * Tt]))
        last = pre[bb * Tt + real_rows - 1:bb * Tt + real_rows]
        pre_scr[bb, 0:1, :] = last
        sho_ref[bb] = last
    prev = prevs[0] if Bb == 1 else jnp.concatenate(prevs, axis=0)

    xm = pre + (prev - pre) * mu_ref[...]
    W = RWKV_W
    rr = xm[:, 0:W]
    kx = xm[:, W:2 * W]
    vv = xm[:, 2 * W:3 * W]
    wlo = xm[:, 3 * W:3 * W + LANES]
    alo = xm[:, 3 * W + LANES:3 * W + 2 * LANES]
    glo = xm[:, 3 * W + 2 * LANES:3 * W + 3 * LANES]
    w_log = -_softplus(-(w0_ref[...] + _bdot(jnp.tanh(wlo), w2_ref[...]))) - 0.5
    logd = -jnp.exp(w_log)
    a = _sigmoid(a0_ref[...] + _bdot(alo, a2_ref[...]))
    gb = _bdot(_sigmoid(glo), g2_ref[...])
    E = e_ref[...]
    kkr = kx * kk_ref[...]
    kk = kkr * lax.rsqrt(_split_dot(kkr * kkr, E, 1) + L2_EPS)
    kb = kx * (1.0 + (a - 1.0) * ka_ref[...])
    a_eff = a
    v_eff = vv
    if t_real < t_total:
        tcol = (lax.broadcasted_iota(jnp.int32, (R, 1), 0) & (Tt - 1)) + ti * Tt
        valid = tcol < t_real
        logd = jnp.where(valid, logd, 0.0)
        a_eff = jnp.where(valid, a, 0.0)
        v_eff = jnp.where(valid, vv, 0.0)

    tri, _ = _chunk_tri(R, C)
    logG = _split_dot_rhs(tri, logd, 3)
    G = jnp.exp(logG)
    Ginv = jnp.exp(-logG)
    aq_all = kk * jnp.exp(logG - logd)
    bk_all = -(a_eff * kk) * Ginv
    kd_all = kb * Ginv
    rq_all = rr * G

    r, c, masks = _tri_masks(C)
    eye = jnp.where(r == c, 1.0, 0.0).astype(F32)
    incl = r >= c
    strict = r > c

    nchunk = Tt // C
    bodies = [(bb, n, hh) for n in range(nchunk) for bb in range(Bb) for hh in range(RWKV_HEADS)]
    nb = len(bodies)
    aqs, bks, kds, rqs, vhs, gls = [], [], [], [], [], []
    for (bb, n, hh) in bodies:
        r0 = bb * Tt + n * C
        ls = slice(hh * N, (hh + 1) * N)
        aqs.append(aq_all[r0:r0 + C, ls])
        bks.append(bk_all[r0:r0 + C, ls])
        kds.append(kd_all[r0:r0 + C, ls])
        rqs.append(rq_all[r0:r0 + C, ls])
        vhs.append(v_eff[r0:r0 + C, ls])
        gls.append(G[r0 + C - 1:r0 + C, ls])
    ars = [jnp.concatenate([aqs[i], rqs[i]], axis=0) for i in range(nb)]
    sbs = [_bdot_nt(ars[i], bks[i]) for i in range(nb)]
    sks = [_bdot_nt(ars[i], kds[i]) for i in range(nb)]
    Ls = [jnp.where(strict, -sbs[i][:C], 0.0) for i in range(nb)]
    ras = [jnp.where(incl, sbs[i][C:], 0.0) for i in range(nb)]
    rks = [jnp.where(incl, sks[i][C:], 0.0) for i in range(nb)]
    bmvs = [_bdot(jnp.where(strict, sks[i][:C], 0.0), vhs[i]) for i in range(nb)]
    Ts = _unit_lower_inverses(Ls, masks, eye)
    TAs = [_bdot(Ts[i], aqs[i]) for i in range(nb)]
    TBVs = [_bdot(Ts[i], bmvs[i]) for i in range(nb)]
    Xs = [bks[i] * gls[i] for i in range(nb)]
    Zs = [kds[i] * gls[i] for i in range(nb)]
    Q1s = [_bdot(TAs[i].T, Xs[i]) for i in range(nb)]
    M0s = [_bdot(jnp.concatenate([TBVs[i], vhs[i]], axis=0).T, jnp.concatenate([Xs[i], Zs[i]], axis=0))
           for i in range(nb)]
    rqps = [rqs[i] + _bdot(ras[i], TAs[i]) for i in range(nb)]
    y0s = [_bdot(jnp.concatenate([ras[i], rks[i]], axis=1), jnp.concatenate([TBVs[i], vhs[i]], axis=0))
           for i in range(nb)]

    per_chunk = Bb * RWKV_HEADS
    for n in range(nchunk):
        idx = range(n * per_chunk, (n + 1) * per_chunk)
        Ss = {i: s_ref[bodies[i][0], bodies[i][2]] for i in idx}
        for i in idx:
            bb, _, hh = bodies[i]
            r0 = bb * Tt + n * C
            y_scr[r0:r0 + C, hh * N:(hh + 1) * N] = _bdot_nt(rqps[i], Ss[i]) + y0s[i]
        for i in idx:
            bb, _, hh = bodies[i]
            s_ref[bb, hh] = Ss[i] * gls[i] + _bdot(Ss[i], Q1s[i]) + M0s[i]

    y = y_scr[...]
    inv_n = 1.0 / N
    mu = _split_dot(y, E, 2) * inv_n
    yc = y - mu
    var = _split_dot(yc * yc, E, 1) * inv_n
    yn = yc * lax.rsqrt(var + RWKV_LN_EPS) * lng_ref[...] + lnb_ref[...]
    bonus = _split_dot(rr * kb * rk_ref[...], E, 1) * vv
    o_ref[...] = ((yn + bonus) * gb).reshape(Bb, Tt, RWKV_W)


def _rwkv_call(x, s0, shift, ls, w, l, *, Bb, Tt, C, t_real):
    B, T, D = x.shape
    assert Tt & (Tt - 1) == 0 and C & (C - 1) == 0 and B % Bb == 0 and T % Tt == 0 and Tt % C == 0
    kern = functools.partial(_rwkv_kernel, Bb=Bb, Tt=Tt, C=C, t_real=t_real, t_total=T)
    names = ['norm_mix_g', 'rwkv_w', 'rwkv_mu', 'rwkv_w0', 'rwkv_w2', 'rwkv_a0', 'rwkv_a2', 'rwkv_g2',
             'rwkv_k_k', 'rwkv_k_a', 'rwkv_r_k', 'rwkv_ln_g', 'rwkv_ln_b', 'rwkv_e']
    return pl.pallas_call(
        kern,
        grid=(B // Bb, T // Tt),
        in_specs=[
            pl.BlockSpec((Bb, Tt, D), lambda b, t: (b, t, 0)),
            pl.BlockSpec((None, Bb, RWKV_HEADS, RWKV_HEAD, RWKV_HEAD), lambda b, t: (ls, b, 0, 0, 0)),
            pl.BlockSpec((None, Bb, 1, RW_PAD), lambda b, t: (ls, b, 0, 0)),
        ] + [_wspec(w[n], l) for n in names],
        out_specs=[
            pl.BlockSpec((Bb, Tt, RWKV_W), lambda b, t: (b, t, 0)),
            pl.BlockSpec((Bb, RWKV_HEADS, RWKV_HEAD, RWKV_HEAD), lambda b, t: (b, 0, 0, 0)),
            pl.BlockSpec((Bb, 1, RW_PAD), lambda b, t: (b, 0, 0)),
        ],
        out_shape=[
            jax.ShapeDtypeStruct((B, T, RWKV_W), F32),
            jax.ShapeDtypeStruct((B, RWKV_HEADS, RWKV_HEAD, RWKV_HEAD), F32),
            jax.ShapeDtypeStruct((B, 1, RW_PAD), F32),
        ],
        scratch_shapes=[pltpu.VMEM((Bb, SUBLANES, RW_PAD), F32), pltpu.VMEM((Bb * Tt, RWKV_W), F32)],
        compiler_params=pltpu.CompilerParams(
            dimension_semantics=("arbitrary", "arbitrary"), vmem_limit_bytes=VMEM_LIMIT),
        name="rwkv_mixer",
    )(x, s0, shift, *[w[n] for n in names])


def _mla_proj_kernel(x_ref, ng_ref, wm_ref, qg_ref, kvg_ref, wq_ref, cos_ref, sin_ref, *rest, prompt):
    if prompt:
        wuk_ref, wuv_ref, ckv_ref, kr_ref, q_ref, k_ref, v_ref = rest
    else:
        ckv_ref, kr_ref, qn_ref, qr_ref = rest
    h = _rms(x_ref[...], ng_ref[...]).astype(BF16)
    p = jnp.dot(h, wm_ref[...], preferred_element_type=F32)
    cqn = _rms(p[:, :MLA_Q_RANK], qg_ref[...]).astype(BF16)
    ckv = _rms(p[:, MLA_Q_RANK:MLA_Q_RANK + MLA_KV_RANK], kvg_ref[...])
    o = MLA_Q_RANK + MLA_KV_RANK
    cos = cos_ref[...]
    sin = sin_ref[...]
    krp = p[:, o:o + LANES] * cos + p[:, o + LANES:o + 2 * LANES] * sin
    ckv_ref[...] = ckv
    kr_ref[...] = krp[:, :MLA_ROPE]
    q = jnp.dot(cqn, wq_ref[...], preferred_element_type=F32)
    wn = MLA_HEADS * MLA_NOPE
    wr = MLA_HEADS * LANES
    cos4 = jnp.concatenate([cos] * MLA_HEADS, axis=1)
    sin4 = jnp.concatenate([sin] * MLA_HEADS, axis=1)
    qn = q[:, :wn] * MLA_SCALE
    qr = (q[:, wn:wn + wr] * cos4 + q[:, wn + wr:wn + 2 * wr] * sin4) * MLA_SCALE
    if prompt:
        cb = ckv.astype(BF16)
        kn = jnp.dot(cb, wuk_ref[...], preferred_element_type=F32)
        qparts, kparts = [], []
        for hh in range(MLA_HEADS):
            qparts += [qn[:, hh * MLA_NOPE:(hh + 1) * MLA_NOPE], qr[:, hh * LANES:(hh + 1) * LANES]]
            kparts += [kn[:, hh * MLA_NOPE:(hh + 1) * MLA_NOPE], krp]
        q_ref[...] = jnp.concatenate(qparts, axis=1).astype(BF16)
        k_ref[...] = jnp.concatenate(kparts, axis=1).astype(BF16)
        v_ref[...] = lax.dot_general(wuv_ref[...], cb, (((1,), (1,)), ((), ())),
                                     preferred_element_type=F32).astype(BF16)
    else:
        qn_ref[...] = qn
        qr_ref[...] = qr


def _mla_proj_call(x2, w, l, cos, sin, *, tr, prompt):
    M, D = x2.shape
    nt = cos.shape[0] // tr
    full = lambda a: _wspec(a, l)
    tab = pl.BlockSpec((tr, LANES), lambda i: (i % nt, 0))
    tok = lambda n: pl.BlockSpec((tr, n), lambda i: (i, 0))
    ins = [x2, w['norm_mix_g'], w['mla_w'], w['mla_q_norm_g'], w['mla_kv_norm_g'], w['mla_wq'], cos, sin]
    in_specs = [tok(D)] + [full(a) for a in ins[1:6]] + [tab, tab]
    out_specs = [tok(MLA_KV_RANK), tok(MLA_ROPE)]
    out_shape = [jax.ShapeDtypeStruct((M, MLA_KV_RANK), F32), jax.ShapeDtypeStruct((M, MLA_ROPE), F32)]
    if prompt:
        T = cos.shape[0]
        ins += [w['mla_wuk'], w['mla_wuv_t']]
        in_specs += [full(w['mla_wuk']), full(w['mla_wuv_t'])]
        out_specs += [tok(MLA_HEADS * QK_HEAD), tok(MLA_HEADS * QK_HEAD),
                      pl.BlockSpec((None, MLA_HEADS * MLA_V, tr), lambda i: (i // nt, 0, i % nt))]
        out_shape += [jax.ShapeDtypeStruct((M, MLA_HEADS * QK_HEAD), BF16),
                      jax.ShapeDtypeStruct((M, MLA_HEADS * QK_HEAD), BF16),
                      jax.ShapeDtypeStruct((M // T, MLA_HEADS * MLA_V, T), BF16)]
    else:
        out_specs += [tok(MLA_HEADS * MLA_NOPE), tok(MLA_HEADS * LANES)]
        out_shape += [jax.ShapeDtypeStruct((M, MLA_HEADS * MLA_NOPE), F32),
                      jax.ShapeDtypeStruct((M, MLA_HEADS * LANES), F32)]
    return pl.pallas_call(
        functools.partial(_mla_proj_kernel, prompt=prompt),
        grid=(M // tr,),
        in_specs=in_specs, out_specs=out_specs, out_shape=out_shape,
        compiler_params=pltpu.CompilerParams(dimension_semantics=("arbitrary",), vmem_limit_bytes=VMEM_LIMIT),
        name="mla_proj_prompt" if prompt else "mla_proj_sample",
    )(*ins)


def _flash_kernel(q_ref, k_ref, vt_ref, o_ref, m_scr, l_scr, acc_scr, *, tq):
    qi = pl.program_id(1)
    ki = pl.program_id(2)

    @pl.when(ki == 0)
    def _():
        m_scr[...] = jnp.full(m_scr.shape, -jnp.inf, F32)
        l_scr[...] = jnp.zeros(l_scr.shape, F32)
        acc_scr[...] = jnp.zeros(acc_scr.shape, F32)

    def step(diagonal):
        if diagonal:
            kpos = lax.broadcasted_iota(jnp.int32, (tq, tq), 0)
            qpos = lax.broadcasted_iota(jnp.int32, (tq, tq), 1)
            keep = kpos <= qpos
        for hh in range(MLA_HEADS):
            lq = slice(hh * QK_HEAD, (hh + 1) * QK_HEAD)
            rv = slice(hh * MLA_V, (hh + 1) * MLA_V)
            st = lax.dot_general(k_ref[0, :, lq], q_ref[0, :, lq], (((1,), (1,)), ((), ())),
                                 preferred_element_type=F32)
            if diagonal:
                st = jnp.where(keep, st, -jnp.inf)
            m_old = m_scr[hh:hh + 1, :]
            m_new = jnp.maximum(m_old, jnp.max(st, axis=0, keepdims=True))
            alpha = jnp.exp(m_old - m_new)
            p = jnp.exp(st - m_new)
            l_scr[hh:hh + 1, :] = alpha * l_scr[hh:hh + 1, :] + jnp.sum(p, axis=0, keepdims=True)
            acc_scr[rv, :] = alpha * acc_scr[rv, :] + jnp.dot(vt_ref[0, rv, :], p.astype(BF16),
                                                              preferred_element_type=F32)
            m_scr[hh:hh + 1, :] = m_new

    @pl.when(ki < qi)
    def _():
        step(False)

    @pl.when(ki == qi)
    def _():
        step(True)
        for hh in range(MLA_HEADS):
            rv = slice(hh * MLA_V, (hh + 1) * MLA_V)
            o_ref[0, rv, :] = acc_scr[rv, :] / l_scr[hh:hh + 1, :]


def _flash_call(q, k, vt, *, tq):
    B, T, _ = q.shape
    nq = T // tq
    hv = vt.shape[1]
    return pl.pallas_call(
        functools.partial(_flash_kernel, tq=tq),
        grid=(B, nq, nq),
        in_specs=[pl.BlockSpec((1, tq, q.shape[2]), lambda b, i, j: (b, i, 0)),
                  pl.BlockSpec((1, tq, k.shape[2]), lambda b, i, j: (b, jnp.minimum(i, j), 0)),
                  pl.BlockSpec((1, hv, tq), lambda b, i, j: (b, 0, jnp.minimum(i, j)))],
        out_specs=pl.BlockSpec((1, hv, tq), lambda b, i, j: (b, 0, i)),
        out_shape=jax.ShapeDtypeStruct((B, hv, T), F32),
        scratch_shapes=[pltpu.VMEM((SUBLANES, tq), F32), pltpu.VMEM((SUBLANES, tq), F32),
                        pltpu.VMEM((hv, tq), F32)],
        compiler_params=pltpu.CompilerParams(
            dimension_semantics=("arbitrary", "arbitrary", "arbitrary"), vmem_limit_bytes=VMEM_LIMIT),
        name="mla_flash",
    )(q, k, vt)


def _decode_kernel(pt_ref, qn_ref, qr_ref, ckvn_ref, krn_ref, wuk_ref, wuv_ref, *rest, G):
    ckv_refs = rest[:G]
    krt_refs = rest[G:2 * G]
    o_ref = rest[2 * G]
    qlat_scr, qrm_scr, m_scr, l_scr, acc_scr = rest[2 * G + 1:]
    j = pl.program_id(1)
    rid = lax.broadcasted_iota(jnp.int32, (SUBLANES, 1), 0)

    @pl.when(j == 0)
    def _():
        qn = qn_ref[0]
        qr = qr_ref[0]
        qlat = jnp.zeros((SUBLANES, MLA_KV_RANK), F32)
        qrm = jnp.zeros((SUBLANES, MLA_ROPE), F32)
        for hh in range(MLA_HEADS):
            ql = _bdot_nt(qn[:, hh * MLA_NOPE:(hh + 1) * MLA_NOPE], wuk_ref[hh])
            qlat = jnp.where(rid == hh, ql[0:1, :], qlat)
            qrm = jnp.where(rid == hh, qr[0:1, hh * LANES:hh * LANES + MLA_ROPE], qrm)
        qlat_scr[...] = qlat
        qrm_scr[...] = qrm
        m_scr[...] = jnp.full(m_scr.shape, -jnp.inf, F32)
        l_scr[...] = jnp.zeros(l_scr.shape, F32)
        acc_scr[...] = jnp.zeros(acc_scr.shape, F32)

    qlat = qlat_scr[...].astype(BF16)
    qrm = qrm_scr[...].astype(BF16)
    pages = [ckv_refs[i][...].astype(BF16) for i in range(G)]
    s = jnp.concatenate(
        [_bdot_nt(qlat, pages[i]) + _bdot(qrm, krt_refs[i][...]) for i in range(G)], axis=1)
    m_old = m_scr[...]
    m_new = jnp.maximum(m_old, jnp.max(s, axis=-1, keepdims=True))
    alpha = jnp.exp(m_old - m_new)
    p = jnp.exp(s - m_new)
    l_new = alpha * l_scr[...] + jnp.sum(p, axis=-1, keepdims=True)
    acc = alpha * acc_scr[...]
    page = pages[0].shape[0]
    for i in range(G):
        acc = acc + jnp.dot(p[:, i * page:(i + 1) * page].astype(BF16), pages[i], preferred_element_type=F32)
    m_scr[...] = m_new
    l_scr[...] = l_new
    acc_scr[...] = acc

    @pl.when(j == pl.num_programs(1) - 1)
    def _():
        ckvn = ckvn_ref[0][0:1, :]
        krn = krn_ref[0][0:1, :]
        ql = qlat_scr[...]
        qr_ = qrm_scr[...]
        s_new = (jnp.sum(ql * ckvn, axis=-1, keepdims=True) + jnp.sum(qr_ * krn, axis=-1, keepdims=True))
        m_fin = jnp.maximum(m_new, s_new)
        a2 = jnp.exp(m_new - m_fin)
        p_new = jnp.exp(s_new - m_fin)
        l_fin = a2 * l_new + p_new
        o_lat = (a2 * acc + p_new * ckvn) / l_fin
        outs = []
        for hh in range(MLA_HEADS):
            oh = _bdot(o_lat, wuv_ref[hh])
            outs.append(oh[hh:hh + 1, :])
        o_ref[0] = jnp.broadcast_to(jnp.concatenate(outs, axis=1), (SUBLANES, MLA_HEADS * MLA_V))


def _decode_call(page_table, qn, qr, ckvn, krn, cache_ckv, cache_krope_t, layer, w):
    B = qn.shape[0]
    n_pages = page_table.shape[1]
    page = cache_ckv.shape[2]
    G = 16
    while n_pages % G:
        G //= 2
    tok = lambda a: pl.BlockSpec((1,) + a.shape[1:], lambda b, j, pt: (b, 0, 0))
    full = lambda a: _wspec(a, layer)

    def cache_spec(shape, i):
        return pl.BlockSpec((None, None) + shape, lambda b, j, pt: (layer, pt[b, j * G + i], 0, 0))

    in_specs = ([tok(qn), tok(qr), tok(ckvn), tok(krn), full(w['mla_wuk_h']), full(w['mla_wuv_h'])]
                + [cache_spec((page, MLA_KV_RANK), i) for i in range(G)]
                + [cache_spec((MLA_ROPE, page), i) for i in range(G)])
    grid_spec = pltpu.PrefetchScalarGridSpec(
        num_scalar_prefetch=1,
        grid=(B, n_pages // G),
        in_specs=in_specs,
        out_specs=pl.BlockSpec((1, SUBLANES, MLA_HEADS * MLA_V), lambda b, j, pt: (b, 0, 0)),
        scratch_shapes=[pltpu.VMEM((SUBLANES, MLA_KV_RANK), F32), pltpu.VMEM((SUBLANES, MLA_ROPE), F32),
                        pltpu.VMEM((SUBLANES, 1), F32), pltpu.VMEM((SUBLANES, 1), F32),
                        pltpu.VMEM((SUBLANES, MLA_KV_RANK), F32)],
    )
    return pl.pallas_call(
        functools.partial(_decode_kernel, G=G),
        grid_spec=grid_spec,
        out_shape=jax.ShapeDtypeStruct((B, SUBLANES, MLA_HEADS * MLA_V), F32),
        compiler_params=pltpu.CompilerParams(
            dimension_semantics=("arbitrary", "arbitrary"), vmem_limit_bytes=VMEM_LIMIT),
        name="mla_decode",
    )(page_table, qn, qr, ckvn, krn, w['mla_wuk_h'], w['mla_wuv_h'],
      *([cache_ckv] * G), *([cache_krope_t] * G))


def _merge_ffn_kernel(x_ref, oa_ref, ob_ref, oc_ref, ng_ref, wgt_ref, bg_ref, wa_ref, wb_ref, wc_ref, wo_ref,
                      nf_ref, wup_ref, wdn_ref, fin_ref, y_ref, *, final, oc_transposed):
    x = x_ref[...]
    h = _rms(x, ng_ref[...]).astype(BF16)
    gates = _sigmoid(jnp.dot(h, wgt_ref[...], preferred_element_type=F32) + bg_ref[...])
    D = D_MODEL
    oc = oc_ref[...].T if oc_transposed else oc_ref[...]
    merged = (gates[:, 0:D] * _bdot(oa_ref[...], wa_ref[...])
              + gates[:, D:2 * D] * _bdot(ob_ref[...], wb_ref[...])
              + gates[:, 2 * D:3 * D] * _bdot(oc, wc_ref[...]))
    x1 = x + _bdot(merged, wo_ref[...])
    h2 = _rms(x1, nf_ref[...]).astype(BF16)
    up = jnp.dot(h2, wup_ref[...], preferred_element_type=F32)
    g = up[:, :D_FF]
    x2 = x1 + _bdot(g * _sigmoid(g) * up[:, D_FF:], wdn_ref[...])
    y_ref[...] = _rms(x2, fin_ref[...]) if final else x2


def _merge_ffn_call(x, oa, ob, oc, w, l, fin_g, *, tm, final):
    M, D = x.shape
    const = lambda a: _wspec(a, l, pipeline_mode=pl.Buffered(1))
    row = lambda n: pl.BlockSpec((tm, n), lambda i: (i, 0))
    names = ['norm_mix_g', 'w_gate', 'b_gate', 'w_br_a', 'w_br_b', 'w_br_c', 'w_out', 'norm_ffn_g',
             'w_ffn_up', 'w_ffn_down']
    oc_transposed = oc.ndim == 3
    if oc_transposed:
        nt = oc.shape[2] // tm
        oc_spec = pl.BlockSpec((None, oc.shape[1], tm), lambda i: (i // nt, 0, i % nt))
    else:
        oc_spec = row(oc.shape[1])
    return pl.pallas_call(
        functools.partial(_merge_ffn_kernel, final=final, oc_transposed=oc_transposed),
        grid=(M // tm,),
        in_specs=[row(D), row(oa.shape[1]), row(ob.shape[1]), oc_spec]
                 + [const(w[n]) for n in names] + [const(fin_g)],
        out_specs=row(D),
        out_shape=jax.ShapeDtypeStruct((M, D), F32),
        compiler_params=pltpu.CompilerParams(
            dimension_semantics=("arbitrary",), vmem_limit_bytes=VMEM_LIMIT),
        name="merge_ffn",
    )(x, oa, ob, oc, *[w[n] for n in names], fin_g)


def _pad_cols(a, n):
    return jnp.pad(a, [(0, 0)] * (a.ndim - 1) + [(0, n - a.shape[-1])])


def _swap_halves(a, width):
    shp = a.shape
    a = a.reshape(shp[:-1] + (shp[-1] // width, 2, width // 2))
    return a[..., ::-1, :].reshape(shp)


def _pad_groups(a, width, to):
    shp = a.shape
    a = a.reshape(shp[:-1] + (shp[-1] // width, width))
    a = jnp.pad(a, [(0, 0)] * (a.ndim - 1) + [(0, to - width)])
    return a.reshape(shp[:-1] + (-1,))


def _rw_pad(a):
    W = RWKV_W
    z = jnp.zeros(a.shape[:-1] + (LANES - RWKV_DECAY_LORA,), a.dtype)
    return jnp.concatenate([a[..., :3 * W], a[..., 3 * W:3 * W + RWKV_DECAY_LORA], z,
                            a[..., 3 * W + RWKV_DECAY_LORA:3 * W + RWKV_DECAY_LORA + RWKV_AAA_LORA], z,
                            a[..., 3 * W + RWKV_DECAY_LORA + RWKV_AAA_LORA:]], axis=-1)


def _rw_unpad(a):
    W = RWKV_W
    return jnp.concatenate([a[..., :3 * W], a[..., 3 * W:3 * W + RWKV_DECAY_LORA],
                            a[..., 3 * W + LANES:3 * W + LANES + RWKV_AAA_LORA],
                            a[..., 3 * W + 2 * LANES:]], axis=-1)


def _prep_weights(p):
    w_in = p['w_in']
    depth = w_in.shape[0]
    row = lambda a: a.reshape(depth, 1, -1).astype(F32)
    w = {}
    w['norm_mix_g'] = row(p['norm_mix_g'])
    w['norm_ffn_g'] = row(p['norm_ffn_g'])
    ba = w_in[..., _O_BETA:_O_Z]
    w['gdn_w'] = jnp.concatenate([w_in[..., _O_QKV:_O_BETA], w_in[..., _O_Z:_O_RW], _pad_cols(ba, LANES)],
                                 axis=-1).astype(BF16)
    w['gdn_wba_t'] = jnp.swapaxes(ba, 1, 2).astype(BF16)
    w['gdn_conv_w'] = p['gdn_conv_w']
    zero4 = jnp.zeros((depth, GDN_HEADS), F32)
    a_log = jnp.concatenate([zero4, p['gdn_a_log']], axis=1)
    dt_b = jnp.concatenate([zero4, p['gdn_dt_bias']], axis=1)
    w['gdn_prow'] = _pad_cols(jnp.stack([a_log, dt_b], axis=1), LANES)
    w['gdn_pcol'] = jnp.stack([a_log, dt_b], axis=2)
    w['gdn_norm_g'] = row(p['gdn_norm_g'])
    lane = jnp.arange(LANES)[:, None]
    head_of_col = jnp.arange(GDN_VAL_W)[None, :] // GDN_DV
    w['gdn_rep'] = jnp.concatenate([lane == head_of_col, lane == head_of_col + GDN_HEADS], axis=1).astype(BF16)
    w['rwkv_w'] = _rw_pad(w_in[..., _O_RW:_O_CQ]).astype(BF16)
    w['rwkv_mu'] = row(_rw_pad(p['rwkv_mu']))
    w['rwkv_w0'] = row(p['rwkv_w0'])
    w['rwkv_w2'] = jnp.pad(p['rwkv_w2'], ((0, 0), (0, LANES - RWKV_DECAY_LORA), (0, 0))).astype(BF16)
    w['rwkv_a0'] = row(p['rwkv_a0'])
    w['rwkv_a2'] = jnp.pad(p['rwkv_a2'], ((0, 0), (0, LANES - RWKV_AAA_LORA), (0, 0))).astype(BF16)
    w['rwkv_g2'] = p['rwkv_g2'].astype(BF16)
    w['rwkv_k_k'] = row(p['rwkv_k_k'])
    w['rwkv_k_a'] = row(p['rwkv_k_a'])
    w['rwkv_r_k'] = row(p['rwkv_r_k'])
    w['rwkv_ln_g'] = row(p['rwkv_ln_g'])
    w['rwkv_ln_b'] = row(p['rwkv_ln_b'])
    hid = jnp.arange(RWKV_W) // RWKV_HEAD
    w['rwkv_e'] = (hid[:, None] == hid[None, :]).astype(BF16)
    w_kr = w_in[..., _O_KR:_O_GATE]
    w['mla_w'] = jnp.concatenate([w_in[..., _O_CQ:_O_KR], _pad_cols(w_kr, LANES),
                                  _pad_cols(_swap_halves(w_kr, MLA_ROPE), LANES)], axis=-1).astype(BF16)
    w['mla_q_norm_g'] = row(p['mla_q_norm_g'])
    w['mla_kv_norm_g'] = row(p['mla_kv_norm_g'])
    w_uq = p['mla_w_uq']
    wq_n = w_uq[..., :MLA_NOPE].reshape(depth, MLA_Q_RANK, -1)
    wq_r = w_uq[..., MLA_NOPE:].reshape(depth, MLA_Q_RANK, -1)
    w['mla_wq'] = jnp.concatenate([wq_n, _pad_groups(wq_r, MLA_ROPE, LANES),
                                   _pad_groups(_swap_halves(wq_r, MLA_ROPE), MLA_ROPE, LANES)],
                                  axis=-1).astype(BF16)
    w['mla_wuk'] = p['mla_w_uk'].reshape(depth, MLA_KV_RANK, -1).astype(BF16)
    w['mla_wuv_t'] = jnp.swapaxes(p['mla_w_uv'].reshape(depth, MLA_KV_RANK, -1), 1, 2).astype(BF16)
    w['mla_wuk_h'] = jnp.transpose(p['mla_w_uk'], (0, 2, 1, 3)).astype(BF16)
    w['mla_wuv_h'] = jnp.transpose(p['mla_w_uv'], (0, 2, 1, 3)).astype(BF16)
    w['w_gate'] = w_in[..., _O_GATE:].astype(BF16)
    w['b_gate'] = row(p['b_gate'])
    for n in ('w_br_a', 'w_br_b', 'w_br_c', 'w_out', 'w_ffn_up', 'w_ffn_down'):
        w[n] = p[n].astype(BF16)
    return w


def _rope_tables(pos):
    half = MLA_ROPE // 2
    freq = ROPE_THETA ** (-jnp.arange(half, dtype=F32) / half)
    ang = pos.astype(F32)[:, None] * freq
    cos = jnp.cos(ang)
    sin = jnp.sin(ang)
    return (_pad_cols(jnp.concatenate([cos, cos], axis=1), LANES),
            _pad_cols(jnp.concatenate([-sin, sin], axis=1), LANES))


def _pick_tile(T, pref):
    t = min(T, pref)
    while T % t:
        t //= 2
    return t


def kernel(x_prompt, x_sample, cache_ckv, cache_krope, page_table, state_gdn, state_gdn_conv, state_rwkv, state_rwkv_shift, norm_mix_g, norm_ffn_g, norm_final_g, w_in, b_gate, gdn_conv_w, gdn_a_log, gdn_dt_bias, gdn_norm_g, rwkv_mu, rwkv_w0, rwkv_w2, rwkv_a0, rwkv_a2, rwkv_g2, rwkv_k_k, rwkv_k_a, rwkv_r_k, rwkv_ln_g, rwkv_ln_b, mla_q_norm_g, mla_kv_norm_g, mla_w_uq, mla_w_uk, mla_w_uv, w_br_a, w_br_b, w_br_c, w_out, w_ffn_up, w_ffn_down):
    p = dict(norm_mix_g=norm_mix_g, norm_ffn_g=norm_ffn_g, w_in=w_in, b_gate=b_gate, gdn_conv_w=gdn_conv_w,
             gdn_a_log=gdn_a_log, gdn_dt_bias=gdn_dt_bias, gdn_norm_g=gdn_norm_g, rwkv_mu=rwkv_mu,
             rwkv_w0=rwkv_w0, rwkv_w2=rwkv_w2, rwkv_a0=rwkv_a0, rwkv_a2=rwkv_a2, rwkv_g2=rwkv_g2,
             rwkv_k_k=rwkv_k_k, rwkv_k_a=rwkv_k_a, rwkv_r_k=rwkv_r_k, rwkv_ln_g=rwkv_ln_g,
             rwkv_ln_b=rwkv_ln_b, mla_q_norm_g=mla_q_norm_g, mla_kv_norm_g=mla_kv_norm_g,
             mla_w_uq=mla_w_uq, mla_w_uk=mla_w_uk, mla_w_uv=mla_w_uv, w_br_a=w_br_a, w_br_b=w_br_b,
             w_br_c=w_br_c, w_out=w_out, w_ffn_up=w_ffn_up, w_ffn_down=w_ffn_down)
    depth = w_in.shape[0]
    bp, sp, D = x_prompt.shape
    bs, ss, _ = x_sample.shape
    assert ss == 1, "the sample group decodes one new token per sequence"
    past_len = page_table.shape[1] * cache_ckv.shape[2]
    fin_g = norm_final_g.reshape(1, -1).astype(F32)
    cache_krope_t = jnp.swapaxes(cache_krope, 2, 3)

    tp = _pick_tile(sp, 256)
    cp = _pick_tile(tp, 64)
    tq = _pick_tile(sp, 512)
    ts = SUBLANES
    bb_s = _pick_tile(bs, 8)
    cos_p, sin_p = _rope_tables(jnp.arange(sp, dtype=jnp.int32))
    cos_s, sin_s = _rope_tables(jnp.full((bs * ts,), past_len, dtype=jnp.int32))
    zeros_p = dict(
        gdn=jnp.zeros((1, bp, GDN_HEADS, GDN_DK, GDN_DV), F32),
        conv=jnp.zeros((1, bp, GDN_CONV - 1, GDN_CONV_CH), F32),
        rwkv=jnp.zeros((1, bp, RWKV_HEADS, RWKV_HEAD, RWKV_HEAD), F32),
        shift=jnp.zeros((1, bp, 1, RW_PAD), F32))
    shift_s = _rw_pad(state_rwkv_shift)[:, :, None]

    w = _prep_weights(p)
    xp = x_prompt
    xs = jnp.pad(x_sample, ((0, 0), (0, ts - ss), (0, 0)))
    new_p = [[] for _ in range(6)]
    new_s = [[] for _ in range(6)]
    mp = bp * sp
    tm_p = _pick_tile(sp, 256)
    for l in range(depth):
        final = l == depth - 1
        oa, s_g, cbuf = _gdn_call(xp, zeros_p['gdn'], zeros_p['conv'], 0, w, l, Bb=1, Tt=tp, C=cp, t_real=sp)
        ob, s_r, sh = _rwkv_call(xp, zeros_p['rwkv'], zeros_p['shift'], 0, w, l, Bb=1, Tt=tp, C=cp, t_real=sp)
        ckv, kr, q, k, vt = _mla_proj_call(xp.reshape(mp, D), w, l, cos_p, sin_p, tr=tp, prompt=True)
        oc_t = _flash_call(q.reshape(bp, sp, -1), k.reshape(bp, sp, -1), vt, tq=tq)
        for lst, arr in zip(new_p, (ckv.reshape(bp, sp, -1), kr.reshape(bp, sp, -1), s_g, cbuf, s_r, sh[:, 0])):
            lst.append(arr)
        xp = _merge_ffn_call(xp.reshape(mp, D), oa.reshape(mp, -1), ob.reshape(mp, -1),
                             oc_t, w, l, fin_g, tm=tm_p, final=final).reshape(bp, sp, D)
        oa, s_g, cbuf = _gdn_call(xs, state_gdn, state_gdn_conv, l, w, l, Bb=bb_s, Tt=ts, C=ts, t_real=ss)
        ob, s_r, sh = _rwkv_call(xs, state_rwkv, shift_s, l, w, l, Bb=bb_s, Tt=ts, C=ts, t_real=ss)
        ckv, kr, qn, qr = _mla_proj_call(xs.reshape(bs * ts, D), w, l, cos_s, sin_s, tr=bs * ts, prompt=False)
        ckv = ckv.reshape(bs, ts, -1)
        kr = kr.reshape(bs, ts, -1)
        oc = _decode_call(page_table, qn.reshape(bs, ts, -1), qr.reshape(bs, ts, -1), ckv, kr,
                          cache_ckv, cache_krope_t, l, w)
        for lst, arr in zip(new_s, (ckv[:, :ss], kr[:, :ss], s_g, cbuf, s_r, sh[:, 0])):
            lst.append(arr)
        xs_real = _merge_ffn_call(xs[:, 0], oa[:, 0], ob[:, 0], oc[:, 0], w, l, fin_g, tm=bs, final=final)
        xs = jnp.pad(xs_real[:, None], ((0, 0), (0, ts - ss), (0, 0)))
    y_prompt = xp
    y_sample = xs[:, :ss]
    outs_p = [jnp.stack(a) for a in new_p]
    outs_s = [jnp.stack(a) for a in new_s]
    outs_p[5] = _rw_unpad(outs_p[5])
    outs_s[5] = _rw_unpad(outs_s[5])
    return (y_prompt, y_sample, *outs_p, *outs_s)
```

```python
import functools
import math

import jax
import jax.numpy as jnp
from jax import lax
from jax.experimental import pallas as pl
from jax.experimental.pallas import tpu as pltpu

F32 = jnp.float32
BF16 = jnp.bfloat16

D_MODEL = 1024
GDN_HEADS = 4
GDN_DK = 128
GDN_DV = 128
GDN_CONV = 4
GDN_KEY_W = GDN_HEADS * GDN_DK
GDN_VAL_W = GDN_HEADS * GDN_DV
GDN_CONV_CH = 2 * GDN_KEY_W + GDN_VAL_W
RWKV_HEADS = 8
RWKV_HEAD = 64
RWKV_W = RWKV_HEADS * RWKV_HEAD
RWKV_DECAY_LORA = 64
RWKV_AAA_LORA = 64
RWKV_GATE_LORA = 128
RWKV_COLS = 3 * RWKV_W + RWKV_DECAY_LORA + RWKV_AAA_LORA + RWKV_GATE_LORA
RWKV_LN_EPS = 64e-5
MLA_HEADS = 4
MLA_Q_RANK = 256
MLA_KV_RANK = 256
MLA_NOPE = 128
MLA_ROPE = 64
MLA_V = 128
MLA_SCALE = 1.0 / math.sqrt(MLA_NOPE + MLA_ROPE)
ROPE_THETA = 10000.0
D_FF = ((8 * D_MODEL // 3 + 255) // 256) * 256
NORM_EPS = 1e-6
L2_EPS = 1e-6

LANES = 128
SUBLANES = 8
VMEM_LIMIT = 56 * 1024 * 1024

_O_QKV = 0
_O_BETA = _O_QKV + GDN_CONV_CH
_O_ALPHA = _O_BETA + GDN_HEADS
_O_Z = _O_ALPHA + GDN_HEADS
_O_RW = _O_Z + GDN_VAL_W
_O_CQ = _O_RW + RWKV_COLS
_O_CKV = _O_CQ + MLA_Q_RANK
_O_KR = _O_CKV + MLA_KV_RANK
_O_GATE = _O_KR + MLA_ROPE

RW_PAD = 3 * RWKV_W + 3 * LANES
PRE_ROW0 = SUBLANES
QK_HEAD = MLA_NOPE + LANES


def _bdot(a, b):
    return jnp.dot(a.astype(BF16), b.astype(BF16), preferred_element_type=F32)


def _bdot_nt(a, b):
    return lax.dot_general(a.astype(BF16), b.astype(BF16), (((1,), (1,)), ((), ())),
                           preferred_element_type=F32)


def _hdot(a, b):
    return jnp.dot(a, b, precision=lax.Precision.HIGHEST, preferred_element_type=F32)


def _split_dot(a, b_exact, parts):
    acc = None
    rem = a
    for _ in range(parts):
        hi = rem.astype(BF16)
        t = jnp.dot(hi, b_exact, preferred_element_type=F32)
        acc = t if acc is None else acc + t
        rem = rem - hi.astype(F32)
    return acc


def _split_dot_rhs(a_exact, b, parts):
    acc = None
    rem = b
    for _ in range(parts):
        hi = rem.astype(BF16)
        t = jnp.dot(a_exact, hi, preferred_element_type=F32)
        acc = t if acc is None else acc + t
        rem = rem - hi.astype(F32)
    return acc


def _rms(x, g, eps=NORM_EPS):
    return x * lax.rsqrt(jnp.mean(x * x, axis=-1, keepdims=True) + eps) * g


def _sigmoid(x):
    return 1.0 / (1.0 + jnp.exp(-x))


def _softplus(x):
    return jnp.maximum(x, 0.0) + jnp.log(1.0 + jnp.exp(-jnp.abs(x)))


def _tri_masks(C):
    r = lax.broadcasted_iota(jnp.int32, (C, C), 0)
    c = lax.broadcasted_iota(jnp.int32, (C, C), 1)
    masks = []
    k = 0
    while (1 << k) < C:
        rr = r >> k
        cc = c >> k
        m = jnp.where((rr ^ cc) == 1, jnp.where((rr & 1) == 1, 1.0, 0.0), 0.0)
        masks.append(m.astype(F32))
        k += 1
    return r, c, masks


def _unit_lower_inverses(Ls, masks, eye):
    Ts = [eye - L * masks[0] for L in Ls]
    for m in masks[1:]:
        tmp = [_bdot(L * m, T) for L, T in zip(Ls, Ts)]
        Ts = [T - _bdot(T, t) for T, t in zip(Ts, tmp)]
    return Ts


def _chunk_tri(R, C):
    rt = lax.broadcasted_iota(jnp.int32, (R, R), 0)
    ct = lax.broadcasted_iota(jnp.int32, (R, R), 1)
    sh = C.bit_length() - 1
    same = (rt >> sh) == (ct >> sh)
    tri = jnp.where(same, jnp.where(rt >= ct, 1.0, 0.0), 0.0).astype(BF16)
    tri_t = jnp.where(same, jnp.where(ct >= rt, 1.0, 0.0), 0.0).astype(BF16)
    return tri, tri_t


def _gdn_kernel(x_ref, s0_ref, hist_ref, ng_ref, wg_ref, wbat_ref, cw_ref, prow_ref, pcol_ref, gng_ref, rep_ref,
                o_ref, s_ref, cb_ref, pre_scr, *, Bb, Tt, C, t_real, t_total):
    ti = pl.program_id(1)
    R = Bb * Tt

    @pl.when(ti == 0)
    def _():
        s_ref[...] = s0_ref[...]
        pre_scr[:, PRE_ROW0 - 3:PRE_ROW0, :] = hist_ref[...]

    h = _rms(x_ref[...].reshape(R, D_MODEL), ng_ref[...]).astype(BF16)
    proj = jnp.dot(h, wg_ref[...], preferred_element_type=F32)
    ba_row = lax.dot_general(wbat_ref[...], h, (((1,), (1,)), ((), ())),
                             preferred_element_type=F32)

    cw = cw_ref[...]
    pre_all = proj[:, :GDN_CONV_CH]
    real_rows = min(Tt, t_real)
    heads = []
    for bb in range(Bb):
        pre_scr[bb, PRE_ROW0:2 * PRE_ROW0, :] = pre_all[bb * Tt:bb * Tt + PRE_ROW0]
        yh = pre_scr[bb, PRE_ROW0 - 3:2 * PRE_ROW0 - 3, :] * cw[0:1]
        for i in range(1, GDN_CONV):
            yh = yh + pre_scr[bb, PRE_ROW0 - 3 + i:2 * PRE_ROW0 - 3 + i, :] * cw[i:i + 1]
        heads.append(yh)
        if real_rows >= PRE_ROW0:
            carry = pre_all[(bb + 1) * Tt - 3:(bb + 1) * Tt]
        else:
            carry = pre_scr[bb, PRE_ROW0 - 3 + real_rows:PRE_ROW0 + real_rows, :]
        pre_scr[bb, PRE_ROW0 - 3:PRE_ROW0, :] = carry
        cb_ref[bb] = carry
    if Tt == PRE_ROW0:
        y = heads[0] if Bb == 1 else jnp.concatenate(heads, axis=0)
    else:
        y = pre_all * cw[GDN_CONV - 1:GDN_CONV]
        for k in range(1, GDN_CONV):
            y = y + pltpu.roll(pre_all, k, axis=0) * cw[GDN_CONV - 1 - k:GDN_CONV - k]
        pieces = []
        for bb in range(Bb):
            pieces += [heads[bb], y[bb * Tt + PRE_ROW0:(bb + 1) * Tt]]
        y = jnp.concatenate(pieces, axis=0)
    qkv = y * _sigmoid(y)

    ba = proj[:, GDN_CONV_CH + GDN_VAL_W:]
    prow = prow_ref[...]
    beta_col = _sigmoid(ba)
    loga_col = -jnp.exp(prow[0:1]) * _softplus(ba + prow[1:2])
    pcol = pcol_ref[...]
    loga_row = -jnp.exp(pcol[:, 0:1]) * _softplus(ba_row + pcol[:, 1:2])
    if t_real < t_total:
        tcol = (lax.broadcasted_iota(jnp.int32, (R, 1), 0) & (Tt - 1)) + ti * Tt
        trow = (lax.broadcasted_iota(jnp.int32, (1, R), 1) & (Tt - 1)) + ti * Tt
        beta_col = jnp.where(tcol < t_real, beta_col, 0.0)
        loga_col = jnp.where(tcol < t_real, loga_col, 0.0)
        loga_row = jnp.where(trow < t_real, loga_row, 0.0)

    rep = rep_ref[...]
    beta_rep = _split_dot(beta_col, rep[:, :GDN_VAL_W], 2)
    loga_rep = _split_dot(loga_col, rep[:, GDN_VAL_W:], 3)
    tri, tri_t = _chunk_tri(R, C)
    gcol_rep = _split_dot_rhs(tri, loga_rep, 3)
    grow = _split_dot(loga_row, tri_t, 3)

    r, c, masks = _tri_masks(C)
    eye = jnp.where(r == c, 1.0, 0.0).astype(F32)
    incl = r >= c
    strict = r > c
    z = proj[:, GDN_CONV_CH:GDN_CONV_CH + GDN_VAL_W]
    gng = gng_ref[...]
    scale = GDN_DK ** -0.5

    qs, ks = [], []
    for hh in range(GDN_HEADS):
        qh = qkv[:, hh * GDN_DK:(hh + 1) * GDN_DK]
        kh = qkv[:, GDN_KEY_W + hh * GDN_DK:GDN_KEY_W + (hh + 1) * GDN_DK]
        qs.append(qh * lax.rsqrt(jnp.sum(qh * qh, axis=-1, keepdims=True) + L2_EPS) * scale)
        ks.append(kh * lax.rsqrt(jnp.sum(kh * kh, axis=-1, keepdims=True) + L2_EPS))

    nchunk = Tt // C
    bodies = [(bb, n, hh) for n in range(nchunk) for bb in range(Bb) for hh in range(GDN_HEADS)]
    Ls, pre = [], []
    for (bb, n, hh) in bodies:
        r0 = bb * Tt + n * C
        q = qs[hh][r0:r0 + C]
        k = ks[hh][r0:r0 + C]
        v = qkv[r0:r0 + C, 2 * GDN_KEY_W + hh * GDN_DV:2 * GDN_KEY_W + (hh + 1) * GDN_DV]
        bc = beta_rep[r0:r0 + C, hh * LANES:(hh + 1) * LANES]
        gc = gcol_rep[r0:r0 + C, hh * LANES:(hh + 1) * LANES]
        gr = grow[GDN_HEADS + hh:GDN_HEADS + hh + 1, r0:r0 + C]
        decay = jnp.where(incl, jnp.exp(jnp.where(incl, gc[:, :C] - gr, 0.0)), 0.0)
        eg = jnp.exp(gc)
        kb = k * bc
        kq = _bdot_nt(jnp.concatenate([kb, q], axis=0), k)
        Ls.append(jnp.where(strict, kq[:C] * decay, 0.0))
        qk = jnp.where(incl, kq[C:] * decay, 0.0)
        g_last = gc[C - 1:C, :]
        kd = k * jnp.exp(g_last - gc)
        pre.append((q * eg, qk, jnp.concatenate([v * bc, kb * eg], axis=1), kd, jnp.exp(g_last)))
    Ts = _unit_lower_inverses(Ls, masks, eye)
    uws = [_bdot(T, p[2]) for T, p in zip(Ts, pre)]
    kuws = [_bdot(p[3].T, uw) for p, uw in zip(pre, uws)]
    quws = [_bdot(p[1], uw) for p, uw in zip(pre, uws)]

    for i, (bb, n, hh) in enumerate(bodies):
        S = s_ref[bb, hh]
        qeg, _, _, _, egl = pre[i]
        o = _bdot(qeg - quws[i][:, GDN_DV:], S) + quws[i][:, :GDN_DV]
        s_ref[bb, hh] = S * egl - _bdot(kuws[i][:, GDN_DV:], S) + kuws[i][:, :GDN_DV]
        r0 = bb * Tt + n * C
        zh = z[r0:r0 + C, hh * GDN_DV:(hh + 1) * GDN_DV]
        o_ref[bb, n * C:(n + 1) * C, hh * GDN_DV:(hh + 1) * GDN_DV] = _rms(o, gng) * (zh * _sigmoid(zh))


def _wspec(a, l, **kw):
    if a.ndim == 2:
        return pl.BlockSpec(a.shape, lambda *_: (0, 0), **kw)
    return pl.BlockSpec((None,) + a.shape[1:], lambda *_: (l,) + (0,) * (a.ndim - 1), **kw)


def _gdn_call(x, s0, hist, ls, w, l, *, Bb, Tt, C, t_real):
    B, T, D = x.shape
    assert Tt & (Tt - 1) == 0 and C & (C - 1) == 0 and B % Bb == 0 and T % Tt == 0 and Tt % C == 0
    kern = functools.partial(_gdn_kernel, Bb=Bb, Tt=Tt, C=C, t_real=t_real, t_total=T)
    names = ['norm_mix_g', 'gdn_w', 'gdn_wba_t', 'gdn_conv_w', 'gdn_prow', 'gdn_pcol', 'gdn_norm_g', 'gdn_rep']
    return pl.pallas_call(
        kern,
        grid=(B // Bb, T // Tt),
        in_specs=[
            pl.BlockSpec((Bb, Tt, D), lambda b, t: (b, t, 0)),
            pl.BlockSpec((None, Bb, GDN_HEADS, GDN_DK, GDN_DV), lambda b, t: (ls, b, 0, 0, 0)),
            pl.BlockSpec((None, Bb, GDN_CONV - 1, GDN_CONV_CH), lambda b, t: (ls, b, 0, 0)),
        ] + [_wspec(w[n], l) for n in names],
        out_specs=[
            pl.BlockSpec((Bb, Tt, GDN_VAL_W), lambda b, t: (b, t, 0)),
            pl.BlockSpec((Bb, GDN_HEADS, GDN_DK, GDN_DV), lambda b, t: (b, 0, 0, 0)),
            pl.BlockSpec((Bb, GDN_CONV - 1, GDN_CONV_CH), lambda b, t: (b, 0, 0)),
        ],
        out_shape=[
            jax.ShapeDtypeStruct((B, T, GDN_VAL_W), F32),
            jax.ShapeDtypeStruct((B, GDN_HEADS, GDN_DK, GDN_DV), F32),
            jax.ShapeDtypeStruct((B, GDN_CONV - 1, GDN_CONV_CH), F32),
        ],
        scratch_shapes=[pltpu.VMEM((Bb, 2 * PRE_ROW0, GDN_CONV_CH), F32)],
        compiler_params=pltpu.CompilerParams(
            dimension_semantics=("arbitrary", "arbitrary"), vmem_limit_bytes=VMEM_LIMIT),
        name="gdn_mixer",
    )(x, s0, hist, *[w[n] for n in names])


def _rwkv_kernel(x_ref, s0_ref, sh_ref, ng_ref, wr_ref, mu_ref, w0_ref, w2_ref, a0_ref, a2_ref, g2_ref,
                 kk_ref, ka_ref, rk_ref, lng_ref, lnb_ref, e_ref,
                 o_ref, s_ref, sho_ref, pre_scr, y_scr, *, Bb, Tt, C, t_real, t_total):
    ti = pl.program_id(1)
    N = RWKV_HEAD
    R = Bb * Tt

    @pl.when(ti == 0)
    def _():
        s_ref[...] = s0_ref[...]
        pre_scr[:, 0:1, :] = sh_ref[...]

    h = _rms(x_ref[...].reshape(R, D_MODEL), ng_ref[...]).astype(BF16)
    pre = jnp.dot(h, wr_ref[...], preferred_element_type=F32)
    real_rows = min(Tt, t_real)
    rolled = pltpu.roll(pre, 1, axis=0)
    first = lax.broadcasted_iota(jnp.int32, (Tt, 1), 0) == 0
    prevs = []
    for bb in range(Bb):
        prevs.append(jnp.where(first, pre_scr[bb, 0:1, :], rolled[bb * Tt:(bb + 1) * Tt]))
        last = pre[bb * Tt + real_rows - 1:bb * Tt + real_rows]
        pre_scr[bb, 0:1, :] = last
        sho_ref[bb] = last
    prev = prevs[0] if Bb == 1 else jnp.concatenate(prevs, axis=0)

    xm = pre + (prev - pre) * mu_ref[...]
    W = RWKV_W
    rr = xm[:, 0:W]
    kx = xm[:, W:2 * W]
    vv = xm[:, 2 * W:3 * W]
    wlo = xm[:, 3 * W:3 * W + LANES]
    alo = xm[:, 3 * W + LANES:3 * W + 2 * LANES]
    glo = xm[:, 3 * W + 2 * LANES:3 * W + 3 * LANES]
    w_log = -_softplus(-(w0_ref[...] + _bdot(jnp.tanh(wlo), w2_ref[...]))) - 0.5
    logd = -jnp.exp(w_log)
    a = _sigmoid(a0_ref[...] + _bdot(alo, a2_ref[...]))
    gb = _bdot(_sigmoid(glo), g2_ref[...])
    E = e_ref[...]
    kkr = kx * kk_ref[...]
    kk = kkr * lax.rsqrt(_split_dot(kkr * kkr, E, 1) + L2_EPS)
    kb = kx * (1.0 + (a - 1.0) * ka_ref[...])
    a_eff = a
    v_eff = vv
    if t_real < t_total:
        tcol = (lax.broadcasted_iota(jnp.int32, (R, 1), 0) & (Tt - 1)) + ti * Tt
        valid = tcol < t_real
        logd = jnp.where(valid, logd, 0.0)
        a_eff = jnp.where(valid, a, 0.0)
        v_eff = jnp.where(valid, vv, 0.0)

    tri, _ = _chunk_tri(R, C)
    logG = _split_dot_rhs(tri, logd, 3)
    G = jnp.exp(logG)
    Ginv = jnp.exp(-logG)
    aq_all = kk * jnp.exp(logG - logd)
    bk_all = -(a_eff * kk) * Ginv
    kd_all = kb * Ginv
    rq_all = rr * G

    r, c, masks = _tri_masks(C)
    eye = jnp.where(r == c, 1.0, 0.0).astype(F32)
    incl = r >= c
    strict = r > c

    nchunk = Tt // C
    bodies = [(bb, n, hh) for n in range(nchunk) for bb in range(Bb) for hh in range(RWKV_HEADS)]
    nb = len(bodies)
    aqs, bks, kds, rqs, vhs, gls = [], [], [], [], [], []
    for (bb, n, hh) in bodies:
        r0 = bb * Tt + n * C
        ls = slice(hh * N, (hh + 1) * N)
        aqs.append(aq_all[r0:r0 + C, ls])
        bks.append(bk_all[r0:r0 + C, ls])
        kds.append(kd_all[r0:r0 + C, ls])
        rqs.append(rq_all[r0:r0 + C, ls])
        vhs.append(v_eff[r0:r0 + C, ls])
        gls.append(G[r0 + C - 1:r0 + C, ls])
    ars = [jnp.concatenate([aqs[i], rqs[i]], axis=0) for i in range(nb)]
    sbs = [_bdot_nt(ars[i], bks[i]) for i in range(nb)]
    sks = [_bdot_nt(ars[i], kds[i]) for i in range(nb)]
    Ls = [jnp.where(strict, -sbs[i][:C], 0.0) for i in range(nb)]
    ras = [jnp.where(incl, sbs[i][C:], 0.0) for i in range(nb)]
    rks = [jnp.where(incl, sks[i][C:], 0.0) for i in range(nb)]
    bmvs = [_bdot(jnp.where(strict, sks[i][:C], 0.0), vhs[i]) for i in range(nb)]
    Ts = _unit_lower_inverses(Ls, masks, eye)
    TAs = [_bdot(Ts[i], aqs[i]) for i in range(nb)]
    TBVs = [_bdot(Ts[i], bmvs[i]) for i in range(nb)]
    Xs = [bks[i] * gls[i] for i in range(nb)]
    Zs = [kds[i] * gls[i] for i in range(nb)]
    Q1s = [_bdot(TAs[i].T, Xs[i]) for i in range(nb)]
    M0s = [_bdot(jnp.concatenate([TBVs[i], vhs[i]], axis=0).T, jnp.concatenate([Xs[i], Zs[i]], axis=0))
           for i in range(nb)]
    rqps = [rqs[i] + _bdot(ras[i], TAs[i]) for i in range(nb)]
    y0s = [_bdot(jnp.concatenate([ras[i], rks[i]], axis=1), jnp.concatenate([TBVs[i], vhs[i]], axis=0))
           for i in range(nb)]

    per_chunk = Bb * RWKV_HEADS
    for n in range(nchunk):
        idx = range(n * per_chunk, (n + 1) * per_chunk)
        Ss = {i: s_ref[bodies[i][0], bodies[i][2]] for i in idx}
        for i in idx:
            bb, _, hh = bodies[i]
            r0 = bb * Tt + n * C
            y_scr[r0:r0 + C, hh * N:(hh + 1) * N] = _bdot_nt(rqps[i], Ss[i]) + y0s[i]
        for i in idx:
            bb, _, hh = bodies[i]
            s_ref[bb, hh] = Ss[i] * gls[i] + _bdot(Ss[i], Q1s[i]) + M0s[i]

    y = y_scr[...]
    inv_n = 1.0 / N
    mu = _split_dot(y, E, 2) * inv_n
    yc = y - mu
    var = _split_dot(yc * yc, E, 1) * inv_n
    yn = yc * lax.rsqrt(var + RWKV_LN_EPS) * lng_ref[...] + lnb_ref[...]
    bonus = _split_dot(rr * kb * rk_ref[...], E, 1) * vv
    o_ref[...] = ((yn + bonus) * gb).reshape(Bb, Tt, RWKV_W)


def _rwkv_call(x, s0, shift, ls, w, l, *, Bb, Tt, C, t_real):
    B, T, D = x.shape
    assert Tt & (Tt - 1) == 0 and C & (C - 1) == 0 and B % Bb == 0 and T % Tt == 0 and Tt % C == 0
    kern = functools.partial(_rwkv_kernel, Bb=Bb, Tt=Tt, C=C, t_real=t_real, t_total=T)
    names = ['norm_mix_g', 'rwkv_w', 'rwkv_mu', 'rwkv_w0', 'rwkv_w2', 'rwkv_a0', 'rwkv_a2', 'rwkv_g2',
             'rwkv_k_k', 'rwkv_k_a', 'rwkv_r_k', 'rwkv_ln_g', 'rwkv_ln_b', 'rwkv_e']
    return pl.pallas_call(
        kern,
        grid=(B // Bb, T // Tt),
        in_specs=[
            pl.BlockSpec((Bb, Tt, D), lambda b, t: (b, t, 0)),
            pl.BlockSpec((None, Bb, RWKV_HEADS, RWKV_HEAD, RWKV_HEAD), lambda b, t: (ls, b, 0, 0, 0)),
            pl.BlockSpec((None, Bb, 1, RW_PAD), lambda b, t: (ls, b, 0, 0)),
        ] + [_wspec(w[n], l) for n in names],
        out_specs=[
            pl.BlockSpec((Bb, Tt, RWKV_W), lambda b, t: (b, t, 0)),
            pl.BlockSpec((Bb, RWKV_HEADS, RWKV_HEAD, RWKV_HEAD), lambda b, t: (b, 0, 0, 0)),
            pl.BlockSpec((Bb, 1, RW_PAD), lambda b, t: (b, 0, 0)),
        ],
        out_shape=[
            jax.ShapeDtypeStruct((B, T, RWKV_W), F32),
            jax.ShapeDtypeStruct((B, RWKV_HEADS, RWKV_HEAD, RWKV_HEAD), F32),
            jax.ShapeDtypeStruct((B, 1, RW_PAD), F32),
        ],
        scratch_shapes=[pltpu.VMEM((Bb, SUBLANES, RW_PAD), F32), pltpu.VMEM((Bb * Tt, RWKV_W), F32)],
        compiler_params=pltpu.CompilerParams(
            dimension_semantics=("arbitrary", "arbitrary"), vmem_limit_bytes=VMEM_LIMIT),
        name="rwkv_mixer",
    )(x, s0, shift, *[w[n] for n in names])


def _mla_proj_kernel(x_ref, ng_ref, wm_ref, qg_ref, kvg_ref, wq_ref, cos_ref, sin_ref, *rest, prompt):
    if prompt:
        wuk_ref, wuv_ref, ckv_ref, kr_ref, q_ref, k_ref, v_ref = rest
    else:
        ckv_ref, kr_ref, qn_ref, qr_ref = rest
    h = _rms(x_ref[...], ng_ref[...]).astype(BF16)
    p = jnp.dot(h, wm_ref[...], preferred_element_type=F32)
    cqn = _rms(p[:, :MLA_Q_RANK], qg_ref[...]).astype(BF16)
    ckv = _rms(p[:, MLA_Q_RANK:MLA_Q_RANK + MLA_KV_RANK], kvg_ref[...])
    o = MLA_Q_RANK + MLA_KV_RANK
    cos = cos_ref[...]
    sin = sin_ref[...]
    krp = p[:, o:o + LANES] * cos + p[:, o + LANES:o + 2 * LANES] * sin
    ckv_ref[...] = ckv
    kr_ref[...] = krp[:, :MLA_ROPE]
    q = jnp.dot(cqn, wq_ref[...], preferred_element_type=F32)
    wn = MLA_HEADS * MLA_NOPE
    wr = MLA_HEADS * LANES
    cos4 = jnp.concatenate([cos] * MLA_HEADS, axis=1)
    sin4 = jnp.concatenate([sin] * MLA_HEADS, axis=1)
    qn = q[:, :wn] * MLA_SCALE
    qr = (q[:, wn:wn + wr] * cos4 + q[:, wn + wr:wn + 2 * wr] * sin4) * MLA_SCALE
    if prompt:
        cb = ckv.astype(BF16)
        kn = jnp.dot(cb, wuk_ref[...], preferred_element_type=F32)
        qparts, kparts = [], []
        for hh in range(MLA_HEADS):
            qparts += [qn[:, hh * MLA_NOPE:(hh + 1) * MLA_NOPE], qr[:, hh * LANES:(hh + 1) * LANES]]
            kparts += [kn[:, hh * MLA_NOPE:(hh + 1) * MLA_NOPE], krp]
        q_ref[...] = jnp.concatenate(qparts, axis=1).astype(BF16)
        k_ref[...] = jnp.concatenate(kparts, axis=1).astype(BF16)
        v_ref[...] = lax.dot_general(wuv_ref[...], cb, (((1,), (1,)), ((), ())),
                                     preferred_element_type=F32).astype(BF16)
    else:
        qn_ref[...] = qn
        qr_ref[...] = qr


def _mla_proj_call(x2, w, l, cos, sin, *, tr, prompt):
    M, D = x2.shape
    nt = cos.shape[0] // tr
    full = lambda a: _wspec(a, l)
    tab = pl.BlockSpec((tr, LANES), lambda i: (i % nt, 0))
    tok = lambda n: pl.BlockSpec((tr, n), lambda i: (i, 0))
    ins = [x2, w['norm_mix_g'], w['mla_w'], w['mla_q_norm_g'], w['mla_kv_norm_g'], w['mla_wq'], cos, sin]
    in_specs = [tok(D)] + [full(a) for a in ins[1:6]] + [tab, tab]
    out_specs = [tok(MLA_KV_RANK), tok(MLA_ROPE)]
    out_shape = [jax.ShapeDtypeStruct((M, MLA_KV_RANK), F32), jax.ShapeDtypeStruct((M, MLA_ROPE), F32)]
    if prompt:
        T = cos.shape[0]
        ins += [w['mla_wuk'], w['mla_wuv_t']]
        in_specs += [full(w['mla_wuk']), full(w['mla_wuv_t'])]
        out_specs += [tok(MLA_HEADS * QK_HEAD), tok(MLA_HEADS * QK_HEAD),
                      pl.BlockSpec((None, MLA_HEADS * MLA_V, tr), lambda i: (i // nt, 0, i % nt))]
        out_shape += [jax.ShapeDtypeStruct((M, MLA_HEADS * QK_HEAD), BF16),
                      jax.ShapeDtypeStruct((M, MLA_HEADS * QK_HEAD), BF16),
                      jax.ShapeDtypeStruct((M // T, MLA_HEADS * MLA_V, T), BF16)]
    else:
        out_specs += [tok(MLA_HEADS * MLA_NOPE), tok(MLA_HEADS * LANES)]
        out_shape += [jax.ShapeDtypeStruct((M, MLA_HEADS * MLA_NOPE), F32),
                      jax.ShapeDtypeStruct((M, MLA_HEADS * LANES), F32)]
    return pl.pallas_call(
        functools.partial(_mla_proj_kernel, prompt=prompt),
        grid=(M // tr,),
        in_specs=in_specs, out_specs=out_specs, out_shape=out_shape,
        compiler_params=pltpu.CompilerParams(dimension_semantics=("arbitrary",), vmem_limit_bytes=VMEM_LIMIT),
        name="mla_proj_prompt" if prompt else "mla_proj_sample",
    )(*ins)


def _flash_kernel(q_ref, k_ref, vt_ref, o_ref, m_scr, l_scr, acc_scr, *, tq):
    qi = pl.program_id(1)
    ki = pl.program_id(2)

    @pl.when(ki == 0)
    def _():
        m_scr[...] = jnp.full(m_scr.shape, -jnp.inf, F32)
        l_scr[...] = jnp.zeros(l_scr.shape, F32)
        acc_scr[...] = jnp.zeros(acc_scr.shape, F32)

    def step(diagonal):
        if diagonal:
            kpos = lax.broadcasted_iota(jnp.int32, (tq, tq), 0)
            qpos = lax.broadcasted_iota(jnp.int32, (tq, tq), 1)
            keep = kpos <= qpos
        for hh in range(MLA_HEADS):
            lq = slice(hh * QK_HEAD, (hh + 1) * QK_HEAD)
            rv = slice(hh * MLA_V, (hh + 1) * MLA_V)
            st = lax.dot_general(k_ref[0, :, lq], q_ref[0, :, lq], (((1,), (1,)), ((), ())),
                                 preferred_element_type=F32)
            if diagonal:
                st = jnp.where(keep, st, -jnp.inf)
            m_old = m_scr[hh:hh + 1, :]
            m_new = jnp.maximum(m_old, jnp.max(st, axis=0, keepdims=True))
            alpha = jnp.exp(m_old - m_new)
            p = jnp.exp(st - m_new)
            l_scr[hh:hh + 1, :] = alpha * l_scr[hh:hh + 1, :] + jnp.sum(p, axis=0, keepdims=True)
            acc_scr[rv, :] = alpha * acc_scr[rv, :] + jnp.dot(vt_ref[0, rv, :], p.astype(BF16),
                                                              preferred_element_type=F32)
            m_scr[hh:hh + 1, :] = m_new

    @pl.when(ki < qi)
    def _():
        step(False)

    @pl.when(ki == qi)
    def _():
        step(True)
        for hh in range(MLA_HEADS):
            rv = slice(hh * MLA_V, (hh + 1) * MLA_V)
            o_ref[0, rv, :] = acc_scr[rv, :] / l_scr[hh:hh + 1, :]


def _flash_call(q, k, vt, *, tq):
    B, T, _ = q.shape
    nq = T // tq
    hv = vt.shape[1]
    return pl.pallas_call(
        functools.partial(_flash_kernel, tq=tq),
        grid=(B, nq, nq),
        in_specs=[pl.BlockSpec((1, tq, q.shape[2]), lambda b, i, j: (b, i, 0)),
                  pl.BlockSpec((1, tq, k.shape[2]), lambda b, i, j: (b, jnp.minimum(i, j), 0)),
                  pl.BlockSpec((1, hv, tq), lambda b, i, j: (b, 0, jnp.minimum(i, j)))],
        out_specs=pl.BlockSpec((1, hv, tq), lambda b, i, j: (b, 0, i)),
        out_shape=jax.ShapeDtypeStruct((B, hv, T), F32),
        scratch_shapes=[pltpu.VMEM((SUBLANES, tq), F32), pltpu.VMEM((SUBLANES, tq), F32),
                        pltpu.VMEM((hv, tq), F32)],
        compiler_params=pltpu.CompilerParams(
            dimension_semantics=("arbitrary", "arbitrary", "arbitrary"), vmem_limit_bytes=VMEM_LIMIT),
        name="mla_flash",
    )(q, k, vt)


def _decode_kernel(pt_ref, qn_ref, qr_ref, ckvn_ref, krn_ref, wuk_ref, wuv_ref, ckv_hbm, krt_hbm, o_ref,
                   bufc, bufk, sem, *, G, NJ, BB, layer, nsteps):
    s = pl.program_id(0)
    total = nsteps * NJ
    nj_shift = NJ.bit_length() - 1
    page = bufc.shape[3]
    rid = lax.broadcasted_iota(jnp.int32, (SUBLANES, 1), 0)

    def page_copies(t, bi, i, slot):
        pg = pt_ref[(t >> nj_shift) * BB + bi, (t & (NJ - 1)) * G + i]
        return (pltpu.make_async_copy(ckv_hbm.at[layer, pg], bufc.at[bi, slot, i], sem.at[0, slot]),
                pltpu.make_async_copy(krt_hbm.at[layer, pg], bufk.at[bi, slot, i], sem.at[1, slot]))

    @pl.when(s == 0)
    def _():
        for bi in range(BB):
            for i in range(G):
                for c in page_copies(0, bi, i, 0):
                    c.start()

    qlats, qrms = [], []
    for bi in range(BB):
        qn = qn_ref[bi]
        qr = qr_ref[bi]
        qlat = jnp.zeros((SUBLANES, MLA_KV_RANK), F32)
        qrm = jnp.zeros((SUBLANES, MLA_ROPE), F32)
        for hh in range(MLA_HEADS):
            ql = _bdot_nt(qn[:, hh * MLA_NOPE:(hh + 1) * MLA_NOPE], wuk_ref[hh])
            qlat = jnp.where(rid == hh, ql[0:1, :], qlat)
            qrm = jnp.where(rid == hh, qr[0:1, hh * LANES:hh * LANES + MLA_ROPE], qrm)
        qlats.append(qlat)
        qrms.append(qrm)

    def body(j, carry):
        t = s * NJ + j
        slot = j & 1
        t_next = jnp.minimum(t + 1, total - 1)
        for bi in range(BB):
            for i in range(G):
                for c in page_copies(t, bi, i, slot):
                    c.wait()
        scores = [[] for _ in range(BB)]
        for i in range(G):
            for bi in range(BB):
                sc = (_bdot_nt(qlats[bi], bufc[bi, slot, i]) + _bdot(qrms[bi], bufk[bi, slot, i]))
                scores[bi].append(sc)
                for c in page_copies(t_next, bi, i, 1 - slot):
                    c.start()
        out = []
        for bi in range(BB):
            m_old, l_old, acc_old = carry[bi]
            sc = jnp.concatenate(scores[bi], axis=1)
            m_new = jnp.maximum(m_old, jnp.max(sc, axis=-1, keepdims=True))
            alpha = jnp.exp(m_old - m_new)
            p = jnp.exp(sc - m_new)
            l_new = alpha * l_old + jnp.sum(p, axis=-1, keepdims=True)
            acc = alpha * acc_old
            for i in range(G):
                acc = acc + _bdot(p[:, i * page:(i + 1) * page], bufc[bi, slot, i])
            out.append((m_new, l_new, acc))
        return tuple(out)

    init = tuple((jnp.full((SUBLANES, 1), -jnp.inf, F32), jnp.zeros((SUBLANES, 1), F32),
                  jnp.zeros((SUBLANES, MLA_KV_RANK), F32)) for _ in range(BB))
    final = lax.fori_loop(0, NJ, body, init)

    @pl.when(s == nsteps - 1)
    def _():
        for bi in range(BB):
            for i in range(G):
                for c in page_copies(total - 1, bi, i, 1 - ((NJ - 1) & 1)):
                    c.wait()

    for bi in range(BB):
        m_new, l_new, acc = final[bi]
        ckvn = ckvn_ref[bi][0:1, :]
        krn = krn_ref[bi][0:1, :]
        s_new = (jnp.sum(qlats[bi] * ckvn, axis=-1, keepdims=True)
                 + jnp.sum(qrms[bi] * krn, axis=-1, keepdims=True))
        m_fin = jnp.maximum(m_new, s_new)
        a2 = jnp.exp(m_new - m_fin)
        p_new = jnp.exp(s_new - m_fin)
        l_fin = a2 * l_new + p_new
        o_lat = (a2 * acc + p_new * ckvn) / l_fin
        outs = []
        for hh in range(MLA_HEADS):
            oh = _bdot(o_lat, wuv_ref[hh])
            outs.append(oh[hh:hh + 1, :])
        o_ref[bi] = jnp.broadcast_to(jnp.concatenate(outs, axis=1), (SUBLANES, MLA_HEADS * MLA_V))


def _decode_call(page_table, qn, qr, ckvn, krn, cache_ckv, cache_krope_t, layer, w):
    B = qn.shape[0]
    n_pages = page_table.shape[1]
    page = cache_ckv.shape[2]
    G = _pick_tile(n_pages // 2, 16)
    NJ = n_pages // G
    BB = _pick_tile(B, 2)
    assert NJ >= 2 and NJ & (NJ - 1) == 0, "work items alternate between two buffer slots"
    nsteps = B // BB
    tok = lambda a: pl.BlockSpec((BB,) + a.shape[1:], lambda b, pt: (b, 0, 0))
    full = lambda a: _wspec(a, layer)
    hbm = pl.BlockSpec(memory_space=pl.ANY)
    grid_spec = pltpu.PrefetchScalarGridSpec(
        num_scalar_prefetch=1,
        grid=(nsteps,),
        in_specs=[tok(qn), tok(qr), tok(ckvn), tok(krn), full(w['mla_wuk_h']), full(w['mla_wuv_h']), hbm, hbm],
        out_specs=pl.BlockSpec((BB, SUBLANES, MLA_HEADS * MLA_V), lambda b, pt: (b, 0, 0)),
        scratch_shapes=[pltpu.VMEM((BB, 2, G, page, MLA_KV_RANK), F32),
                        pltpu.VMEM((BB, 2, G, MLA_ROPE, page), F32),
                        pltpu.SemaphoreType.DMA((2, 2))],
    )
    return pl.pallas_call(
        functools.partial(_decode_kernel, G=G, NJ=NJ, BB=BB, layer=layer, nsteps=nsteps),
        grid_spec=grid_spec,
        out_shape=jax.ShapeDtypeStruct((B, SUBLANES, MLA_HEADS * MLA_V), F32),
        compiler_params=pltpu.CompilerParams(
            dimension_semantics=("arbitrary",), vmem_limit_bytes=VMEM_LIMIT),
        name="mla_decode",
    )(page_table, qn, qr, ckvn, krn, w['mla_wuk_h'], w['mla_wuv_h'], cache_ckv, cache_krope_t)


def _merge_ffn_kernel(x_ref, oa_ref, ob_ref, oc_ref, ng_ref, wgt_ref, bg_ref, wa_ref, wb_ref, wc_ref, wo_ref,
                      nf_ref, wup_ref, wdn_ref, fin_ref, y_ref, *, final, oc_transposed):
    x = x_ref[...]
    h = _rms(x, ng_ref[...]).astype(BF16)
    gates = _sigmoid(jnp.dot(h, wgt_ref[...], preferred_element_type=F32) + bg_ref[...])
    D = D_MODEL
    oc = oc_ref[...].T if oc_transposed else oc_ref[...]
    merged = (gates[:, 0:D] * _bdot(oa_ref[...], wa_ref[...])
              + gates[:, D:2 * D] * _bdot(ob_ref[...], wb_ref[...])
              + gates[:, 2 * D:3 * D] * _bdot(oc, wc_ref[...]))
    x1 = x + _bdot(merged, wo_ref[...])
    h2 = _rms(x1, nf_ref[...]).astype(BF16)
    up = jnp.dot(h2, wup_ref[...], preferred_element_type=F32)
    g = up[:, :D_FF]
    x2 = x1 + _bdot(g * _sigmoid(g) * up[:, D_FF:], wdn_ref[...])
    y_ref[...] = _rms(x2, fin_ref[...]) if final else x2


def _merge_ffn_call(x, oa, ob, oc, w, l, fin_g, *, tm, final):
    M, D = x.shape
    const = lambda a: _wspec(a, l, pipeline_mode=pl.Buffered(1))
    row = lambda n: pl.BlockSpec((tm, n), lambda i: (i, 0))
    names = ['norm_mix_g', 'w_gate', 'b_gate', 'w_br_a', 'w_br_b', 'w_br_c', 'w_out', 'norm_ffn_g',
             'w_ffn_up', 'w_ffn_down']
    oc_transposed = oc.ndim == 3
    if oc_transposed:
        nt = oc.shape[2] // tm
        oc_spec = pl.BlockSpec((None, oc.shape[1], tm), lambda i: (i // nt, 0, i % nt))
    else:
        oc_spec = row(oc.shape[1])
    return pl.pallas_call(
        functools.partial(_merge_ffn_kernel, final=final, oc_transposed=oc_transposed),
        grid=(M // tm,),
        in_specs=[row(D), row(oa.shape[1]), row(ob.shape[1]), oc_spec]
                 + [const(w[n]) for n in names] + [const(fin_g)],
        out_specs=row(D),
        out_shape=jax.ShapeDtypeStruct((M, D), F32),
        compiler_params=pltpu.CompilerParams(
            dimension_semantics=("arbitrary",), vmem_limit_bytes=VMEM_LIMIT),
        name="merge_ffn",
    )(x, oa, ob, oc, *[w[n] for n in names], fin_g)


def _pad_cols(a, n):
    return jnp.pad(a, [(0, 0)] * (a.ndim - 1) + [(0, n - a.shape[-1])])


def _swap_halves(a, width):
    shp = a.shape
    a = a.reshape(shp[:-1] + (shp[-1] // width, 2, width // 2))
    return a[..., ::-1, :].reshape(shp)


def _pad_groups(a, width, to):
    shp = a.shape
    a = a.reshape(shp[:-1] + (shp[-1] // width, width))
    a = jnp.pad(a, [(0, 0)] * (a.ndim - 1) + [(0, to - width)])
    return a.reshape(shp[:-1] + (-1,))


def _rw_pad(a):
    W = RWKV_W
    z = jnp.zeros(a.shape[:-1] + (LANES - RWKV_DECAY_LORA,), a.dtype)
    return jnp.concatenate([a[..., :3 * W], a[..., 3 * W:3 * W + RWKV_DECAY_LORA], z,
                            a[..., 3 * W + RWKV_DECAY_LORA:3 * W + RWKV_DECAY_LORA + RWKV_AAA_LORA], z,
                            a[..., 3 * W + RWKV_DECAY_LORA + RWKV_AAA_LORA:]], axis=-1)


def _rw_unpad(a):
    W = RWKV_W
    return jnp.concatenate([a[..., :3 * W], a[..., 3 * W:3 * W + RWKV_DECAY_LORA],
                            a[..., 3 * W + LANES:3 * W + LANES + RWKV_AAA_LORA],
                            a[..., 3 * W + 2 * LANES:]], axis=-1)


def _prep_weights(p):
    w_in = p['w_in']
    depth = w_in.shape[0]
    row = lambda a: a.reshape(depth, 1, -1).astype(F32)
    w = {}
    w['norm_mix_g'] = row(p['norm_mix_g'])
    w['norm_ffn_g'] = row(p['norm_ffn_g'])
    ba = w_in[..., _O_BETA:_O_Z]
    w['gdn_w'] = jnp.concatenate([w_in[..., _O_QKV:_O_BETA], w_in[..., _O_Z:_O_RW], _pad_cols(ba, LANES)],
                                 axis=-1).astype(BF16)
    w['gdn_wba_t'] = jnp.swapaxes(ba, 1, 2).astype(BF16)
    w['gdn_conv_w'] = p['gdn_conv_w']
    zero4 = jnp.zeros((depth, GDN_HEADS), F32)
    a_log = jnp.concatenate([zero4, p['gdn_a_log']], axis=1)
    dt_b = jnp.concatenate([zero4, p['gdn_dt_bias']], axis=1)
    w['gdn_prow'] = _pad_cols(jnp.stack([a_log, dt_b], axis=1), LANES)
    w['gdn_pcol'] = jnp.stack([a_log, dt_b], axis=2)
    w['gdn_norm_g'] = row(p['gdn_norm_g'])
    lane = jnp.arange(LANES)[:, None]
    head_of_col = jnp.arange(GDN_VAL_W)[None, :] // GDN_DV
    w['gdn_rep'] = jnp.concatenate([lane == head_of_col, lane == head_of_col + GDN_HEADS], axis=1).astype(BF16)
    w['rwkv_w'] = _rw_pad(w_in[..., _O_RW:_O_CQ]).astype(BF16)
    w['rwkv_mu'] = row(_rw_pad(p['rwkv_mu']))
    w['rwkv_w0'] = row(p['rwkv_w0'])
    w['rwkv_w2'] = jnp.pad(p['rwkv_w2'], ((0, 0), (0, LANES - RWKV_DECAY_LORA), (0, 0))).astype(BF16)
    w['rwkv_a0'] = row(p['rwkv_a0'])
    w['rwkv_a2'] = jnp.pad(p['rwkv_a2'], ((0, 0), (0, LANES - RWKV_AAA_LORA), (0, 0))).astype(BF16)
    w['rwkv_g2'] = p['rwkv_g2'].astype(BF16)
    w['rwkv_k_k'] = row(p['rwkv_k_k'])
    w['rwkv_k_a'] = row(p['rwkv_k_a'])
    w['rwkv_r_k'] = row(p['rwkv_r_k'])
    w['rwkv_ln_g'] = row(p['rwkv_ln_g'])
    w['rwkv_ln_b'] = row(p['rwkv_ln_b'])
    hid = jnp.arange(RWKV_W) // RWKV_HEAD
    w['rwkv_e'] = (hid[:, None] == hid[None, :]).astype(BF16)
    w_kr = w_in[..., _O_KR:_O_GATE]
    w['mla_w'] = jnp.concatenate([w_in[..., _O_CQ:_O_KR], _pad_cols(w_kr, LANES),
                                  _pad_cols(_swap_halves(w_kr, MLA_ROPE), LANES)], axis=-1).astype(BF16)
    w['mla_q_norm_g'] = row(p['mla_q_norm_g'])
    w['mla_kv_norm_g'] = row(p['mla_kv_norm_g'])
    w_uq = p['mla_w_uq']
    wq_n = w_uq[..., :MLA_NOPE].reshape(depth, MLA_Q_RANK, -1)
    wq_r = w_uq[..., MLA_NOPE:].reshape(depth, MLA_Q_RANK, -1)
    w['mla_wq'] = jnp.concatenate([wq_n, _pad_groups(wq_r, MLA_ROPE, LANES),
                                   _pad_groups(_swap_halves(wq_r, MLA_ROPE), MLA_ROPE, LANES)],
                                  axis=-1).astype(BF16)
    w['mla_wuk'] = p['mla_w_uk'].reshape(depth, MLA_KV_RANK, -1).astype(BF16)
    w['mla_wuv_t'] = jnp.swapaxes(p['mla_w_uv'].reshape(depth, MLA_KV_RANK, -1), 1, 2).astype(BF16)
    w['mla_wuk_h'] = jnp.transpose(p['mla_w_uk'], (0, 2, 1, 3)).astype(BF16)
    w['mla_wuv_h'] = jnp.transpose(p['mla_w_uv'], (0, 2, 1, 3)).astype(BF16)
    w['w_gate'] = w_in[..., _O_GATE:].astype(BF16)
    w['b_gate'] = row(p['b_gate'])
    for n in ('w_br_a', 'w_br_b', 'w_br_c', 'w_out', 'w_ffn_up', 'w_ffn_down'):
        w[n] = p[n].astype(BF16)
    return w


def _rope_tables(pos):
    half = MLA_ROPE // 2
    freq = ROPE_THETA ** (-jnp.arange(half, dtype=F32) / half)
    ang = pos.astype(F32)[:, None] * freq
    cos = jnp.cos(ang)
    sin = jnp.sin(ang)
    return (_pad_cols(jnp.concatenate([cos, cos], axis=1), LANES),
            _pad_cols(jnp.concatenate([-sin, sin], axis=1), LANES))


def _pick_tile(T, pref):
    t = min(T, pref)
    while T % t:
        t //= 2
    return t


def kernel(x_prompt, x_sample, cache_ckv, cache_krope, page_table, state_gdn, state_gdn_conv, state_rwkv, state_rwkv_shift, norm_mix_g, norm_ffn_g, norm_final_g, w_in, b_gate, gdn_conv_w, gdn_a_log, gdn_dt_bias, gdn_norm_g, rwkv_mu, rwkv_w0, rwkv_w2, rwkv_a0, rwkv_a2, rwkv_g2, rwkv_k_k, rwkv_k_a, rwkv_r_k, rwkv_ln_g, rwkv_ln_b, mla_q_norm_g, mla_kv_norm_g, mla_w_uq, mla_w_uk, mla_w_uv, w_br_a, w_br_b, w_br_c, w_out, w_ffn_up, w_ffn_down):
    p = dict(norm_mix_g=norm_mix_g, norm_ffn_g=norm_ffn_g, w_in=w_in, b_gate=b_gate, gdn_conv_w=gdn_conv_w,
             gdn_a_log=gdn_a_log, gdn_dt_bias=gdn_dt_bias, gdn_norm_g=gdn_norm_g, rwkv_mu=rwkv_mu,
             rwkv_w0=rwkv_w0, rwkv_w2=rwkv_w2, rwkv_a0=rwkv_a0, rwkv_a2=rwkv_a2, rwkv_g2=rwkv_g2,
             rwkv_k_k=rwkv_k_k, rwkv_k_a=rwkv_k_a, rwkv_r_k=rwkv_r_k, rwkv_ln_g=rwkv_ln_g,
             rwkv_ln_b=rwkv_ln_b, mla_q_norm_g=mla_q_norm_g, mla_kv_norm_g=mla_kv_norm_g,
             mla_w_uq=mla_w_uq, mla_w_uk=mla_w_uk, mla_w_uv=mla_w_uv, w_br_a=w_br_a, w_br_b=w_br_b,
             w_br_c=w_br_c, w_out=w_out, w_ffn_up=w_ffn_up, w_ffn_down=w_ffn_down)
    depth = w_in.shape[0]
    bp, sp, D = x_prompt.shape
    bs, ss, _ = x_sample.shape
    assert ss == 1, "the sample group decodes one new token per sequence"
    past_len = page_table.shape[1] * cache_ckv.shape[2]
    fin_g = norm_final_g.reshape(1, -1).astype(F32)
    cache_krope_t = jnp.swapaxes(cache_krope, 2, 3)

    tp = _pick_tile(sp, 256)
    cp = _pick_tile(tp, 64)
    tq = _pick_tile(sp, 512)
    ts = SUBLANES
    bb_s = _pick_tile(bs, 8)
    cos_p, sin_p = _rope_tables(jnp.arange(sp, dtype=jnp.int32))
    cos_s, sin_s = _rope_tables(jnp.full((bs * ts,), past_len, dtype=jnp.int32))
    zeros_p = dict(
        gdn=jnp.zeros((1, bp, GDN_HEADS, GDN_DK, GDN_DV), F32),
        conv=jnp.zeros((1, bp, GDN_CONV - 1, GDN_CONV_CH), F32),
        rwkv=jnp.zeros((1, bp, RWKV_HEADS, RWKV_HEAD, RWKV_HEAD), F32),
        shift=jnp.zeros((1, bp, 1, RW_PAD), F32))
    shift_s = _rw_pad(state_rwkv_shift)[:, :, None]

    w = _prep_weights(p)
    xp = x_prompt
    xs = jnp.pad(x_sample, ((0, 0), (0, ts - ss), (0, 0)))
    new_p = [[] for _ in range(6)]
    new_s = [[] for _ in range(6)]
    mp = bp * sp
    tm_p = _pick_tile(sp, 256)
    for l in range(depth):
        final = l == depth - 1
        oa, s_g, cbuf = _gdn_call(xp, zeros_p['gdn'], zeros_p['conv'], 0, w, l, Bb=1, Tt=tp, C=cp, t_real=sp)
        ob, s_r, sh = _rwkv_call(xp, zeros_p['rwkv'], zeros_p['shift'], 0, w, l, Bb=1, Tt=tp, C=cp, t_real=sp)
        ckv, kr, q, k, vt = _mla_proj_call(xp.reshape(mp, D), w, l, cos_p, sin_p, tr=tp, prompt=True)
        oc_t = _flash_call(q.reshape(bp, sp, -1), k.reshape(bp, sp, -1), vt, tq=tq)
        for lst, arr in zip(new_p, (ckv.reshape(bp, sp, -1), kr.reshape(bp, sp, -1), s_g, cbuf, s_r, sh[:, 0])):
            lst.append(arr)
        xp = _merge_ffn_call(xp.reshape(mp, D), oa.reshape(mp, -1), ob.reshape(mp, -1),
                             oc_t, w, l, fin_g, tm=tm_p, final=final).reshape(bp, sp, D)
        oa, s_g, cbuf = _gdn_call(xs, state_gdn, state_gdn_conv, l, w, l, Bb=bb_s, Tt=ts, C=ts, t_real=ss)
        ob, s_r, sh = _rwkv_call(xs, state_rwkv, shift_s, l, w, l, Bb=bb_s, Tt=ts, C=ts, t_real=ss)
        ckv, kr, qn, qr = _mla_proj_call(xs.reshape(bs * ts, D), w, l, cos_s, sin_s, tr=bs * ts, prompt=False)
        ckv = ckv.reshape(bs, ts, -1)
        kr = kr.reshape(bs, ts, -1)
        oc = _decode_call(page_table, qn.reshape(bs, ts, -1), qr.reshape(bs, ts, -1), ckv, kr,
                          cache_ckv, cache_krope_t, l, w)
        for lst, arr in zip(new_s, (ckv[:, :ss], kr[:, :ss], s_g, cbuf, s_r, sh[:, 0])):
            lst.append(arr)
        xs_real = _merge_ffn_call(xs[:, 0], oa[:, 0], ob[:, 0], oc[:, 0], w, l, fin_g, tm=bs, final=final)
        xs = jnp.pad(xs_real[:, None], ((0, 0), (0, ts - ss), (0, 0)))
    y_prompt = xp
    y_sample = xs[:, :ss]
    outs_p = [jnp.stack(a) for a in new_p]
    outs_s = [jnp.stack(a) for a in new_s]
    outs_p[5] = _rw_unpad(outs_p[5])
    outs_s[5] = _rw_unpad(outs_s[5])
    return (y_prompt, y_sample, *outs_p, *outs_s)
```

```python
import functools
import math

import jax
import jax.numpy as jnp
from jax import lax
from jax.experimental import pallas as pl
from jax.experimental.pallas import tpu as pltpu

F32 = jnp.float32
BF16 = jnp.bfloat16

D_MODEL = 1024
GDN_HEADS = 4
GDN_DK = 128
GDN_DV = 128
GDN_CONV = 4
GDN_KEY_W = GDN_HEADS * GDN_DK
GDN_VAL_W = GDN_HEADS * GDN_DV
GDN_CONV_CH = 2 * GDN_KEY_W + GDN_VAL_W
RWKV_HEADS = 8
RWKV_HEAD = 64
RWKV_W = RWKV_HEADS * RWKV_HEAD
RWKV_DECAY_LORA = 64
RWKV_AAA_LORA = 64
RWKV_GATE_LORA = 128
RWKV_COLS = 3 * RWKV_W + RWKV_DECAY_LORA + RWKV_AAA_LORA + RWKV_GATE_LORA
RWKV_LN_EPS = 64e-5
MLA_HEADS = 4
MLA_Q_RANK = 256
MLA_KV_RANK = 256
MLA_NOPE = 128
MLA_ROPE = 64
MLA_V = 128
MLA_SCALE = 1.0 / math.sqrt(MLA_NOPE + MLA_ROPE)
ROPE_THETA = 10000.0
D_FF = ((8 * D_MODEL // 3 + 255) // 256) * 256
NORM_EPS = 1e-6
L2_EPS = 1e-6

LANES = 128
SUBLANES = 8
VMEM_LIMIT = 56 * 1024 * 1024

_O_QKV = 0
_O_BETA = _O_QKV + GDN_CONV_CH
_O_ALPHA = _O_BETA + GDN_HEADS
_O_Z = _O_ALPHA + GDN_HEADS
_O_RW = _O_Z + GDN_VAL_W
_O_CQ = _O_RW + RWKV_COLS
_O_CKV = _O_CQ + MLA_Q_RANK
_O_KR = _O_CKV + MLA_KV_RANK
_O_GATE = _O_KR + MLA_ROPE

RW_PAD = 3 * RWKV_W + 3 * LANES
PRE_ROW0 = SUBLANES
QK_HEAD = MLA_NOPE + LANES
DECODE_SLOTS = 3


def _bdot(a, b):
    return jnp.dot(a.astype(BF16), b.astype(BF16), preferred_element_type=F32)


def _bdot_nt(a, b):
    return lax.dot_general(a.astype(BF16), b.astype(BF16), (((1,), (1,)), ((), ())),
                           preferred_element_type=F32)


def _hdot(a, b):
    return jnp.dot(a, b, precision=lax.Precision.HIGHEST, preferred_element_type=F32)


def _split_dot(a, b_exact, parts):
    acc = None
    rem = a
    for _ in range(parts):
        hi = rem.astype(BF16)
        t = jnp.dot(hi, b_exact, preferred_element_type=F32)
        acc = t if acc is None else acc + t
        rem = rem - hi.astype(F32)
    return acc


def _split_dot_rhs(a_exact, b, parts):
    acc = None
    rem = b
    for _ in range(parts):
        hi = rem.astype(BF16)
        t = jnp.dot(a_exact, hi, preferred_element_type=F32)
        acc = t if acc is None else acc + t
        rem = rem - hi.astype(F32)
    return acc


def _rms(x, g, eps=NORM_EPS):
    return x * lax.rsqrt(jnp.mean(x * x, axis=-1, keepdims=True) + eps) * g


def _sigmoid(x):
    return 0.5 * jnp.tanh(0.5 * x) + 0.5


def _softplus(x):
    return jnp.maximum(x, 0.0) + jnp.log(1.0 + jnp.exp(-jnp.abs(x)))


def _tri_masks(C):
    r = lax.broadcasted_iota(jnp.int32, (C, C), 0)
    c = lax.broadcasted_iota(jnp.int32, (C, C), 1)
    masks = []
    k = 0
    while (1 << k) < C:
        rr = r >> k
        cc = c >> k
        m = jnp.where((rr ^ cc) == 1, jnp.where((rr & 1) == 1, 1.0, 0.0), 0.0)
        masks.append(m.astype(F32))
        k += 1
    return r, c, masks


def _unit_lower_inverses(Ls, masks, eye):
    Ts = [eye - L * masks[0] for L in Ls]
    for m in masks[1:]:
        tmp = [_bdot(L * m, T) for L, T in zip(Ls, Ts)]
        Ts = [T - _bdot(T, t) for T, t in zip(Ts, tmp)]
    return Ts


def _chunk_tri(R, C):
    rt = lax.broadcasted_iota(jnp.int32, (R, R), 0)
    ct = lax.broadcasted_iota(jnp.int32, (R, R), 1)
    sh = C.bit_length() - 1
    same = (rt >> sh) == (ct >> sh)
    tri = jnp.where(same, jnp.where(rt >= ct, 1.0, 0.0), 0.0).astype(BF16)
    tri_t = jnp.where(same, jnp.where(ct >= rt, 1.0, 0.0), 0.0).astype(BF16)
    return tri, tri_t


def _gdn_kernel(x_ref, s0_ref, hist_ref, ng_ref, wg_ref, wbat_ref, cw_ref, prow_ref, pcol_ref, gng_ref, rep_ref,
                o_ref, s_ref, cb_ref, pre_scr, *, Bb, Tt, C, t_real, t_total):
    ti = pl.program_id(1)
    R = Bb * Tt

    @pl.when(ti == 0)
    def _():
        s_ref[...] = s0_ref[...]
        pre_scr[:, PRE_ROW0 - 3:PRE_ROW0, :] = hist_ref[...]

    h = _rms(x_ref[...].reshape(R, D_MODEL), ng_ref[...]).astype(BF16)
    proj = jnp.dot(h, wg_ref[...], preferred_element_type=F32)
    ba_row = lax.dot_general(wbat_ref[...], h, (((1,), (1,)), ((), ())),
                             preferred_element_type=F32)

    cw = cw_ref[...]
    pre_all = proj[:, :GDN_CONV_CH]
    real_rows = min(Tt, t_real)
    heads = []
    for bb in range(Bb):
        pre_scr[bb, PRE_ROW0:2 * PRE_ROW0, :] = pre_all[bb * Tt:bb * Tt + PRE_ROW0]
        yh = pre_scr[bb, PRE_ROW0 - 3:2 * PRE_ROW0 - 3, :] * cw[0:1]
        for i in range(1, GDN_CONV):
            yh = yh + pre_scr[bb, PRE_ROW0 - 3 + i:2 * PRE_ROW0 - 3 + i, :] * cw[i:i + 1]
        heads.append(yh)
        if real_rows >= PRE_ROW0:
            carry = pre_all[(bb + 1) * Tt - 3:(bb + 1) * Tt]
        else:
            carry = pre_scr[bb, PRE_ROW0 - 3 + real_rows:PRE_ROW0 + real_rows, :]
        pre_scr[bb, PRE_ROW0 - 3:PRE_ROW0, :] = carry
        cb_ref[bb] = carry
    if Tt == PRE_ROW0:
        y = heads[0] if Bb == 1 else jnp.concatenate(heads, axis=0)
    else:
        y = pre_all * cw[GDN_CONV - 1:GDN_CONV]
        for k in range(1, GDN_CONV):
            y = y + pltpu.roll(pre_all, k, axis=0) * cw[GDN_CONV - 1 - k:GDN_CONV - k]
        pieces = []
        for bb in range(Bb):
            pieces += [heads[bb], y[bb * Tt + PRE_ROW0:(bb + 1) * Tt]]
        y = jnp.concatenate(pieces, axis=0)
    qkv = y * _sigmoid(y)

    ba = proj[:, GDN_CONV_CH + GDN_VAL_W:]
    prow = prow_ref[...]
    beta_col = _sigmoid(ba)
    loga_col = -jnp.exp(prow[0:1]) * _softplus(ba + prow[1:2])
    pcol = pcol_ref[...]
    loga_row = -jnp.exp(pcol[:, 0:1]) * _softplus(ba_row + pcol[:, 1:2])
    if t_real < t_total:
        tcol = (lax.broadcasted_iota(jnp.int32, (R, 1), 0) & (Tt - 1)) + ti * Tt
        trow = (lax.broadcasted_iota(jnp.int32, (1, R), 1) & (Tt - 1)) + ti * Tt
        beta_col = jnp.where(tcol < t_real, beta_col, 0.0)
        loga_col = jnp.where(tcol < t_real, loga_col, 0.0)
        loga_row = jnp.where(trow < t_real, loga_row, 0.0)

    rep = rep_ref[...]
    beta_rep = _split_dot(beta_col, rep[:, :GDN_VAL_W], 2)
    loga_rep = _split_dot(loga_col, rep[:, GDN_VAL_W:], 3)
    tri, tri_t = _chunk_tri(R, C)
    gcol_rep = _split_dot_rhs(tri, loga_rep, 3)
    grow = _split_dot(loga_row, tri_t, 3)

    r, c, masks = _tri_masks(C)
    eye = jnp.where(r == c, 1.0, 0.0).astype(F32)
    incl = r >= c
    strict = r > c
    z = proj[:, GDN_CONV_CH:GDN_CONV_CH + GDN_VAL_W]
    gng = gng_ref[...]
    scale = GDN_DK ** -0.5

    qs, ks = [], []
    for hh in range(GDN_HEADS):
        qh = qkv[:, hh * GDN_DK:(hh + 1) * GDN_DK]
        kh = qkv[:, GDN_KEY_W + hh * GDN_DK:GDN_KEY_W + (hh + 1) * GDN_DK]
        qs.append(qh * lax.rsqrt(jnp.sum(qh * qh, axis=-1, keepdims=True) + L2_EPS) * scale)
        ks.append(kh * lax.rsqrt(jnp.sum(kh * kh, axis=-1, keepdims=True) + L2_EPS))

    nchunk = Tt // C
    bodies = [(bb, n, hh) for n in range(nchunk) for bb in range(Bb) for hh in range(GDN_HEADS)]
    Ls, pre = [], []
    for (bb, n, hh) in bodies:
        r0 = bb * Tt + n * C
        q = qs[hh][r0:r0 + C]
        k = ks[hh][r0:r0 + C]
        v = qkv[r0:r0 + C, 2 * GDN_KEY_W + hh * GDN_DV:2 * GDN_KEY_W + (hh + 1) * GDN_DV]
        bc = beta_rep[r0:r0 + C, hh * LANES:(hh + 1) * LANES]
        gc = gcol_rep[r0:r0 + C, hh * LANES:(hh + 1) * LANES]
        gr = grow[GDN_HEADS + hh:GDN_HEADS + hh + 1, r0:r0 + C]
        decay = jnp.where(incl, jnp.exp(jnp.where(incl, gc[:, :C] - gr, 0.0)), 0.0)
        eg = jnp.exp(gc)
        kb = k * bc
        kq = _bdot_nt(jnp.concatenate([kb, q], axis=0), k)
        Ls.append(jnp.where(strict, kq[:C] * decay, 0.0))
        qk = jnp.where(incl, kq[C:] * decay, 0.0)
        g_last = gc[C - 1:C, :]
        kd = k * jnp.exp(g_last - gc)
        pre.append((q * eg, qk, jnp.concatenate([v * bc, kb * eg], axis=1), kd, jnp.exp(g_last)))
    Ts = _unit_lower_inverses(Ls, masks, eye)
    uws = [_bdot(T, p[2]) for T, p in zip(Ts, pre)]
    kuws = [_bdot(p[3].T, uw) for p, uw in zip(pre, uws)]
    quws = [_bdot(p[1], uw) for p, uw in zip(pre, uws)]

    for i, (bb, n, hh) in enumerate(bodies):
        S = s_ref[bb, hh]
        qeg, _, _, _, egl = pre[i]
        o = _bdot(qeg - quws[i][:, GDN_DV:], S) + quws[i][:, :GDN_DV]
        s_ref[bb, hh] = S * egl - _bdot(kuws[i][:, GDN_DV:], S) + kuws[i][:, :GDN_DV]
        r0 = bb * Tt + n * C
        zh = z[r0:r0 + C, hh * GDN_DV:(hh + 1) * GDN_DV]
        o_ref[bb, n * C:(n + 1) * C, hh * GDN_DV:(hh + 1) * GDN_DV] = _rms(o, gng) * (zh * _sigmoid(zh))


def _wspec(a, l, **kw):
    if a.ndim == 2:
        return pl.BlockSpec(a.shape, lambda *_: (0, 0), **kw)
    return pl.BlockSpec((None,) + a.shape[1:], lambda *_: (l,) + (0,) * (a.ndim - 1), **kw)


def _gdn_call(x, s0, hist, ls, w, l, *, Bb, Tt, C, t_real):
    B, T, D = x.shape
    assert Tt & (Tt - 1) == 0 and C & (C - 1) == 0 and B % Bb == 0 and T % Tt == 0 and Tt % C == 0
    kern = functools.partial(_gdn_kernel, Bb=Bb, Tt=Tt, C=C, t_real=t_real, t_total=T)
    names = ['norm_mix_g', 'gdn_w', 'gdn_wba_t', 'gdn_conv_w', 'gdn_prow', 'gdn_pcol', 'gdn_norm_g', 'gdn_rep']
    return pl.pallas_call(
        kern,
        grid=(B // Bb, T // Tt),
        in_specs=[
            pl.BlockSpec((Bb, Tt, D), lambda b, t: (b, t, 0)),
            pl.BlockSpec((None, Bb, GDN_HEADS, GDN_DK, GDN_DV), lambda b, t: (ls, b, 0, 0, 0)),
            pl.BlockSpec((None, Bb, GDN_CONV - 1, GDN_CONV_CH), lambda b, t: (ls, b, 0, 0)),
        ] + [_wspec(w[n], l) for n in names],
        out_specs=[
            pl.BlockSpec((Bb, Tt, GDN_VAL_W), lambda b, t: (b, t, 0)),
            pl.BlockSpec((Bb, GDN_HEADS, GDN_DK, GDN_DV), lambda b, t: (b, 0, 0, 0)),
            pl.BlockSpec((Bb, GDN_CONV - 1, GDN_CONV_CH), lambda b, t: (b, 0, 0)),
        ],
        out_shape=[
            jax.ShapeDtypeStruct((B, T, GDN_VAL_W), F32),
            jax.ShapeDtypeStruct((B, GDN_HEADS, GDN_DK, GDN_DV), F32),
            jax.ShapeDtypeStruct((B, GDN_CONV - 1, GDN_CONV_CH), F32),
        ],
        scratch_shapes=[pltpu.VMEM((Bb, 2 * PRE_ROW0, GDN_CONV_CH), F32)],
        compiler_params=pltpu.CompilerParams(
            dimension_semantics=("arbitrary", "arbitrary"), vmem_limit_bytes=VMEM_LIMIT),
        name="gdn_mixer",
    )(x, s0, hist, *[w[n] for n in names])


def _rwkv_kernel(x_ref, s0_ref, sh_ref, ng_ref, wr_ref, mu_ref, w0_ref, w2_ref, a0_ref, a2_ref, g2_ref,
                 kk_ref, ka_ref, rk_ref, lng_ref, lnb_ref, e_ref,
                 o_ref, s_ref, sho_ref, pre_scr, y_scr, *, Bb, Tt, C, t_real, t_total):
    ti = pl.program_id(1)
    N = RWKV_HEAD
    R = Bb * Tt

    @pl.when(ti == 0)
    def _():
        s_ref[...] = s0_ref[...]
        pre_scr[:, 0:1, :] = sh_ref[...]

    h = _rms(x_ref[...].reshape(R, D_MODEL), ng_ref[...]).astype(BF16)
    pre = jnp.dot(h, wr_ref[...], preferred_element_type=F32)
    real_rows = min(Tt, t_real)
    rolled = pltpu.roll(pre, 1, axis=0)
    first = lax.broadcasted_iota(jnp.int32, (Tt, 1), 0) == 0
    prevs = []
    for bb in range(Bb):
        prevs.append(jnp.where(first, pre_scr[bb, 0:1, :], rolled[bb * Tt:(bb + 1) * Tt]))
        last = pre[bb * Tt + real_rows - 1:bb * Tt + real_rows]
        pre_scr[bb, 0:1, :] = last
        sho_ref[bb] = last
    prev = prevs[0] if Bb == 1 else jnp.concatenate(prevs, axis=0)

    xm = pre + (prev - pre) * mu_ref[...]
    W = RWKV_W
    rr = xm[:, 0:W]
    kx = xm[:, W:2 * W]
    vv = xm[:, 2 * W:3 * W]
    wlo = xm[:, 3 * W:3 * W + LANES]
    alo = xm[:, 3 * W + LANES:3 * W + 2 * LANES]
    glo = xm[:, 3 * W + 2 * LANES:3 * W + 3 * LANES]
    w_log = -_softplus(-(w0_ref[...] + _bdot(jnp.tanh(wlo), w2_ref[...]))) - 0.5
    logd = -jnp.exp(w_log)
    a = _sigmoid(a0_ref[...] + _bdot(alo, a2_ref[...]))
    gb = _bdot(_sigmoid(glo), g2_ref[...])
    E = e_ref[...]
    kkr = kx * kk_ref[...]
    kk = kkr * lax.rsqrt(_split_dot(kkr * kkr, E, 1) + L2_EPS)
    kb = kx * (1.0 + (a - 1.0) * ka_ref[...])
    a_eff = a
    v_eff = vv
    if t_real < t_total:
        tcol = (lax.broadcasted_iota(jnp.int32, (R, 1), 0) & (Tt - 1)) + ti * Tt
        valid = tcol < t_real
        logd = jnp.where(valid, logd, 0.0)
        a_eff = jnp.where(valid, a, 0.0)
        v_eff = jnp.where(valid, vv, 0.0)

    tri, _ = _chunk_tri(R, C)
    logG = _split_dot_rhs(tri, logd, 3)
    G = jnp.exp(logG)
    Ginv = jnp.exp(-logG)
    aq_all = kk * jnp.exp(logG - logd)
    bk_all = -(a_eff * kk) * Ginv
    kd_all = kb * Ginv
    rq_all = rr * G

    r, c, masks = _tri_masks(C)
    eye = jnp.where(r == c, 1.0, 0.0).astype(F32)
    incl = r >= c
    strict = r > c

    nchunk = Tt // C
    bodies = [(bb, n, hh) for n in range(nchunk) for bb in range(Bb) for hh in range(RWKV_HEADS)]
    nb = len(bodies)
    aqs, bks, kds, rqs, vhs, gls = [], [], [], [], [], []
    for (bb, n, hh) in bodies:
        r0 = bb * Tt + n * C
        ls = slice(hh * N, (hh + 1) * N)
        aqs.append(aq_all[r0:r0 + C, ls])
        bks.append(bk_all[r0:r0 + C, ls])
        kds.append(kd_all[r0:r0 + C, ls])
        rqs.append(rq_all[r0:r0 + C, ls])
        vhs.append(v_eff[r0:r0 + C, ls])
        gls.append(G[r0 + C - 1:r0 + C, ls])
    ars = [jnp.concatenate([aqs[i], rqs[i]], axis=0) for i in range(nb)]
    sbs = [_bdot_nt(ars[i], bks[i]) for i in range(nb)]
    sks = [_bdot_nt(ars[i], kds[i]) for i in range(nb)]
    Ls = [jnp.where(strict, -sbs[i][:C], 0.0) for i in range(nb)]
    ras = [jnp.where(incl, sbs[i][C:], 0.0) for i in range(nb)]
    rks = [jnp.where(incl, sks[i][C:], 0.0) for i in range(nb)]
    bmvs = [_bdot(jnp.where(strict, sks[i][:C], 0.0), vhs[i]) for i in range(nb)]
    Ts = _unit_lower_inverses(Ls, masks, eye)
    TAs = [_bdot(Ts[i], aqs[i]) for i in range(nb)]
    TBVs = [_bdot(Ts[i], bmvs[i]) for i in range(nb)]
    Xs = [bks[i] * gls[i] for i in range(nb)]
    Zs = [kds[i] * gls[i] for i in range(nb)]
    Q1s = [_bdot(TAs[i].T, Xs[i]) for i in range(nb)]
    M0s = [_bdot(jnp.concatenate([TBVs[i], vhs[i]], axis=0).T, jnp.concatenate([Xs[i], Zs[i]], axis=0))
           for i in range(nb)]
    rqps = [rqs[i] + _bdot(ras[i], TAs[i]) for i in range(nb)]
    y0s = [_bdot(jnp.concatenate([ras[i], rks[i]], axis=1), jnp.concatenate([TBVs[i], vhs[i]], axis=0))
           for i in range(nb)]

    per_chunk = Bb * RWKV_HEADS
    for n in range(nchunk):
        idx = range(n * per_chunk, (n + 1) * per_chunk)
        Ss = {i: s_ref[bodies[i][0], bodies[i][2]] for i in idx}
        for i in idx:
            bb, _, hh = bodies[i]
            r0 = bb * Tt + n * C
            y_scr[r0:r0 + C, hh * N:(hh + 1) * N] = _bdot_nt(rqps[i], Ss[i]) + y0s[i]
        for i in idx:
            bb, _, hh = bodies[i]
            s_ref[bb, hh] = Ss[i] * gls[i] + _bdot(Ss[i], Q1s[i]) + M0s[i]

    y = y_scr[...]
    inv_n = 1.0 / N
    mu = _split_dot(y, E, 2) * inv_n
    yc = y - mu
    var = _split_dot(yc * yc, E, 1) * inv_n
    yn = yc * lax.rsqrt(var + RWKV_LN_EPS) * lng_ref[...] + lnb_ref[...]
    bonus = _split_dot(rr * kb * rk_ref[...], E, 1) * vv
    o_ref[...] = ((yn + bonus) * gb).reshape(Bb, Tt, RWKV_W)


def _rwkv_call(x, s0, shift, ls, w, l, *, Bb, Tt, C, t_real):
    B, T, D = x.shape
    assert Tt & (Tt - 1) == 0 and C & (C - 1) == 0 and B % Bb == 0 and T % Tt == 0 and Tt % C == 0
    kern = functools.partial(_rwkv_kernel, Bb=Bb, Tt=Tt, C=C, t_real=t_real, t_total=T)
    names = ['norm_mix_g', 'rwkv_w', 'rwkv_mu', 'rwkv_w0', 'rwkv_w2', 'rwkv_a0', 'rwkv_a2', 'rwkv_g2',
             'rwkv_k_k', 'rwkv_k_a', 'rwkv_r_k', 'rwkv_ln_g', 'rwkv_ln_b', 'rwkv_e']
    return pl.pallas_call(
        kern,
        grid=(B // Bb, T // Tt),
        in_specs=[
            pl.BlockSpec((Bb, Tt, D), lambda b, t: (b, t, 0)),
            pl.BlockSpec((None, Bb, RWKV_HEADS, RWKV_HEAD, RWKV_HEAD), lambda b, t: (ls, b, 0, 0, 0)),
            pl.BlockSpec((None, Bb, 1, RW_PAD), lambda b, t: (ls, b, 0, 0)),
        ] + [_wspec(w[n], l) for n in names],
        out_specs=[
            pl.BlockSpec((Bb, Tt, RWKV_W), lambda b, t: (b, t, 0)),
            pl.BlockSpec((Bb, RWKV_HEADS, RWKV_HEAD, RWKV_HEAD), lambda b, t: (b, 0, 0, 0)),
            pl.BlockSpec((Bb, 1, RW_PAD), lambda b, t: (b, 0, 0)),
        ],
        out_shape=[
            jax.ShapeDtypeStruct((B, T, RWKV_W), F32),
            jax.ShapeDtypeStruct((B, RWKV_HEADS, RWKV_HEAD, RWKV_HEAD), F32),
            jax.ShapeDtypeStruct((B, 1, RW_PAD), F32),
        ],
        scratch_shapes=[pltpu.VMEM((Bb, SUBLANES, RW_PAD), F32), pltpu.VMEM((Bb * Tt, RWKV_W), F32)],
        compiler_params=pltpu.CompilerParams(
            dimension_semantics=("arbitrary", "arbitrary"), vmem_limit_bytes=VMEM_LIMIT),
        name="rwkv_mixer",
    )(x, s0, shift, *[w[n] for n in names])


def _mla_proj_kernel(x_ref, ng_ref, wm_ref, qg_ref, kvg_ref, wq_ref, cos_ref, sin_ref, *rest, prompt):
    if prompt:
        wuk_ref, wuv_ref, ckv_ref, kr_ref, q_ref, k_ref, v_ref = rest
    else:
        ckv_ref, kr_ref, qn_ref, qr_ref = rest
    h = _rms(x_ref[...], ng_ref[...]).astype(BF16)
    p = jnp.dot(h, wm_ref[...], preferred_element_type=F32)
    cqn = _rms(p[:, :MLA_Q_RANK], qg_ref[...]).astype(BF16)
    ckv = _rms(p[:, MLA_Q_RANK:MLA_Q_RANK + MLA_KV_RANK], kvg_ref[...])
    o = MLA_Q_RANK + MLA_KV_RANK
    cos = cos_ref[...]
    sin = sin_ref[...]
    krp = p[:, o:o + LANES] * cos + p[:, o + LANES:o + 2 * LANES] * sin
    ckv_ref[...] = ckv
    kr_ref[...] = krp[:, :MLA_ROPE]
    q = jnp.dot(cqn, wq_ref[...], preferred_element_type=F32)
    wn = MLA_HEADS * MLA_NOPE
    wr = MLA_HEADS * LANES
    cos4 = jnp.concatenate([cos] * MLA_HEADS, axis=1)
    sin4 = jnp.concatenate([sin] * MLA_HEADS, axis=1)
    qn = q[:, :wn] * MLA_SCALE
    qr = (q[:, wn:wn + wr] * cos4 + q[:, wn + wr:wn + 2 * wr] * sin4) * MLA_SCALE
    if prompt:
        cb = ckv.astype(BF16)
        kn = jnp.dot(cb, wuk_ref[...], preferred_element_type=F32)
        qparts, kparts = [], []
        for hh in range(MLA_HEADS):
            qparts += [qn[:, hh * MLA_NOPE:(hh + 1) * MLA_NOPE], qr[:, hh * LANES:(hh + 1) * LANES]]
            kparts += [kn[:, hh * MLA_NOPE:(hh + 1) * MLA_NOPE], krp]
        q_ref[...] = jnp.concatenate(qparts, axis=1).astype(BF16)
        k_ref[...] = jnp.concatenate(kparts, axis=1).astype(BF16)
        v_ref[...] = lax.dot_general(wuv_ref[...], cb, (((1,), (1,)), ((), ())),
                                     preferred_element_type=F32).astype(BF16)
    else:
        qn_ref[...] = qn
        qr_ref[...] = qr


def _mla_proj_call(x2, w, l, cos, sin, *, tr, prompt):
    M, D = x2.shape
    nt = cos.shape[0] // tr
    full = lambda a: _wspec(a, l)
    tab = pl.BlockSpec((tr, LANES), lambda i: (i % nt, 0))
    tok = lambda n: pl.BlockSpec((tr, n), lambda i: (i, 0))
    ins = [x2, w['norm_mix_g'], w['mla_w'], w['mla_q_norm_g'], w['mla_kv_norm_g'], w['mla_wq'], cos, sin]
    in_specs = [tok(D)] + [full(a) for a in ins[1:6]] + [tab, tab]
    out_specs = [tok(MLA_KV_RANK), tok(MLA_ROPE)]
    out_shape = [jax.ShapeDtypeStruct((M, MLA_KV_RANK), F32), jax.ShapeDtypeStruct((M, MLA_ROPE), F32)]
    if prompt:
        T = cos.shape[0]
        ins += [w['mla_wuk'], w['mla_wuv_t']]
        in_specs += [full(w['mla_wuk']), full(w['mla_wuv_t'])]
        out_specs += [tok(MLA_HEADS * QK_HEAD), tok(MLA_HEADS * QK_HEAD),
                      pl.BlockSpec((None, MLA_HEADS * MLA_V, tr), lambda i: (i // nt, 0, i % nt))]
        out_shape += [jax.ShapeDtypeStruct((M, MLA_HEADS * QK_HEAD), BF16),
                      jax.ShapeDtypeStruct((M, MLA_HEADS * QK_HEAD), BF16),
                      jax.ShapeDtypeStruct((M // T, MLA_HEADS * MLA_V, T), BF16)]
    else:
        out_specs += [tok(MLA_HEADS * MLA_NOPE), tok(MLA_HEADS * LANES)]
        out_shape += [jax.ShapeDtypeStruct((M, MLA_HEADS * MLA_NOPE), F32),
                      jax.ShapeDtypeStruct((M, MLA_HEADS * LANES), F32)]
    return pl.pallas_call(
        functools.partial(_mla_proj_kernel, prompt=prompt),
        grid=(M // tr,),
        in_specs=in_specs, out_specs=out_specs, out_shape=out_shape,
        compiler_params=pltpu.CompilerParams(dimension_semantics=("arbitrary",), vmem_limit_bytes=VMEM_LIMIT),
        name="mla_proj_prompt" if prompt else "mla_proj_sample",
    )(*ins)


def _flash_kernel(q_ref, k_ref, vt_ref, o_ref, m_scr, l_scr, acc_scr, *, tq):
    qi = pl.program_id(1)
    ki = pl.program_id(2)

    @pl.when(ki == 0)
    def _():
        m_scr[...] = jnp.full(m_scr.shape, -jnp.inf, F32)
        l_scr[...] = jnp.zeros(l_scr.shape, F32)
        acc_scr[...] = jnp.zeros(acc_scr.shape, F32)

    def step(diagonal):
        if diagonal:
            kpos = lax.broadcasted_iota(jnp.int32, (tq, tq), 0)
            qpos = lax.broadcasted_iota(jnp.int32, (tq, tq), 1)
            keep = kpos <= qpos
        for hh in range(MLA_HEADS):
            lq = slice(hh * QK_HEAD, (hh + 1) * QK_HEAD)
            rv = slice(hh * MLA_V, (hh + 1) * MLA_V)
            st = lax.dot_general(k_ref[0, :, lq], q_ref[0, :, lq], (((1,), (1,)), ((), ())),
                                 preferred_element_type=F32)
            if diagonal:
                st = jnp.where(keep, st, -jnp.inf)
            m_old = m_scr[hh:hh + 1, :]
            m_new = jnp.maximum(m_old, jnp.max(st, axis=0, keepdims=True))
            alpha = jnp.exp(m_old - m_new)
            p = jnp.exp(st - m_new)
            l_scr[hh:hh + 1, :] = alpha * l_scr[hh:hh + 1, :] + jnp.sum(p, axis=0, keepdims=True)
            acc_scr[rv, :] = alpha * acc_scr[rv, :] + jnp.dot(vt_ref[0, rv, :], p.astype(BF16),
                                                              preferred_element_type=F32)
            m_scr[hh:hh + 1, :] = m_new

    @pl.when(ki < qi)
    def _():
        step(False)

    @pl.when(ki == qi)
    def _():
        step(True)
        for hh in range(MLA_HEADS):
            rv = slice(hh * MLA_V, (hh + 1) * MLA_V)
            o_ref[0, rv, :] = acc_scr[rv, :] / l_scr[hh:hh + 1, :]


def _flash_call(q, k, vt, *, tq):
    B, T, _ = q.shape
    nq = T // tq
    hv = vt.shape[1]
    return pl.pallas_call(
        functools.partial(_flash_kernel, tq=tq),
        grid=(B, nq, nq),
        in_specs=[pl.BlockSpec((1, tq, q.shape[2]), lambda b, i, j: (b, i, 0)),
                  pl.BlockSpec((1, tq, k.shape[2]), lambda b, i, j: (b, jnp.minimum(i, j), 0)),
                  pl.BlockSpec((1, hv, tq), lambda b, i, j: (b, 0, jnp.minimum(i, j)))],
        out_specs=pl.BlockSpec((1, hv, tq), lambda b, i, j: (b, 0, i)),
        out_shape=jax.ShapeDtypeStruct((B, hv, T), F32),
        scratch_shapes=[pltpu.VMEM((SUBLANES, tq), F32), pltpu.VMEM((SUBLANES, tq), F32),
                        pltpu.VMEM((hv, tq), F32)],
        compiler_params=pltpu.CompilerParams(
            dimension_semantics=("arbitrary", "arbitrary", "arbitrary"), vmem_limit_bytes=VMEM_LIMIT),
        name="mla_flash",
    )(q, k, vt)


def _decode_kernel(pt_ref, qn_ref, qr_ref, ckvn_ref, krn_ref, wuk_ref, wuv_ref, ckv_hbm, krt_hbm, o_ref,
                   bufc, bufk, sem, *, G, NJ, BB, layer, nsteps):
    s = pl.program_id(0)
    total = nsteps * NJ
    ahead = DECODE_SLOTS - 1
    nj_shift = NJ.bit_length() - 1
    page = bufc.shape[3]
    rid = lax.broadcasted_iota(jnp.int32, (SUBLANES, 1), 0)

    def page_copies(t, bi, i, slot):
        pg = pt_ref[(t >> nj_shift) * BB + bi, (t & (NJ - 1)) * G + i]
        return (pltpu.make_async_copy(ckv_hbm.at[layer, pg], bufc.at[bi, slot, i], sem.at[0, slot]),
                pltpu.make_async_copy(krt_hbm.at[layer, pg], bufk.at[bi, slot, i], sem.at[1, slot]))

    @pl.when(s == 0)
    def _():
        for t0 in range(ahead):
            for bi in range(BB):
                for i in range(G):
                    for c in page_copies(t0, bi, i, t0):
                        c.start(priority=i % 2)

    qlats, qrms = [], []
    for bi in range(BB):
        qn = qn_ref[bi]
        qr = qr_ref[bi]
        qlat = jnp.zeros((SUBLANES, MLA_KV_RANK), F32)
        qrm = jnp.zeros((SUBLANES, MLA_ROPE), F32)
        for hh in range(MLA_HEADS):
            ql = _bdot_nt(qn[:, hh * MLA_NOPE:(hh + 1) * MLA_NOPE], wuk_ref[hh])
            qlat = jnp.where(rid == hh, ql[0:1, :], qlat)
            qrm = jnp.where(rid == hh, qr[0:1, hh * LANES:hh * LANES + MLA_ROPE], qrm)
        qlats.append(qlat)
        qrms.append(qrm)

    def body(j, carry):
        t = s * NJ + j
        slot = lax.rem(t, DECODE_SLOTS)
        t_next = jnp.minimum(t + ahead, total - 1)
        slot_next = lax.rem(t + ahead, DECODE_SLOTS)
        for bi in range(BB):
            for i in range(G):
                for c in page_copies(t, bi, i, slot):
                    c.wait()
        scores = [[] for _ in range(BB)]
        for i in range(G):
            for bi in range(BB):
                sc = (_bdot_nt(qlats[bi], bufc[bi, slot, i]) + _bdot(qrms[bi], bufk[bi, slot, i]))
                scores[bi].append(sc)
                for c in page_copies(t_next, bi, i, slot_next):
                    c.start(priority=i % 2)
        out = []
        for bi in range(BB):
            m_old, l_old, acc_old = carry[bi]
            sc = jnp.concatenate(scores[bi], axis=1)
            m_new = jnp.maximum(m_old, jnp.max(sc, axis=-1, keepdims=True))
            alpha = jnp.exp(m_old - m_new)
            p = jnp.exp(sc - m_new)
            l_new = alpha * l_old + jnp.sum(p, axis=-1, keepdims=True)
            acc = alpha * acc_old
            for i in range(G):
                acc = acc + _bdot(p[:, i * page:(i + 1) * page], bufc[bi, slot, i])
            out.append((m_new, l_new, acc))
        return tuple(out)

    init = tuple((jnp.full((SUBLANES, 1), -jnp.inf, F32), jnp.zeros((SUBLANES, 1), F32),
                  jnp.zeros((SUBLANES, MLA_KV_RANK), F32)) for _ in range(BB))
    final = lax.fori_loop(0, NJ, body, init)

    @pl.when(s == nsteps - 1)
    def _():
        for extra in range(ahead):
            for bi in range(BB):
                for i in range(G):
                    for c in page_copies(total - 1, bi, i, (total + extra) % DECODE_SLOTS):
                        c.wait()

    for bi in range(BB):
        m_new, l_new, acc = final[bi]
        ckvn = ckvn_ref[bi][0:1, :]
        krn = krn_ref[bi][0:1, :]
        s_new = (jnp.sum(qlats[bi] * ckvn, axis=-1, keepdims=True)
                 + jnp.sum(qrms[bi] * krn, axis=-1, keepdims=True))
        m_fin = jnp.maximum(m_new, s_new)
        a2 = jnp.exp(m_new - m_fin)
        p_new = jnp.exp(s_new - m_fin)
        l_fin = a2 * l_new + p_new
        o_lat = (a2 * acc + p_new * ckvn) / l_fin
        outs = []
        for hh in range(MLA_HEADS):
            oh = _bdot(o_lat, wuv_ref[hh])
            outs.append(oh[hh:hh + 1, :])
        o_ref[bi] = jnp.broadcast_to(jnp.concatenate(outs, axis=1), (SUBLANES, MLA_HEADS * MLA_V))


def _decode_call(page_table, qn, qr, ckvn, krn, cache_ckv, cache_krope_t, layer, w):
    B = qn.shape[0]
    n_pages = page_table.shape[1]
    page = cache_ckv.shape[2]
    G = _pick_tile(n_pages // 2, 16)
    NJ = n_pages // G
    BB = _pick_tile(B, 2)
    assert NJ >= 2 and NJ & (NJ - 1) == 0
    nsteps = B // BB
    tok = lambda a: pl.BlockSpec((BB,) + a.shape[1:], lambda b, pt: (b, 0, 0))
    full = lambda a: _wspec(a, layer)
    hbm = pl.BlockSpec(memory_space=pl.ANY)
    grid_spec = pltpu.PrefetchScalarGridSpec(
        num_scalar_prefetch=1,
        grid=(nsteps,),
        in_specs=[tok(qn), tok(qr), tok(ckvn), tok(krn), full(w['mla_wuk_h']), full(w['mla_wuv_h']), hbm, hbm],
        out_specs=pl.BlockSpec((BB, SUBLANES, MLA_HEADS * MLA_V), lambda b, pt: (b, 0, 0)),
        scratch_shapes=[pltpu.VMEM((BB, DECODE_SLOTS, G, page, MLA_KV_RANK), F32),
                        pltpu.VMEM((BB, DECODE_SLOTS, G, MLA_ROPE, page), F32),
                        pltpu.SemaphoreType.DMA((2, DECODE_SLOTS))],
    )
    return pl.pallas_call(
        functools.partial(_decode_kernel, G=G, NJ=NJ, BB=BB, layer=layer, nsteps=nsteps),
        grid_spec=grid_spec,
        out_shape=jax.ShapeDtypeStruct((B, SUBLANES, MLA_HEADS * MLA_V), F32),
        compiler_params=pltpu.CompilerParams(
            dimension_semantics=("arbitrary",), vmem_limit_bytes=VMEM_LIMIT),
        name="mla_decode",
    )(page_table, qn, qr, ckvn, krn, w['mla_wuk_h'], w['mla_wuv_h'], cache_ckv, cache_krope_t)


def _merge_ffn_kernel(x_ref, oa_ref, ob_ref, oc_ref, ng_ref, wgt_ref, bg_ref, wa_ref, wb_ref, wc_ref, wo_ref,
                      nf_ref, wup_ref, wdn_ref, fin_ref, y_ref, *, final, oc_transposed):
    x = x_ref[...]
    h = _rms(x, ng_ref[...]).astype(BF16)
    gates = _sigmoid(jnp.dot(h, wgt_ref[...], preferred_element_type=F32) + bg_ref[...])
    D = D_MODEL
    oc = oc_ref[...].T if oc_transposed else oc_ref[...]
    merged = (gates[:, 0:D] * _bdot(oa_ref[...], wa_ref[...])
              + gates[:, D:2 * D] * _bdot(ob_ref[...], wb_ref[...])
              + gates[:, 2 * D:3 * D] * _bdot(oc, wc_ref[...]))
    x1 = x + _bdot(merged, wo_ref[...])
    h2 = _rms(x1, nf_ref[...]).astype(BF16)
    up = jnp.dot(h2, wup_ref[...], preferred_element_type=F32)
    g = up[:, :D_FF]
    x2 = x1 + _bdot(g * _sigmoid(g) * up[:, D_FF:], wdn_ref[...])
    y_ref[...] = _rms(x2, fin_ref[...]) if final else x2


def _merge_ffn_call(x, oa, ob, oc, w, l, fin_g, *, tm, final):
    M, D = x.shape
    const = lambda a: _wspec(a, l, pipeline_mode=pl.Buffered(1))
    row = lambda n: pl.BlockSpec((tm, n), lambda i: (i, 0))
    names = ['norm_mix_g', 'w_gate', 'b_gate', 'w_br_a', 'w_br_b', 'w_br_c', 'w_out', 'norm_ffn_g',
             'w_ffn_up', 'w_ffn_down']
    oc_transposed = oc.ndim == 3
    if oc_transposed:
        nt = oc.shape[2] // tm
        oc_spec = pl.BlockSpec((None, oc.shape[1], tm), lambda i: (i // nt, 0, i % nt))
    else:
        oc_spec = row(oc.shape[1])
    return pl.pallas_call(
        functools.partial(_merge_ffn_kernel, final=final, oc_transposed=oc_transposed),
        grid=(M // tm,),
        in_specs=[row(D), row(oa.shape[1]), row(ob.shape[1]), oc_spec]
                 + [const(w[n]) for n in names] + [const(fin_g)],
        out_specs=row(D),
        out_shape=jax.ShapeDtypeStruct((M, D), F32),
        compiler_params=pltpu.CompilerParams(
            dimension_semantics=("arbitrary",), vmem_limit_bytes=VMEM_LIMIT),
        name="merge_ffn",
    )(x, oa, ob, oc, *[w[n] for n in names], fin_g)


def _pad_cols(a, n):
    return jnp.pad(a, [(0, 0)] * (a.ndim - 1) + [(0, n - a.shape[-1])])


def _swap_halves(a, width):
    shp = a.shape
    a = a.reshape(shp[:-1] + (shp[-1] // width, 2, width // 2))
    return a[..., ::-1, :].reshape(shp)


def _pad_groups(a, width, to):
    shp = a.shape
    a = a.reshape(shp[:-1] + (shp[-1] // width, width))
    a = jnp.pad(a, [(0, 0)] * (a.ndim - 1) + [(0, to - width)])
    return a.reshape(shp[:-1] + (-1,))


def _rw_pad(a):
    W = RWKV_W
    z = jnp.zeros(a.shape[:-1] + (LANES - RWKV_DECAY_LORA,), a.dtype)
    return jnp.concatenate([a[..., :3 * W], a[..., 3 * W:3 * W + RWKV_DECAY_LORA], z,
                            a[..., 3 * W + RWKV_DECAY_LORA:3 * W + RWKV_DECAY_LORA + RWKV_AAA_LORA], z,
                            a[..., 3 * W + RWKV_DECAY_LORA + RWKV_AAA_LORA:]], axis=-1)


def _rw_unpad(a):
    W = RWKV_W
    return jnp.concatenate([a[..., :3 * W], a[..., 3 * W:3 * W + RWKV_DECAY_LORA],
                            a[..., 3 * W + LANES:3 * W + LANES + RWKV_AAA_LORA],
                            a[..., 3 * W + 2 * LANES:]], axis=-1)


def _prep_weights(p):
    w_in = p['w_in']
    depth = w_in.shape[0]
    row = lambda a: a.reshape(depth, 1, -1).astype(F32)
    w = {}
    w['norm_mix_g'] = row(p['norm_mix_g'])
    w['norm_ffn_g'] = row(p['norm_ffn_g'])
    ba = w_in[..., _O_BETA:_O_Z]
    w['gdn_w'] = jnp.concatenate([w_in[..., _O_QKV:_O_BETA], w_in[..., _O_Z:_O_RW], _pad_cols(ba, LANES)],
                                 axis=-1).astype(BF16)
    w['gdn_wba_t'] = jnp.swapaxes(ba, 1, 2).astype(BF16)
    w['gdn_conv_w'] = p['gdn_conv_w']
    zero4 = jnp.zeros((depth, GDN_HEADS), F32)
    a_log = jnp.concatenate([zero4, p['gdn_a_log']], axis=1)
    dt_b = jnp.concatenate([zero4, p['gdn_dt_bias']], axis=1)
    w['gdn_prow'] = _pad_cols(jnp.stack([a_log, dt_b], axis=1), LANES)
    w['gdn_pcol'] = jnp.stack([a_log, dt_b], axis=2)
    w['gdn_norm_g'] = row(p['gdn_norm_g'])
    lane = jnp.arange(LANES)[:, None]
    head_of_col = jnp.arange(GDN_VAL_W)[None, :] // GDN_DV
    w['gdn_rep'] = jnp.concatenate([lane == head_of_col, lane == head_of_col + GDN_HEADS], axis=1).astype(BF16)
    w['rwkv_w'] = _rw_pad(w_in[..., _O_RW:_O_CQ]).astype(BF16)
    w['rwkv_mu'] = row(_rw_pad(p['rwkv_mu']))
    w['rwkv_w0'] = row(p['rwkv_w0'])
    w['rwkv_w2'] = jnp.pad(p['rwkv_w2'], ((0, 0), (0, LANES - RWKV_DECAY_LORA), (0, 0))).astype(BF16)
    w['rwkv_a0'] = row(p['rwkv_a0'])
    w['rwkv_a2'] = jnp.pad(p['rwkv_a2'], ((0, 0), (0, LANES - RWKV_AAA_LORA), (0, 0))).astype(BF16)
    w['rwkv_g2'] = p['rwkv_g2'].astype(BF16)
    w['rwkv_k_k'] = row(p['rwkv_k_k'])
    w['rwkv_k_a'] = row(p['rwkv_k_a'])
    w['rwkv_r_k'] = row(p['rwkv_r_k'])
    w['rwkv_ln_g'] = row(p['rwkv_ln_g'])
    w['rwkv_ln_b'] = row(p['rwkv_ln_b'])
    hid = jnp.arange(RWKV_W) // RWKV_HEAD
    w['rwkv_e'] = (hid[:, None] == hid[None, :]).astype(BF16)
    w_kr = w_in[..., _O_KR:_O_GATE]
    w['mla_w'] = jnp.concatenate([w_in[..., _O_CQ:_O_KR], _pad_cols(w_kr, LANES),
                                  _pad_cols(_swap_halves(w_kr, MLA_ROPE), LANES)], axis=-1).astype(BF16)
    w['mla_q_norm_g'] = row(p['mla_q_norm_g'])
    w['mla_kv_norm_g'] = row(p['mla_kv_norm_g'])
    w_uq = p['mla_w_uq']
    wq_n = w_uq[..., :MLA_NOPE].reshape(depth, MLA_Q_RANK, -1)
    wq_r = w_uq[..., MLA_NOPE:].reshape(depth, MLA_Q_RANK, -1)
    w['mla_wq'] = jnp.concatenate([wq_n, _pad_groups(wq_r, MLA_ROPE, LANES),
                                   _pad_groups(_swap_halves(wq_r, MLA_ROPE), MLA_ROPE, LANES)],
                                  axis=-1).astype(BF16)
    w['mla_wuk'] = p['mla_w_uk'].reshape(depth, MLA_KV_RANK, -1).astype(BF16)
    w['mla_wuv_t'] = jnp.swapaxes(p['mla_w_uv'].reshape(depth, MLA_KV_RANK, -1), 1, 2).astype(BF16)
    w['mla_wuk_h'] = jnp.transpose(p['mla_w_uk'], (0, 2, 1, 3)).astype(BF16)
    w['mla_wuv_h'] = jnp.transpose(p['mla_w_uv'], (0, 2, 1, 3)).astype(BF16)
    w['w_gate'] = w_in[..., _O_GATE:].astype(BF16)
    w['b_gate'] = row(p['b_gate'])
    for n in ('w_br_a', 'w_br_b', 'w_br_c', 'w_out', 'w_ffn_up', 'w_ffn_down'):
        w[n] = p[n].astype(BF16)
    return w


def _rope_tables(pos):
    half = MLA_ROPE // 2
    freq = ROPE_THETA ** (-jnp.arange(half, dtype=F32) / half)
    ang = pos.astype(F32)[:, None] * freq
    cos = jnp.cos(ang)
    sin = jnp.sin(ang)
    return (_pad_cols(jnp.concatenate([cos, cos], axis=1), LANES),
            _pad_cols(jnp.concatenate([-sin, sin], axis=1), LANES))


def _pick_tile(T, pref):
    t = min(T, pref)
    while T % t:
        t //= 2
    return t


def kernel(x_prompt, x_sample, cache_ckv, cache_krope, page_table, state_gdn, state_gdn_conv, state_rwkv, state_rwkv_shift, norm_mix_g, norm_ffn_g, norm_final_g, w_in, b_gate, gdn_conv_w, gdn_a_log, gdn_dt_bias, gdn_norm_g, rwkv_mu, rwkv_w0, rwkv_w2, rwkv_a0, rwkv_a2, rwkv_g2, rwkv_k_k, rwkv_k_a, rwkv_r_k, rwkv_ln_g, rwkv_ln_b, mla_q_norm_g, mla_kv_norm_g, mla_w_uq, mla_w_uk, mla_w_uv, w_br_a, w_br_b, w_br_c, w_out, w_ffn_up, w_ffn_down):
    p = dict(norm_mix_g=norm_mix_g, norm_ffn_g=norm_ffn_g, w_in=w_in, b_gate=b_gate, gdn_conv_w=gdn_conv_w,
             gdn_a_log=gdn_a_log, gdn_dt_bias=gdn_dt_bias, gdn_norm_g=gdn_norm_g, rwkv_mu=rwkv_mu,
             rwkv_w0=rwkv_w0, rwkv_w2=rwkv_w2, rwkv_a0=rwkv_a0, rwkv_a2=rwkv_a2, rwkv_g2=rwkv_g2,
             rwkv_k_k=rwkv_k_k, rwkv_k_a=rwkv_k_a, rwkv_r_k=rwkv_r_k, rwkv_ln_g=rwkv_ln_g,
             rwkv_ln_b=rwkv_ln_b, mla_q_norm_g=mla_q_norm_g, mla_kv_norm_g=mla_kv_norm_g,
             mla_w_uq=mla_w_uq, mla_w_uk=mla_w_uk, mla_w_uv=mla_w_uv, w_br_a=w_br_a, w_br_b=w_br_b,
             w_br_c=w_br_c, w_out=w_out, w_ffn_up=w_ffn_up, w_ffn_down=w_ffn_down)
    depth = w_in.shape[0]
    bp, sp, D = x_prompt.shape
    bs, ss, _ = x_sample.shape
    assert ss == 1, "the sample group decodes one new token per sequence"
    past_len = page_table.shape[1] * cache_ckv.shape[2]
    fin_g = norm_final_g.reshape(1, -1).astype(F32)
    cache_krope_t = jnp.swapaxes(cache_krope, 2, 3)

    tp = _pick_tile(sp, 256)
    cp = _pick_tile(tp, 64)
    tq = _pick_tile(sp, 512)
    ts = SUBLANES
    bb_s = _pick_tile(bs, 8)
    cos_p, sin_p = _rope_tables(jnp.arange(sp, dtype=jnp.int32))
    cos_s, sin_s = _rope_tables(jnp.full((bs * ts,), past_len, dtype=jnp.int32))
    zeros_p = dict(
        gdn=jnp.zeros((1, bp, GDN_HEADS, GDN_DK, GDN_DV), F32),
        conv=jnp.zeros((1, bp, GDN_CONV - 1, GDN_CONV_CH), F32),
        rwkv=jnp.zeros((1, bp, RWKV_HEADS, RWKV_HEAD, RWKV_HEAD), F32),
        shift=jnp.zeros((1, bp, 1, RW_PAD), F32))
    shift_s = _rw_pad(state_rwkv_shift)[:, :, None]

    w = _prep_weights(p)
    xp = x_prompt
    xs = jnp.pad(x_sample, ((0, 0), (0, ts - ss), (0, 0)))
    new_p = [[] for _ in range(6)]
    new_s = [[] for _ in range(6)]
    mp = bp * sp
    tm_p = _pick_tile(sp, 512)
    for l in range(depth):
        final = l == depth - 1
        oa, s_g, cbuf = _gdn_call(xp, zeros_p['gdn'], zeros_p['conv'], 0, w, l, Bb=1, Tt=tp, C=cp, t_real=sp)
        ob, s_r, sh = _rwkv_call(xp, zeros_p['rwkv'], zeros_p['shift'], 0, w, l, Bb=1, Tt=tp, C=cp, t_real=sp)
        ckv, kr, q, k, vt = _mla_proj_call(xp.reshape(mp, D), w, l, cos_p, sin_p, tr=tq, prompt=True)
        oc_t = _flash_call(q.reshape(bp, sp, -1), k.reshape(bp, sp, -1), vt, tq=tq)
        for lst, arr in zip(new_p, (ckv.reshape(bp, sp, -1), kr.reshape(bp, sp, -1), s_g, cbuf, s_r, sh[:, 0])):
            lst.append(arr)
        xp = _merge_ffn_call(xp.reshape(mp, D), oa.reshape(mp, -1), ob.reshape(mp, -1),
                             oc_t, w, l, fin_g, tm=tm_p, final=final).reshape(bp, sp, D)
        oa, s_g, cbuf = _gdn_call(xs, state_gdn, state_gdn_conv, l, w, l, Bb=bb_s, Tt=ts, C=ts, t_real=ss)
        ob, s_r, sh = _rwkv_call(xs, state_rwkv, shift_s, l, w, l, Bb=bb_s, Tt=ts, C=ts, t_real=ss)
        ckv, kr, qn, qr = _mla_proj_call(xs.reshape(bs * ts, D), w, l, cos_s, sin_s, tr=bs * ts, prompt=False)
        ckv = ckv.reshape(bs, ts, -1)
        kr = kr.reshape(bs, ts, -1)
        oc = _decode_call(page_table, qn.reshape(bs, ts, -1), qr.reshape(bs, ts, -1), ckv, kr,
                          cache_ckv, cache_krope_t, l, w)
        for lst, arr in zip(new_s, (ckv[:, :ss], kr[:, :ss], s_g, cbuf, s_r, sh[:, 0])):
            lst.append(arr)
        xs_real = _merge_ffn_call(xs[:, 0], oa[:, 0], ob[:, 0], oc[:, 0], w, l, fin_g, tm=bs, final=final)
        xs = jnp.pad(xs_real[:, None], ((0, 0), (0, ts - ss), (0, 0)))
    y_prompt = xp
    y_sample = xs[:, :ss]
    outs_p = [jnp.stack(a) for a in new_p]
    outs_s = [jnp.stack(a) for a in new_s]
    outs_p[5] = _rw_unpad(outs_p[5])
    outs_s[5] = _rw_unpad(outs_s[5])
    return (y_prompt, y_sample, *outs_p, *outs_s)
```

```python
import functools
import math

import jax
import jax.numpy as jnp
from jax import lax
from jax.experimental import pallas as pl
from jax.experimental.pallas import tpu as pltpu

F32 = jnp.float32
BF16 = jnp.bfloat16

D_MODEL = 1024
GDN_HEADS = 4
GDN_DK = 128
GDN_DV = 128
GDN_CONV = 4
GDN_KEY_W = GDN_HEADS * GDN_DK
GDN_VAL_W = GDN_HEADS * GDN_DV
GDN_CONV_CH = 2 * GDN_KEY_W + GDN_VAL_W
RWKV_HEADS = 8
RWKV_HEAD = 64
RWKV_W = RWKV_HEADS * RWKV_HEAD
RWKV_DECAY_LORA = 64
RWKV_AAA_LORA = 64
RWKV_GATE_LORA = 128
RWKV_COLS = 3 * RWKV_W + RWKV_DECAY_LORA + RWKV_AAA_LORA + RWKV_GATE_LORA
RWKV_LN_EPS = 64e-5
MLA_HEADS = 4
MLA_Q_RANK = 256
MLA_KV_RANK = 256
MLA_NOPE = 128
MLA_ROPE = 64
MLA_V = 128
MLA_SCALE = 1.0 / math.sqrt(MLA_NOPE + MLA_ROPE)
ROPE_THETA = 10000.0
D_FF = ((8 * D_MODEL // 3 + 255) // 256) * 256
NORM_EPS = 1e-6
L2_EPS = 1e-6

LANES = 128
SUBLANES = 8
VMEM_LIMIT = 56 * 1024 * 1024

_O_QKV = 0
_O_BETA = _O_QKV + GDN_CONV_CH
_O_ALPHA = _O_BETA + GDN_HEADS
_O_Z = _O_ALPHA + GDN_HEADS
_O_RW = _O_Z + GDN_VAL_W
_O_CQ = _O_RW + RWKV_COLS
_O_CKV = _O_CQ + MLA_Q_RANK
_O_KR = _O_CKV + MLA_KV_RANK
_O_GATE = _O_KR + MLA_ROPE

RW_PAD = 3 * RWKV_W + 3 * LANES
PRE_ROW0 = SUBLANES
QK_HEAD = MLA_NOPE + LANES
DECODE_SLOTS = 3


def _bdot(a, b):
    return jnp.dot(a.astype(BF16), b.astype(BF16), preferred_element_type=F32)


def _bdot_nt(a, b):
    return lax.dot_general(a.astype(BF16), b.astype(BF16), (((1,), (1,)), ((), ())),
                           preferred_element_type=F32)


def _hdot(a, b):
    return jnp.dot(a, b, precision=lax.Precision.HIGHEST, preferred_element_type=F32)


def _split_dot(a, b_exact, parts):
    acc = None
    rem = a
    for _ in range(parts):
        hi = rem.astype(BF16)
        t = jnp.dot(hi, b_exact, preferred_element_type=F32)
        acc = t if acc is None else acc + t
        rem = rem - hi.astype(F32)
    return acc


def _split_dot_rhs(a_exact, b, parts):
    acc = None
    rem = b
    for _ in range(parts):
        hi = rem.astype(BF16)
        t = jnp.dot(a_exact, hi, preferred_element_type=F32)
        acc = t if acc is None else acc + t
        rem = rem - hi.astype(F32)
    return acc


def _rms(x, g, eps=NORM_EPS):
    return x * lax.rsqrt(jnp.mean(x * x, axis=-1, keepdims=True) + eps) * g


def _sigmoid(x):
    return 0.5 * jnp.tanh(0.5 * x) + 0.5


def _softplus(x):
    return jnp.maximum(x, 0.0) + jnp.log(1.0 + jnp.exp(-jnp.abs(x)))


def _tri_masks(C):
    r = lax.broadcasted_iota(jnp.int32, (C, C), 0)
    c = lax.broadcasted_iota(jnp.int32, (C, C), 1)
    masks = []
    k = 0
    while (1 << k) < C:
        rr = r >> k
        cc = c >> k
        m = jnp.where((rr ^ cc) == 1, jnp.where((rr & 1) == 1, 1.0, 0.0), 0.0)
        masks.append(m.astype(F32))
        k += 1
    return r, c, masks


def _unit_lower_inverses(Ls, masks, eye):
    Ts = [eye - L * masks[0] for L in Ls]
    for m in masks[1:]:
        tmp = [_bdot(L * m, T) for L, T in zip(Ls, Ts)]
        Ts = [T - _bdot(T, t) for T, t in zip(Ts, tmp)]
    return Ts


def _chunk_tri(R, C):
    rt = lax.broadcasted_iota(jnp.int32, (R, R), 0)
    ct = lax.broadcasted_iota(jnp.int32, (R, R), 1)
    sh = C.bit_length() - 1
    same = (rt >> sh) == (ct >> sh)
    tri = jnp.where(same, jnp.where(rt >= ct, 1.0, 0.0), 0.0).astype(BF16)
    tri_t = jnp.where(same, jnp.where(ct >= rt, 1.0, 0.0), 0.0).astype(BF16)
    return tri, tri_t


def _gdn_kernel(x_ref, s0_ref, hist_ref, ng_ref, wg_ref, wbat_ref, cw_ref, prow_ref, pcol_ref, gng_ref, rep_ref,
                o_ref, s_ref, cb_ref, pre_scr, *, Bb, Tt, C, t_real, t_total):
    ti = pl.program_id(1)
    R = Bb * Tt

    @pl.when(ti == 0)
    def _():
        s_ref[...] = s0_ref[...]
        pre_scr[:, PRE_ROW0 - 3:PRE_ROW0, :] = hist_ref[...]

    h = _rms(x_ref[...].reshape(R, D_MODEL), ng_ref[...]).astype(BF16)
    proj = jnp.dot(h, wg_ref[...], preferred_element_type=F32)
    ba_row = lax.dot_general(wbat_ref[...], h, (((1,), (1,)), ((), ())),
                             preferred_element_type=F32)

    cw = cw_ref[...]
    pre_all = proj[:, :GDN_CONV_CH]
    real_rows = min(Tt, t_real)
    heads = []
    for bb in range(Bb):
        pre_scr[bb, PRE_ROW0:2 * PRE_ROW0, :] = pre_all[bb * Tt:bb * Tt + PRE_ROW0]
        yh = pre_scr[bb, PRE_ROW0 - 3:2 * PRE_ROW0 - 3, :] * cw[0:1]
        for i in range(1, GDN_CONV):
            yh = yh + pre_scr[bb, PRE_ROW0 - 3 + i:2 * PRE_ROW0 - 3 + i, :] * cw[i:i + 1]
        heads.append(yh)
        if real_rows >= PRE_ROW0:
            carry = pre_all[(bb + 1) * Tt - 3:(bb + 1) * Tt]
        else:
            carry = pre_scr[bb, PRE_ROW0 - 3 + real_rows:PRE_ROW0 + real_rows, :]
        pre_scr[bb, PRE_ROW0 - 3:PRE_ROW0, :] = carry
        cb_ref[bb] = carry
    if Tt == PRE_ROW0:
        y = heads[0] if Bb == 1 else jnp.concatenate(heads, axis=0)
    else:
        y = pre_all * cw[GDN_CONV - 1:GDN_CONV]
        for k in range(1, GDN_CONV):
            y = y + pltpu.roll(pre_all, k, axis=0) * cw[GDN_CONV - 1 - k:GDN_CONV - k]
        pieces = []
        for bb in range(Bb):
            pieces += [heads[bb], y[bb * Tt + PRE_ROW0:(bb + 1) * Tt]]
        y = jnp.concatenate(pieces, axis=0)
    qkv = y * _sigmoid(y)

    ba = proj[:, GDN_CONV_CH + GDN_VAL_W:]
    prow = prow_ref[...]
    beta_col = _sigmoid(ba)
    loga_col = -jnp.exp(prow[0:1]) * _softplus(ba + prow[1:2])
    pcol = pcol_ref[...]
    loga_row = -jnp.exp(pcol[:, 0:1]) * _softplus(ba_row + pcol[:, 1:2])
    if t_real < t_total:
        tcol = (lax.broadcasted_iota(jnp.int32, (R, 1), 0) & (Tt - 1)) + ti * Tt
        trow = (lax.broadcasted_iota(jnp.int32, (1, R), 1) & (Tt - 1)) + ti * Tt
        beta_col = jnp.where(tcol < t_real, beta_col, 0.0)
        loga_col = jnp.where(tcol < t_real, loga_col, 0.0)
        loga_row = jnp.where(trow < t_real, loga_row, 0.0)

    rep = rep_ref[...]
    beta_rep = _split_dot(beta_col, rep[:, :GDN_VAL_W], 2)
    loga_rep = _split_dot(loga_col, rep[:, GDN_VAL_W:], 3)
    tri, tri_t = _chunk_tri(R, C)
    gcol_rep = _split_dot_rhs(tri, loga_rep, 3)
    grow = _split_dot(loga_row, tri_t, 3)

    r, c, masks = _tri_masks(C)
    eye = jnp.where(r == c, 1.0, 0.0).astype(F32)
    incl = r >= c
    strict = r > c
    z = proj[:, GDN_CONV_CH:GDN_CONV_CH + GDN_VAL_W]
    gng = gng_ref[...]
    scale = GDN_DK ** -0.5

    qs, ks = [], []
    for hh in range(GDN_HEADS):
        qh = qkv[:, hh * GDN_DK:(hh + 1) * GDN_DK]
        kh = qkv[:, GDN_KEY_W + hh * GDN_DK:GDN_KEY_W + (hh + 1) * GDN_DK]
        qs.append(qh * lax.rsqrt(jnp.sum(qh * qh, axis=-1, keepdims=True) + L2_EPS) * scale)
        ks.append(kh * lax.rsqrt(jnp.sum(kh * kh, axis=-1, keepdims=True) + L2_EPS))

    nchunk = Tt // C
    bodies = [(bb, n, hh) for n in range(nchunk) for bb in range(Bb) for hh in range(GDN_HEADS)]
    Ls, pre = [], []
    for (bb, n, hh) in bodies:
        r0 = bb * Tt + n * C
        q = qs[hh][r0:r0 + C]
        k = ks[hh][r0:r0 + C]
        v = qkv[r0:r0 + C, 2 * GDN_KEY_W + hh * GDN_DV:2 * GDN_KEY_W + (hh + 1) * GDN_DV]
        bc = beta_rep[r0:r0 + C, hh * LANES:(hh + 1) * LANES]
        gc = gcol_rep[r0:r0 + C, hh * LANES:(hh + 1) * LANES]
        gr = grow[GDN_HEADS + hh:GDN_HEADS + hh + 1, r0:r0 + C]
        decay = jnp.where(incl, jnp.exp(jnp.where(incl, gc[:, :C] - gr, 0.0)), 0.0)
        eg = jnp.exp(gc)
        kb = k * bc
        kq = _bdot_nt(jnp.concatenate([kb, q], axis=0), k)
        Ls.append(jnp.where(strict, kq[:C] * decay, 0.0))
        qk = jnp.where(incl, kq[C:] * decay, 0.0)
        g_last = gc[C - 1:C, :]
        kd = k * jnp.exp(g_last - gc)
        pre.append((q * eg, qk, jnp.concatenate([v * bc, kb * eg], axis=1), kd, jnp.exp(g_last)))
    Ts = _unit_lower_inverses(Ls, masks, eye)
    uws = [_bdot(T, p[2]) for T, p in zip(Ts, pre)]
    kuws = [_bdot(p[3].T, uw) for p, uw in zip(pre, uws)]
    quws = [_bdot(p[1], uw) for p, uw in zip(pre, uws)]

    for i, (bb, n, hh) in enumerate(bodies):
        S = s_ref[bb, hh]
        qeg, _, _, _, egl = pre[i]
        o = _bdot(qeg - quws[i][:, GDN_DV:], S) + quws[i][:, :GDN_DV]
        s_ref[bb, hh] = S * egl - _bdot(kuws[i][:, GDN_DV:], S) + kuws[i][:, :GDN_DV]
        r0 = bb * Tt + n * C
        zh = z[r0:r0 + C, hh * GDN_DV:(hh + 1) * GDN_DV]
        o_ref[bb, n * C:(n + 1) * C, hh * GDN_DV:(hh + 1) * GDN_DV] = _rms(o, gng) * (zh * _sigmoid(zh))


def _wspec(a, l, **kw):
    if a.ndim == 2:
        return pl.BlockSpec(a.shape, lambda *_: (0, 0), **kw)
    return pl.BlockSpec((None,) + a.shape[1:], lambda *_: (l,) + (0,) * (a.ndim - 1), **kw)


def _gdn_call(x, s0, hist, ls, w, l, *, Bb, Tt, C, t_real):
    B, T, D = x.shape
    assert Tt & (Tt - 1) == 0 and C & (C - 1) == 0 and B % Bb == 0 and T % Tt == 0 and Tt % C == 0
    kern = functools.partial(_gdn_kernel, Bb=Bb, Tt=Tt, C=C, t_real=t_real, t_total=T)
    names = ['norm_mix_g', 'gdn_w', 'gdn_wba_t', 'gdn_conv_w', 'gdn_prow', 'gdn_pcol', 'gdn_norm_g', 'gdn_rep']
    return pl.pallas_call(
        kern,
        grid=(B // Bb, T // Tt),
        in_specs=[
            pl.BlockSpec((Bb, Tt, D), lambda b, t: (b, t, 0)),
            pl.BlockSpec((None, Bb, GDN_HEADS, GDN_DK, GDN_DV), lambda b, t: (ls, b, 0, 0, 0)),
            pl.BlockSpec((None, Bb, GDN_CONV - 1, GDN_CONV_CH), lambda b, t: (ls, b, 0, 0)),
        ] + [_wspec(w[n], l) for n in names],
        out_specs=[
            pl.BlockSpec((Bb, Tt, GDN_VAL_W), lambda b, t: (b, t, 0)),
            pl.BlockSpec((Bb, GDN_HEADS, GDN_DK, GDN_DV), lambda b, t: (b, 0, 0, 0)),
            pl.BlockSpec((Bb, GDN_CONV - 1, GDN_CONV_CH), lambda b, t: (b, 0, 0)),
        ],
        out_shape=[
            jax.ShapeDtypeStruct((B, T, GDN_VAL_W), F32),
            jax.ShapeDtypeStruct((B, GDN_HEADS, GDN_DK, GDN_DV), F32),
            jax.ShapeDtypeStruct((B, GDN_CONV - 1, GDN_CONV_CH), F32),
        ],
        scratch_shapes=[pltpu.VMEM((Bb, 2 * PRE_ROW0, GDN_CONV_CH), F32)],
        compiler_params=pltpu.CompilerParams(
            dimension_semantics=("arbitrary", "arbitrary"), vmem_limit_bytes=VMEM_LIMIT),
        name="gdn_mixer",
    )(x, s0, hist, *[w[n] for n in names])


def _rwkv_kernel(x_ref, s0_ref, sh_ref, ng_ref, wr_ref, mu_ref, w0_ref, w2_ref, a0_ref, a2_ref, g2_ref,
                 kk_ref, ka_ref, rk_ref, lng_ref, lnb_ref, e_ref,
                 o_ref, s_ref, sho_ref, pre_scr, y_scr, sp_scr, *, Bb, Tt, C, t_real, t_total):
    ti = pl.program_id(1)
    N = RWKV_HEAD
    R = Bb * Tt

    @pl.when(ti == 0)
    def _():
        pre_scr[:, 0:1, :] = sh_ref[...]

    h = _rms(x_ref[...].reshape(R, D_MODEL), ng_ref[...]).astype(BF16)
    pre = jnp.dot(h, wr_ref[...], preferred_element_type=F32)
    real_rows = min(Tt, t_real)
    rolled = pltpu.roll(pre, 1, axis=0)
    first = lax.broadcasted_iota(jnp.int32, (Tt, 1), 0) == 0
    prevs = []
    for bb in range(Bb):
        prevs.append(jnp.where(first, pre_scr[bb, 0:1, :], rolled[bb * Tt:(bb + 1) * Tt]))
        last = pre[bb * Tt + real_rows - 1:bb * Tt + real_rows]
        pre_scr[bb, 0:1, :] = last
        sho_ref[bb] = last
    prev = prevs[0] if Bb == 1 else jnp.concatenate(prevs, axis=0)

    xm = pre + (prev - pre) * mu_ref[...]
    W = RWKV_W
    rr = xm[:, 0:W]
    kx = xm[:, W:2 * W]
    vv = xm[:, 2 * W:3 * W]
    wlo = xm[:, 3 * W:3 * W + LANES]
    alo = xm[:, 3 * W + LANES:3 * W + 2 * LANES]
    glo = xm[:, 3 * W + 2 * LANES:3 * W + 3 * LANES]
    w_log = -_softplus(-(w0_ref[...] + _bdot(jnp.tanh(wlo), w2_ref[...]))) - 0.5
    logd = -jnp.exp(w_log)
    a = _sigmoid(a0_ref[...] + _bdot(alo, a2_ref[...]))
    gb = _bdot(_sigmoid(glo), g2_ref[...])
    E = e_ref[...]
    kkr = kx * kk_ref[...]
    kk = kkr * lax.rsqrt(_split_dot(kkr * kkr, E, 1) + L2_EPS)
    kb = kx * (1.0 + (a - 1.0) * ka_ref[...])
    a_eff = a
    v_eff = vv
    if t_real < t_total:
        tcol = (lax.broadcasted_iota(jnp.int32, (R, 1), 0) & (Tt - 1)) + ti * Tt
        valid = tcol < t_real
        logd = jnp.where(valid, logd, 0.0)
        a_eff = jnp.where(valid, a, 0.0)
        v_eff = jnp.where(valid, vv, 0.0)

    tri, _ = _chunk_tri(R, C)
    logG = _split_dot_rhs(tri, logd, 3)
    G = jnp.exp(logG)
    Ginv = jnp.exp(-logG)
    aq_all = kk * jnp.exp(logG - logd)
    bk_all = -(a_eff * kk) * Ginv
    kd_all = kb * Ginv
    rq_all = rr * G

    HP = RWKV_HEADS // 2
    C2 = 2 * C
    even_c = lax.broadcasted_iota(jnp.int32, (1, LANES), 1) < N
    rt = lax.broadcasted_iota(jnp.int32, (C, C2), 0)
    lt = lax.broadcasted_iota(jnp.int32, (C, C2), 1)
    ct = lt & (C - 1)
    even_t = lt < C
    strict = rt > ct
    incl = rt >= ct
    eye = jnp.where(rt == ct, 1.0, 0.0).astype(F32)
    masks = []
    k = 0
    while (1 << k) < C:
        masks.append(jnp.where(((rt >> k) ^ (ct >> k)) == 1, jnp.where(((rt >> k) & 1) == 1, 1.0, 0.0), 0.0)
                     .astype(F32))
        k += 1
    nshift = N.bit_length() - 1
    same_head = ((lax.broadcasted_iota(jnp.int32, (LANES, LANES), 0) >> nshift)
                 == (lax.broadcasted_iota(jnp.int32, (LANES, LANES), 1) >> nshift))

    def stack_c(y):
        return jnp.concatenate([jnp.where(even_c, y, 0.0), jnp.where(even_c, 0.0, y)], axis=0)

    def stack_t(t):
        return jnp.concatenate([jnp.where(even_t, t, 0.0), jnp.where(even_t, 0.0, t)], axis=0)

    @pl.when(ti == 0)
    def _():
        zero = jnp.zeros((N, N), F32)
        for bb in range(Bb):
            for pp in range(HP):
                top = jnp.concatenate([s0_ref[bb, 2 * pp], zero], axis=1)
                bot = jnp.concatenate([zero, s0_ref[bb, 2 * pp + 1]], axis=1)
                sp_scr[bb, pp] = jnp.concatenate([top, bot], axis=0)

    nchunk = Tt // C
    bodies = [(bb, n, pp) for n in range(nchunk) for bb in range(Bb) for pp in range(HP)]
    nb = len(bodies)
    aqs, bks, kds, rqs, vhs, gls = [], [], [], [], [], []
    for (bb, n, pp) in bodies:
        r0 = bb * Tt + n * C
        lp = slice(pp * LANES, (pp + 1) * LANES)
        aqs.append(aq_all[r0:r0 + C, lp])
        bks.append(bk_all[r0:r0 + C, lp])
        kds.append(kd_all[r0:r0 + C, lp])
        rqs.append(rq_all[r0:r0 + C, lp])
        vhs.append(v_eff[r0:r0 + C, lp])
        gls.append(G[r0 + C - 1:r0 + C, lp])
    ars = [jnp.concatenate([aqs[i], rqs[i]], axis=0) for i in range(nb)]
    sbs = [_bdot_nt(ars[i], stack_c(bks[i])) for i in range(nb)]
    sks = [_bdot_nt(ars[i], stack_c(kds[i])) for i in range(nb)]
    Ls = [jnp.where(strict, -sbs[i][:C], 0.0) for i in range(nb)]
    ras = [jnp.where(incl, sbs[i][C:], 0.0) for i in range(nb)]
    rks = [jnp.where(incl, sks[i][C:], 0.0) for i in range(nb)]
    vstk = [stack_c(vhs[i]) for i in range(nb)]
    bmvs = [_bdot(jnp.where(strict, sks[i][:C], 0.0), vstk[i]) for i in range(nb)]
    Ts = [eye - L * masks[0] for L in Ls]
    for m in masks[1:]:
        tmp = [_bdot(Ls[i] * m, stack_t(Ts[i])) for i in range(nb)]
        Ts = [Ts[i] - _bdot(Ts[i], stack_t(tmp[i])) for i in range(nb)]
    TAs = [_bdot(Ts[i], stack_c(aqs[i])) for i in range(nb)]
    TBVs = [_bdot(Ts[i], stack_c(bmvs[i])) for i in range(nb)]
    Xs = [bks[i] * gls[i] for i in range(nb)]
    Zs = [kds[i] * gls[i] for i in range(nb)]
    Q1s = [jnp.where(same_head, _bdot(TAs[i].T, Xs[i]), 0.0) for i in range(nb)]
    M0s = [jnp.where(same_head,
                     _bdot(jnp.concatenate([TBVs[i], vhs[i]], axis=0).T, jnp.concatenate([Xs[i], Zs[i]], axis=0)),
                     0.0) for i in range(nb)]
    rqps = [rqs[i] + _bdot(ras[i], stack_c(TAs[i])) for i in range(nb)]
    y0s = [_bdot(ras[i], stack_c(TBVs[i])) + _bdot(rks[i], vstk[i]) for i in range(nb)]

    per_chunk = Bb * HP
    for n in range(nchunk):
        idx = range(n * per_chunk, (n + 1) * per_chunk)
        Ss = {i: sp_scr[bodies[i][0], bodies[i][2]] for i in idx}
        for i in idx:
            bb, _, pp = bodies[i]
            r0 = bb * Tt + n * C
            y_scr[r0:r0 + C, pp * LANES:(pp + 1) * LANES] = _bdot_nt(rqps[i], Ss[i]) + y0s[i]
        for i in idx:
            bb, _, pp = bodies[i]
            sp_scr[bb, pp] = Ss[i] * gls[i] + _bdot(Ss[i], Q1s[i]) + M0s[i]

    @pl.when(ti == pl.num_programs(1) - 1)
    def _():
        for bb in range(Bb):
            for pp in range(HP):
                sp = sp_scr[bb, pp]
                s_ref[bb, 2 * pp] = sp[:N, :N]
                s_ref[bb, 2 * pp + 1] = sp[N:, N:]

    y = y_scr[...]
    inv_n = 1.0 / N
    mu = _split_dot(y, E, 2) * inv_n
    yc = y - mu
    var = _split_dot(yc * yc, E, 1) * inv_n
    yn = yc * lax.rsqrt(var + RWKV_LN_EPS) * lng_ref[...] + lnb_ref[...]
    bonus = _split_dot(rr * kb * rk_ref[...], E, 1) * vv
    o_ref[...] = ((yn + bonus) * gb).reshape(Bb, Tt, RWKV_W)


def _rwkv_call(x, s0, shift, ls, w, l, *, Bb, Tt, C, t_real):
    B, T, D = x.shape
    assert Tt & (Tt - 1) == 0 and C & (C - 1) == 0 and B % Bb == 0 and T % Tt == 0 and Tt % C == 0
    kern = functools.partial(_rwkv_kernel, Bb=Bb, Tt=Tt, C=C, t_real=t_real, t_total=T)
    names = ['norm_mix_g', 'rwkv_w', 'rwkv_mu', 'rwkv_w0', 'rwkv_w2', 'rwkv_a0', 'rwkv_a2', 'rwkv_g2',
             'rwkv_k_k', 'rwkv_k_a', 'rwkv_r_k', 'rwkv_ln_g', 'rwkv_ln_b', 'rwkv_e']
    return pl.pallas_call(
        kern,
        grid=(B // Bb, T // Tt),
        in_specs=[
            pl.BlockSpec((Bb, Tt, D), lambda b, t: (b, t, 0)),
            pl.BlockSpec((None, Bb, RWKV_HEADS, RWKV_HEAD, RWKV_HEAD), lambda b, t: (ls, b, 0, 0, 0)),
            pl.BlockSpec((None, Bb, 1, RW_PAD), lambda b, t: (ls, b, 0, 0)),
        ] + [_wspec(w[n], l) for n in names],
        out_specs=[
            pl.BlockSpec((Bb, Tt, RWKV_W), lambda b, t: (b, t, 0)),
            pl.BlockSpec((Bb, RWKV_HEADS, RWKV_HEAD, RWKV_HEAD), lambda b, t: (b, 0, 0, 0)),
            pl.BlockSpec((Bb, 1, RW_PAD), lambda b, t: (b, 0, 0)),
        ],
        out_shape=[
            jax.ShapeDtypeStruct((B, T, RWKV_W), F32),
            jax.ShapeDtypeStruct((B, RWKV_HEADS, RWKV_HEAD, RWKV_HEAD), F32),
            jax.ShapeDtypeStruct((B, 1, RW_PAD), F32),
        ],
        scratch_shapes=[pltpu.VMEM((Bb, SUBLANES, RW_PAD), F32), pltpu.VMEM((Bb * Tt, RWKV_W), F32),
                        pltpu.VMEM((Bb, RWKV_HEADS // 2, 2 * RWKV_HEAD, 2 * RWKV_HEAD), F32)],
        compiler_params=pltpu.CompilerParams(
            dimension_semantics=("arbitrary", "arbitrary"), vmem_limit_bytes=VMEM_LIMIT),
        name="rwkv_mixer",
    )(x, s0, shift, *[w[n] for n in names])


def _mla_proj_kernel(x_ref, ng_ref, wm_ref, qg_ref, kvg_ref, wq_ref, cos_ref, sin_ref, *rest, prompt):
    if prompt:
        wuk_ref, wuv_ref, ckv_ref, kr_ref, q_ref, k_ref, v_ref = rest
    else:
        ckv_ref, kr_ref, qn_ref, qr_ref = rest
    h = _rms(x_ref[...], ng_ref[...]).astype(BF16)
    p = jnp.dot(h, wm_ref[...], preferred_element_type=F32)
    cqn = _rms(p[:, :MLA_Q_RANK], qg_ref[...]).astype(BF16)
    ckv = _rms(p[:, MLA_Q_RANK:MLA_Q_RANK + MLA_KV_RANK], kvg_ref[...])
    o = MLA_Q_RANK + MLA_KV_RANK
    cos = cos_ref[...]
    sin = sin_ref[...]
    krp = p[:, o:o + LANES] * cos + p[:, o + LANES:o + 2 * LANES] * sin
    ckv_ref[...] = ckv
    kr_ref[...] = krp[:, :MLA_ROPE]
    q = jnp.dot(cqn, wq_ref[...], preferred_element_type=F32)
    wn = MLA_HEADS * MLA_NOPE
    wr = MLA_HEADS * LANES
    cos4 = jnp.concatenate([cos] * MLA_HEADS, axis=1)
    sin4 = jnp.concatenate([sin] * MLA_HEADS, axis=1)
    qn = q[:, :wn] * MLA_SCALE
    qr = (q[:, wn:wn + wr] * cos4 + q[:, wn + wr:wn + 2 * wr] * sin4) * MLA_SCALE
    if prompt:
        cb = ckv.astype(BF16)
        kn = jnp.dot(cb, wuk_ref[...], preferred_element_type=F32)
        qparts, kparts = [], []
        for hh in range(MLA_HEADS):
            qparts += [qn[:, hh * MLA_NOPE:(hh + 1) * MLA_NOPE], qr[:, hh * LANES:(hh + 1) * LANES]]
            kparts += [kn[:, hh * MLA_NOPE:(hh + 1) * MLA_NOPE], krp]
        q_ref[...] = jnp.concatenate(qparts, axis=1).astype(BF16)
        k_ref[...] = jnp.concatenate(kparts, axis=1).astype(BF16)
        v_ref[...] = lax.dot_general(wuv_ref[...], cb, (((1,), (1,)), ((), ())),
                                     preferred_element_type=F32).astype(BF16)
    else:
        qn_ref[...] = qn
        qr_ref[...] = qr


def _mla_proj_call(x2, w, l, cos, sin, *, tr, prompt):
    M, D = x2.shape
    nt = cos.shape[0] // tr
    full = lambda a: _wspec(a, l)
    tab = pl.BlockSpec((tr, LANES), lambda i: (i % nt, 0))
    tok = lambda n: pl.BlockSpec((tr, n), lambda i: (i, 0))
    ins = [x2, w['norm_mix_g'], w['mla_w'], w['mla_q_norm_g'], w['mla_kv_norm_g'], w['mla_wq'], cos, sin]
    in_specs = [tok(D)] + [full(a) for a in ins[1:6]] + [tab, tab]
    out_specs = [tok(MLA_KV_RANK), tok(MLA_ROPE)]
    out_shape = [jax.ShapeDtypeStruct((M, MLA_KV_RANK), F32), jax.ShapeDtypeStruct((M, MLA_ROPE), F32)]
    if prompt:
        T = cos.shape[0]
        ins += [w['mla_wuk'], w['mla_wuv_t']]
        in_specs += [full(w['mla_wuk']), full(w['mla_wuv_t'])]
        out_specs += [tok(MLA_HEADS * QK_HEAD), tok(MLA_HEADS * QK_HEAD),
                      pl.BlockSpec((None, MLA_HEADS * MLA_V, tr), lambda i: (i // nt, 0, i % nt))]
        out_shape += [jax.ShapeDtypeStruct((M, MLA_HEADS * QK_HEAD), BF16),
                      jax.ShapeDtypeStruct((M, MLA_HEADS * QK_HEAD), BF16),
                      jax.ShapeDtypeStruct((M // T, MLA_HEADS * MLA_V, T), BF16)]
    else:
        out_specs += [tok(MLA_HEADS * MLA_NOPE), tok(MLA_HEADS * LANES)]
        out_shape += [jax.ShapeDtypeStruct((M, MLA_HEADS * MLA_NOPE), F32),
                      jax.ShapeDtypeStruct((M, MLA_HEADS * LANES), F32)]
    return pl.pallas_call(
        functools.partial(_mla_proj_kernel, prompt=prompt),
        grid=(M // tr,),
        in_specs=in_specs, out_specs=out_specs, out_shape=out_shape,
        compiler_params=pltpu.CompilerParams(dimension_semantics=("arbitrary",), vmem_limit_bytes=VMEM_LIMIT),
        name="mla_proj_prompt" if prompt else "mla_proj_sample",
    )(*ins)


def _flash_kernel(q_ref, k_ref, vt_ref, o_ref, m_scr, l_scr, acc_scr, *, tq):
    qi = pl.program_id(1)
    ki = pl.program_id(2)

    @pl.when(ki == 0)
    def _():
        m_scr[...] = jnp.full(m_scr.shape, -jnp.inf, F32)
        l_scr[...] = jnp.zeros(l_scr.shape, F32)
        acc_scr[...] = jnp.zeros(acc_scr.shape, F32)

    def step(diagonal):
        if diagonal:
            kpos = lax.broadcasted_iota(jnp.int32, (tq, tq), 0)
            qpos = lax.broadcasted_iota(jnp.int32, (tq, tq), 1)
            keep = kpos <= qpos
        sts = []
        for hh in range(MLA_HEADS):
            lq = slice(hh * QK_HEAD, (hh + 1) * QK_HEAD)
            sts.append(lax.dot_general(k_ref[0, :, lq], q_ref[0, :, lq], (((1,), (1,)), ((), ())),
                                       preferred_element_type=F32))
        for hh in range(MLA_HEADS):
            rv = slice(hh * MLA_V, (hh + 1) * MLA_V)
            st = sts[hh]
            if diagonal:
                st = jnp.where(keep, st, -jnp.inf)
            m_old = m_scr[hh:hh + 1, :]
            m_new = jnp.maximum(m_old, jnp.max(st, axis=0, keepdims=True))
            alpha = jnp.exp(m_old - m_new)
            p = jnp.exp(st - m_new)
            l_scr[hh:hh + 1, :] = alpha * l_scr[hh:hh + 1, :] + jnp.sum(p, axis=0, keepdims=True)
            acc_scr[rv, :] = alpha * acc_scr[rv, :] + jnp.dot(vt_ref[0, rv, :], p.astype(BF16),
                                                              preferred_element_type=F32)
            m_scr[hh:hh + 1, :] = m_new

    @pl.when(ki < qi)
    def _():
        step(False)

    @pl.when(ki == qi)
    def _():
        step(True)
        for hh in range(MLA_HEADS):
            rv = slice(hh * MLA_V, (hh + 1) * MLA_V)
            o_ref[0, rv, :] = acc_scr[rv, :] / l_scr[hh:hh + 1, :]


def _flash_call(q, k, vt, *, tq):
    B, T, _ = q.shape
    nq = T // tq
    hv = vt.shape[1]
    return pl.pallas_call(
        functools.partial(_flash_kernel, tq=tq),
        grid=(B, nq, nq),
        in_specs=[pl.BlockSpec((1, tq, q.shape[2]), lambda b, i, j: (b, i, 0)),
                  pl.BlockSpec((1, tq, k.shape[2]), lambda b, i, j: (b, jnp.minimum(i, j), 0)),
                  pl.BlockSpec((1, hv, tq), lambda b, i, j: (b, 0, jnp.minimum(i, j)))],
        out_specs=pl.BlockSpec((1, hv, tq), lambda b, i, j: (b, 0, i)),
        out_shape=jax.ShapeDtypeStruct((B, hv, T), F32),
        scratch_shapes=[pltpu.VMEM((SUBLANES, tq), F32), pltpu.VMEM((SUBLANES, tq), F32),
                        pltpu.VMEM((hv, tq), F32)],
        compiler_params=pltpu.CompilerParams(
            dimension_semantics=("arbitrary", "arbitrary", "arbitrary"), vmem_limit_bytes=VMEM_LIMIT),
        name="mla_flash",
    )(q, k, vt)


def _decode_kernel(pt_ref, qn_ref, qr_ref, ckvn_ref, krn_ref, wuk_ref, wuv_ref, ckv_hbm, krt_hbm, o_ref,
                   bufc, bufk, sem, *, G, NJ, BB, layer, nsteps):
    s = pl.program_id(0)
    total = nsteps * NJ
    ahead = DECODE_SLOTS - 1
    nj_shift = NJ.bit_length() - 1
    page = bufc.shape[3]
    rid = lax.broadcasted_iota(jnp.int32, (SUBLANES, 1), 0)

    def page_copies(t, bi, i, slot):
        pg = pt_ref[(t >> nj_shift) * BB + bi, (t & (NJ - 1)) * G + i]
        return (pltpu.make_async_copy(ckv_hbm.at[layer, pg], bufc.at[bi, slot, i], sem.at[0, slot]),
                pltpu.make_async_copy(krt_hbm.at[layer, pg], bufk.at[bi, slot, i], sem.at[1, slot]))

    @pl.when(s == 0)
    def _():
        for t0 in range(ahead):
            for bi in range(BB):
                for i in range(G):
                    for c in page_copies(t0, bi, i, t0):
                        c.start(priority=i % 2)

    qlats, qrms = [], []
    for bi in range(BB):
        qn = qn_ref[bi]
        qr = qr_ref[bi]
        qlat = jnp.zeros((SUBLANES, MLA_KV_RANK), F32)
        qrm = jnp.zeros((SUBLANES, MLA_ROPE), F32)
        for hh in range(MLA_HEADS):
            ql = _bdot_nt(qn[:, hh * MLA_NOPE:(hh + 1) * MLA_NOPE], wuk_ref[hh])
            qlat = jnp.where(rid == hh, ql[0:1, :], qlat)
            qrm = jnp.where(rid == hh, qr[0:1, hh * LANES:hh * LANES + MLA_ROPE], qrm)
        qlats.append(qlat)
        qrms.append(qrm)

    def body(j, carry):
        t = s * NJ + j
        slot = lax.rem(t, DECODE_SLOTS)
        t_next = jnp.minimum(t + ahead, total - 1)
        slot_next = lax.rem(t + ahead, DECODE_SLOTS)
        for bi in range(BB):
            for i in range(G):
                for c in page_copies(t, bi, i, slot):
                    c.wait()
        scores = [[] for _ in range(BB)]
        for i in range(G):
            for bi in range(BB):
                sc = (_bdot_nt(qlats[bi], bufc[bi, slot, i]) + _bdot(qrms[bi], bufk[bi, slot, i]))
                scores[bi].append(sc)
                for c in page_copies(t_next, bi, i, slot_next):
                    c.start(priority=i % 2)
        out = []
        for bi in range(BB):
            m_old, l_old, acc_old = carry[bi]
            sc = jnp.concatenate(scores[bi], axis=1)
            m_new = jnp.maximum(m_old, jnp.max(sc, axis=-1, keepdims=True))
            alpha = jnp.exp(m_old - m_new)
            p = jnp.exp(sc - m_new)
            l_new = alpha * l_old + jnp.sum(p, axis=-1, keepdims=True)
            acc = alpha * acc_old
            for i in range(G):
                acc = acc + _bdot(p[:, i * page:(i + 1) * page], bufc[bi, slot, i])
            out.append((m_new, l_new, acc))
        return tuple(out)

    init = tuple((jnp.full((SUBLANES, 1), -jnp.inf, F32), jnp.zeros((SUBLANES, 1), F32),
                  jnp.zeros((SUBLANES, MLA_KV_RANK), F32)) for _ in range(BB))
    final = lax.fori_loop(0, NJ, body, init)

    @pl.when(s == nsteps - 1)
    def _():
        for extra in range(ahead):
            for bi in range(BB):
                for i in range(G):
                    for c in page_copies(total - 1, bi, i, (total + extra) % DECODE_SLOTS):
                        c.wait()

    for bi in range(BB):
        m_new, l_new, acc = final[bi]
        ckvn = ckvn_ref[bi][0:1, :]
        krn = krn_ref[bi][0:1, :]
        s_new = (jnp.sum(qlats[bi] * ckvn, axis=-1, keepdims=True)
                 + jnp.sum(qrms[bi] * krn, axis=-1, keepdims=True))
        m_fin = jnp.maximum(m_new, s_new)
        a2 = jnp.exp(m_new - m_fin)
        p_new = jnp.exp(s_new - m_fin)
        l_fin = a2 * l_new + p_new
        o_lat = (a2 * acc + p_new * ckvn) / l_fin
        outs = []
        for hh in range(MLA_HEADS):
            oh = _bdot(o_lat, wuv_ref[hh])
            outs.append(oh[hh:hh + 1, :])
        o_ref[bi] = jnp.broadcast_to(jnp.concatenate(outs, axis=1), (SUBLANES, MLA_HEADS * MLA_V))


def _decode_call(page_table, qn, qr, ckvn, krn, cache_ckv, cache_krope_t, layer, w):
    B = qn.shape[0]
    n_pages = page_table.shape[1]
    page = cache_ckv.shape[2]
    G = _pick_tile(n_pages // 2, 16)
    NJ = n_pages // G
    BB = _pick_tile(B, 2)
    assert NJ >= 2 and NJ & (NJ - 1) == 0
    nsteps = B // BB
    tok = lambda a: pl.BlockSpec((BB,) + a.shape[1:], lambda b, pt: (b, 0, 0))
    full = lambda a: _wspec(a, layer)
    hbm = pl.BlockSpec(memory_space=pl.ANY)
    grid_spec = pltpu.PrefetchScalarGridSpec(
        num_scalar_prefetch=1,
        grid=(nsteps,),
        in_specs=[tok(qn), tok(qr), tok(ckvn), tok(krn), full(w['mla_wuk_h']), full(w['mla_wuv_h']), hbm, hbm],
        out_specs=pl.BlockSpec((BB, SUBLANES, MLA_HEADS * MLA_V), lambda b, pt: (b, 0, 0)),
        scratch_shapes=[pltpu.VMEM((BB, DECODE_SLOTS, G, page, MLA_KV_RANK), F32),
                        pltpu.VMEM((BB, DECODE_SLOTS, G, MLA_ROPE, page), F32),
                        pltpu.SemaphoreType.DMA((2, DECODE_SLOTS))],
    )
    return pl.pallas_call(
        functools.partial(_decode_kernel, G=G, NJ=NJ, BB=BB, layer=layer, nsteps=nsteps),
        grid_spec=grid_spec,
        out_shape=jax.ShapeDtypeStruct((B, SUBLANES, MLA_HEADS * MLA_V), F32),
        compiler_params=pltpu.CompilerParams(
            dimension_semantics=("arbitrary",), vmem_limit_bytes=VMEM_LIMIT),
        name="mla_decode",
    )(page_table, qn, qr, ckvn, krn, w['mla_wuk_h'], w['mla_wuv_h'], cache_ckv, cache_krope_t)


def _merge_ffn_kernel(x_ref, oa_ref, ob_ref, oc_ref, ng_ref, wgt_ref, bg_ref, wa_ref, wb_ref, wc_ref, wo_ref,
                      nf_ref, wup_ref, wdn_ref, fin_ref, y_ref, *, final, oc_transposed):
    x = x_ref[...]
    h = _rms(x, ng_ref[...]).astype(BF16)
    gates = _sigmoid(jnp.dot(h, wgt_ref[...], preferred_element_type=F32) + bg_ref[...])
    D = D_MODEL
    oc = oc_ref[...].T if oc_transposed else oc_ref[...]
    merged = (gates[:, 0:D] * _bdot(oa_ref[...], wa_ref[...])
              + gates[:, D:2 * D] * _bdot(ob_ref[...], wb_ref[...])
              + gates[:, 2 * D:3 * D] * _bdot(oc, wc_ref[...]))
    x1 = x + _bdot(merged, wo_ref[...])
    h2 = _rms(x1, nf_ref[...]).astype(BF16)
    up = jnp.dot(h2, wup_ref[...], preferred_element_type=F32)
    g = up[:, :D_FF]
    x2 = x1 + _bdot(g * _sigmoid(g) * up[:, D_FF:], wdn_ref[...])
    y_ref[...] = _rms(x2, fin_ref[...]) if final else x2


def _merge_ffn_call(x, oa, ob, oc, w, l, fin_g, *, tm, final):
    M, D = x.shape
    const = lambda a: _wspec(a, l, pipeline_mode=pl.Buffered(1))
    row = lambda n: pl.BlockSpec((tm, n), lambda i: (i, 0))
    names = ['norm_mix_g', 'w_gate', 'b_gate', 'w_br_a', 'w_br_b', 'w_br_c', 'w_out', 'norm_ffn_g',
             'w_ffn_up', 'w_ffn_down']
    oc_transposed = oc.ndim == 3
    if oc_transposed:
        nt = oc.shape[2] // tm
        oc_spec = pl.BlockSpec((None, oc.shape[1], tm), lambda i: (i // nt, 0, i % nt))
    else:
        oc_spec = row(oc.shape[1])
    return pl.pallas_call(
        functools.partial(_merge_ffn_kernel, final=final, oc_transposed=oc_transposed),
        grid=(M // tm,),
        in_specs=[row(D), row(oa.shape[1]), row(ob.shape[1]), oc_spec]
                 + [const(w[n]) for n in names] + [const(fin_g)],
        out_specs=row(D),
        out_shape=jax.ShapeDtypeStruct((M, D), F32),
        compiler_params=pltpu.CompilerParams(
            dimension_semantics=("arbitrary",), vmem_limit_bytes=VMEM_LIMIT),
        name="merge_ffn",
    )(x, oa, ob, oc, *[w[n] for n in names], fin_g)


def _pad_cols(a, n):
    return jnp.pad(a, [(0, 0)] * (a.ndim - 1) + [(0, n - a.shape[-1])])


def _swap_halves(a, width):
    shp = a.shape
    a = a.reshape(shp[:-1] + (shp[-1] // width, 2, width // 2))
    return a[..., ::-1, :].reshape(shp)


def _pad_groups(a, width, to):
    shp = a.shape
    a = a.reshape(shp[:-1] + (shp[-1] // width, width))
    a = jnp.pad(a, [(0, 0)] * (a.ndim - 1) + [(0, to - width)])
    return a.reshape(shp[:-1] + (-1,))


def _rw_pad(a):
    W = RWKV_W
    z = jnp.zeros(a.shape[:-1] + (LANES - RWKV_DECAY_LORA,), a.dtype)
    return jnp.concatenate([a[..., :3 * W], a[..., 3 * W:3 * W + RWKV_DECAY_LORA], z,
                            a[..., 3 * W + RWKV_DECAY_LORA:3 * W + RWKV_DECAY_LORA + RWKV_AAA_LORA], z,
                            a[..., 3 * W + RWKV_DECAY_LORA + RWKV_AAA_LORA:]], axis=-1)


def _rw_unpad(a):
    W = RWKV_W
    return jnp.concatenate([a[..., :3 * W], a[..., 3 * W:3 * W + RWKV_DECAY_LORA],
                            a[..., 3 * W + LANES:3 * W + LANES + RWKV_AAA_LORA],
                            a[..., 3 * W + 2 * LANES:]], axis=-1)


def _prep_weights(p):
    w_in = p['w_in']
    depth = w_in.shape[0]
    row = lambda a: a.reshape(depth, 1, -1).astype(F32)
    w = {}
    w['norm_mix_g'] = row(p['norm_mix_g'])
    w['norm_ffn_g'] = row(p['norm_ffn_g'])
    ba = w_in[..., _O_BETA:_O_Z]
    w['gdn_w'] = jnp.concatenate([w_in[..., _O_QKV:_O_BETA], w_in[..., _O_Z:_O_RW], _pad_cols(ba, LANES)],
                                 axis=-1).astype(BF16)
    w['gdn_wba_t'] = jnp.swapaxes(ba, 1, 2).astype(BF16)
    w['gdn_conv_w'] = p['gdn_conv_w']
    zero4 = jnp.zeros((depth, GDN_HEADS), F32)
    a_log = jnp.concatenate([zero4, p['gdn_a_log']], axis=1)
    dt_b = jnp.concatenate([zero4, p['gdn_dt_bias']], axis=1)
    w['gdn_prow'] = _pad_cols(jnp.stack([a_log, dt_b], axis=1), LANES)
    w['gdn_pcol'] = jnp.stack([a_log, dt_b], axis=2)
    w['gdn_norm_g'] = row(p['gdn_norm_g'])
    lane = jnp.arange(LANES)[:, None]
    head_of_col = jnp.arange(GDN_VAL_W)[None, :] // GDN_DV
    w['gdn_rep'] = jnp.concatenate([lane == head_of_col, lane == head_of_col + GDN_HEADS], axis=1).astype(BF16)
    w['rwkv_w'] = _rw_pad(w_in[..., _O_RW:_O_CQ]).astype(BF16)
    w['rwkv_mu'] = row(_rw_pad(p['rwkv_mu']))
    w['rwkv_w0'] = row(p['rwkv_w0'])
    w['rwkv_w2'] = jnp.pad(p['rwkv_w2'], ((0, 0), (0, LANES - RWKV_DECAY_LORA), (0, 0))).astype(BF16)
    w['rwkv_a0'] = row(p['rwkv_a0'])
    w['rwkv_a2'] = jnp.pad(p['rwkv_a2'], ((0, 0), (0, LANES - RWKV_AAA_LORA), (0, 0))).astype(BF16)
    w['rwkv_g2'] = p['rwkv_g2'].astype(BF16)
    w['rwkv_k_k'] = row(p['rwkv_k_k'])
    w['rwkv_k_a'] = row(p['rwkv_k_a'])
    w['rwkv_r_k'] = row(p['rwkv_r_k'])
    w['rwkv_ln_g'] = row(p['rwkv_ln_g'])
    w['rwkv_ln_b'] = row(p['rwkv_ln_b'])
    hid = jnp.arange(RWKV_W) // RWKV_HEAD
    w['rwkv_e'] = (hid[:, None] == hid[None, :]).astype(BF16)
    w_kr = w_in[..., _O_KR:_O_GATE]
    w['mla_w'] = jnp.concatenate([w_in[..., _O_CQ:_O_KR], _pad_cols(w_kr, LANES),
                                  _pad_cols(_swap_halves(w_kr, MLA_ROPE), LANES)], axis=-1).astype(BF16)
    w['mla_q_norm_g'] = row(p['mla_q_norm_g'])
    w['mla_kv_norm_g'] = row(p['mla_kv_norm_g'])
    w_uq = p['mla_w_uq']
    wq_n = w_uq[..., :MLA_NOPE].reshape(depth, MLA_Q_RANK, -1)
    wq_r = w_uq[..., MLA_NOPE:].reshape(depth, MLA_Q_RANK, -1)
    w['mla_wq'] = jnp.concatenate([wq_n, _pad_groups(wq_r, MLA_ROPE, LANES),
                                   _pad_groups(_swap_halves(wq_r, MLA_ROPE), MLA_ROPE, LANES)],
                                  axis=-1).astype(BF16)
    w['mla_wuk'] = p['mla_w_uk'].reshape(depth, MLA_KV_RANK, -1).astype(BF16)
    w['mla_wuv_t'] = jnp.swapaxes(p['mla_w_uv'].reshape(depth, MLA_KV_RANK, -1), 1, 2).astype(BF16)
    w['mla_wuk_h'] = jnp.transpose(p['mla_w_uk'], (0, 2, 1, 3)).astype(BF16)
    w['mla_wuv_h'] = jnp.transpose(p['mla_w_uv'], (0, 2, 1, 3)).astype(BF16)
    w['w_gate'] = w_in[..., _O_GATE:].astype(BF16)
    w['b_gate'] = row(p['b_gate'])
    for n in ('w_br_a', 'w_br_b', 'w_br_c', 'w_out', 'w_ffn_up', 'w_ffn_down'):
        w[n] = p[n].astype(BF16)
    return w


def _rope_tables(pos):
    half = MLA_ROPE // 2
    freq = ROPE_THETA ** (-jnp.arange(half, dtype=F32) / half)
    ang = pos.astype(F32)[:, None] * freq
    cos = jnp.cos(ang)
    sin = jnp.sin(ang)
    return (_pad_cols(jnp.concatenate([cos, cos], axis=1), LANES),
            _pad_cols(jnp.concatenate([-sin, sin], axis=1), LANES))


def _pick_tile(T, pref):
    t = min(T, pref)
    while T % t:
        t //= 2
    return t


def kernel(x_prompt, x_sample, cache_ckv, cache_krope, page_table, state_gdn, state_gdn_conv, state_rwkv, state_rwkv_shift, norm_mix_g, norm_ffn_g, norm_final_g, w_in, b_gate, gdn_conv_w, gdn_a_log, gdn_dt_bias, gdn_norm_g, rwkv_mu, rwkv_w0, rwkv_w2, rwkv_a0, rwkv_a2, rwkv_g2, rwkv_k_k, rwkv_k_a, rwkv_r_k, rwkv_ln_g, rwkv_ln_b, mla_q_norm_g, mla_kv_norm_g, mla_w_uq, mla_w_uk, mla_w_uv, w_br_a, w_br_b, w_br_c, w_out, w_ffn_up, w_ffn_down):
    p = dict(norm_mix_g=norm_mix_g, norm_ffn_g=norm_ffn_g, w_in=w_in, b_gate=b_gate, gdn_conv_w=gdn_conv_w,
             gdn_a_log=gdn_a_log, gdn_dt_bias=gdn_dt_bias, gdn_norm_g=gdn_norm_g, rwkv_mu=rwkv_mu,
             rwkv_w0=rwkv_w0, rwkv_w2=rwkv_w2, rwkv_a0=rwkv_a0, rwkv_a2=rwkv_a2, rwkv_g2=rwkv_g2,
             rwkv_k_k=rwkv_k_k, rwkv_k_a=rwkv_k_a, rwkv_r_k=rwkv_r_k, rwkv_ln_g=rwkv_ln_g,
             rwkv_ln_b=rwkv_ln_b, mla_q_norm_g=mla_q_norm_g, mla_kv_norm_g=mla_kv_norm_g,
             mla_w_uq=mla_w_uq, mla_w_uk=mla_w_uk, mla_w_uv=mla_w_uv, w_br_a=w_br_a, w_br_b=w_br_b,
             w_br_c=w_br_c, w_out=w_out, w_ffn_up=w_ffn_up, w_ffn_down=w_ffn_down)
    depth = w_in.shape[0]
    bp, sp, D = x_prompt.shape
    bs, ss, _ = x_sample.shape
    assert ss == 1, "the sample group decodes one new token per sequence"
    past_len = page_table.shape[1] * cache_ckv.shape[2]
    fin_g = norm_final_g.reshape(1, -1).astype(F32)
    cache_krope_t = jnp.swapaxes(cache_krope, 2, 3)

    tp = _pick_tile(sp, 256)
    cp = _pick_tile(tp, 64)
    tq = _pick_tile(sp, 512)
    ts = SUBLANES
    bb_s = _pick_tile(bs, 8)
    cos_p, sin_p = _rope_tables(jnp.arange(sp, dtype=jnp.int32))
    cos_s, sin_s = _rope_tables(jnp.full((bs * ts,), past_len, dtype=jnp.int32))
    zeros_p = dict(
        gdn=jnp.zeros((1, bp, GDN_HEADS, GDN_DK, GDN_DV), F32),
        conv=jnp.zeros((1, bp, GDN_CONV - 1, GDN_CONV_CH), F32),
        rwkv=jnp.zeros((1, bp, RWKV_HEADS, RWKV_HEAD, RWKV_HEAD), F32),
        shift=jnp.zeros((1, bp, 1, RW_PAD), F32))
    shift_s = _rw_pad(state_rwkv_shift)[:, :, None]

    w = _prep_weights(p)
    xp = x_prompt
    xs = jnp.pad(x_sample, ((0, 0), (0, ts - ss), (0, 0)))
    new_p = [[] for _ in range(6)]
    new_s = [[] for _ in range(6)]
    mp = bp * sp
    tm_p = _pick_tile(sp, 512)
    for l in range(depth):
        final = l == depth - 1
        oa, s_g, cbuf = _gdn_call(xp, zeros_p['gdn'], zeros_p['conv'], 0, w, l, Bb=1, Tt=tp, C=cp, t_real=sp)
        ob, s_r, sh = _rwkv_call(xp, zeros_p['rwkv'], zeros_p['shift'], 0, w, l, Bb=1, Tt=tp, C=cp, t_real=sp)
        ckv, kr, q, k, vt = _mla_proj_call(xp.reshape(mp, D), w, l, cos_p, sin_p, tr=tq, prompt=True)
        oc_t = _flash_call(q.reshape(bp, sp, -1), k.reshape(bp, sp, -1), vt, tq=tq)
        for lst, arr in zip(new_p, (ckv.reshape(bp, sp, -1), kr.reshape(bp, sp, -1), s_g, cbuf, s_r, sh[:, 0])):
            lst.append(arr)
        xp = _merge_ffn_call(xp.reshape(mp, D), oa.reshape(mp, -1), ob.reshape(mp, -1),
                             oc_t, w, l, fin_g, tm=tm_p, final=final).reshape(bp, sp, D)
        oa, s_g, cbuf = _gdn_call(xs, state_gdn, state_gdn_conv, l, w, l, Bb=bb_s, Tt=ts, C=ts, t_real=ss)
        ob, s_r, sh = _rwkv_call(xs, state_rwkv, shift_s, l, w, l, Bb=bb_s, Tt=ts, C=ts, t_real=ss)
        ckv, kr, qn, qr = _mla_proj_call(xs.reshape(bs * ts, D), w, l, cos_s, sin_s, tr=bs * ts, prompt=False)
        ckv = ckv.reshape(bs, ts, -1)
        kr = kr.reshape(bs, ts, -1)
        oc = _decode_call(page_table, qn.reshape(bs, ts, -1), qr.reshape(bs, ts, -1), ckv, kr,
                          cache_ckv, cache_krope_t, l, w)
        for lst, arr in zip(new_s, (ckv[:, :ss], kr[:, :ss], s_g, cbuf, s_r, sh[:, 0])):
            lst.append(arr)
        xs_real = _merge_ffn_call(xs[:, 0], oa[:, 0], ob[:, 0], oc[:, 0], w, l, fin_g, tm=bs, final=final)
        xs = jnp.pad(xs_real[:, None], ((0, 0), (0, ts - ss), (0, 0)))
    y_prompt = xp
    y_sample = xs[:, :ss]
    outs_p = [jnp.stack(a) for a in new_p]
    outs_s = [jnp.stack(a) for a in new_s]
    outs_p[5] = _rw_unpad(outs_p[5])
    outs_s[5] = _rw_unpad(outs_s[5])
    return (y_prompt, y_sample, *outs_p, *outs_s)
```

```python
import functools
import math

import jax
import jax.numpy as jnp
from jax import lax
from jax.experimental import pallas as pl
from jax.experimental.pallas import tpu as pltpu

F32 = jnp.float32
BF16 = jnp.bfloat16

D_MODEL = 1024
GDN_HEADS = 4
GDN_DK = 128
GDN_DV = 128
GDN_CONV = 4
GDN_KEY_W = GDN_HEADS * GDN_DK
GDN_VAL_W = GDN_HEADS * GDN_DV
GDN_CONV_CH = 2 * GDN_KEY_W + GDN_VAL_W
RWKV_HEADS = 8
RWKV_HEAD = 64
RWKV_W = RWKV_HEADS * RWKV_HEAD
RWKV_DECAY_LORA = 64
RWKV_AAA_LORA = 64
RWKV_GATE_LORA = 128
RWKV_COLS = 3 * RWKV_W + RWKV_DECAY_LORA + RWKV_AAA_LORA + RWKV_GATE_LORA
RWKV_LN_EPS = 64e-5
MLA_HEADS = 4
MLA_Q_RANK = 256
MLA_KV_RANK = 256
MLA_NOPE = 128
MLA_ROPE = 64
MLA_V = 128
MLA_SCALE = 1.0 / math.sqrt(MLA_NOPE + MLA_ROPE)
ROPE_THETA = 10000.0
D_FF = ((8 * D_MODEL // 3 + 255) // 256) * 256
NORM_EPS = 1e-6
L2_EPS = 1e-6

LANES = 128
SUBLANES = 8
VMEM_LIMIT = 56 * 1024 * 1024

_O_QKV = 0
_O_BETA = _O_QKV + GDN_CONV_CH
_O_ALPHA = _O_BETA + GDN_HEADS
_O_Z = _O_ALPHA + GDN_HEADS
_O_RW = _O_Z + GDN_VAL_W
_O_CQ = _O_RW + RWKV_COLS
_O_CKV = _O_CQ + MLA_Q_RANK
_O_KR = _O_CKV + MLA_KV_RANK
_O_GATE = _O_KR + MLA_ROPE

RW_PAD = 3 * RWKV_W + 3 * LANES
PRE_ROW0 = SUBLANES
QK_HEAD = MLA_NOPE + LANES
DECODE_SLOTS = 3
RWKV_GROUP = 2

MIXER_TILE = 256
MIXER_CHUNK = 64
MIXER_SEQS_PROMPT = 2
MIXER_SEQS_SAMPLE = 8
ROW_TILE = 512
DECODE_PAGES = 16
DECODE_SEQS = 2


def _bdot(a, b):
    return jnp.dot(a.astype(BF16), b.astype(BF16), preferred_element_type=F32)


def _bdot_nt(a, b):
    return lax.dot_general(a.astype(BF16), b.astype(BF16), (((1,), (1,)), ((), ())),
                           preferred_element_type=F32)


def _split_dot(a, b_exact, parts):
    acc = None
    rem = a
    for _ in range(parts):
        hi = rem.astype(BF16)
        t = jnp.dot(hi, b_exact, preferred_element_type=F32)
        acc = t if acc is None else acc + t
        rem = rem - hi.astype(F32)
    return acc


def _split_dot_rhs(a_exact, b, parts):
    acc = None
    rem = b
    for _ in range(parts):
        hi = rem.astype(BF16)
        t = jnp.dot(a_exact, hi, preferred_element_type=F32)
        acc = t if acc is None else acc + t
        rem = rem - hi.astype(F32)
    return acc


def _rms(x, g, eps=NORM_EPS):
    return x * lax.rsqrt(jnp.mean(x * x, axis=-1, keepdims=True) + eps) * g


def _sigmoid(x):
    return 0.5 * jnp.tanh(0.5 * x) + 0.5


def _softplus(x):
    return jnp.maximum(x, 0.0) + jnp.log(1.0 + jnp.exp(-jnp.abs(x)))


def _tri_masks(C):
    r = lax.broadcasted_iota(jnp.int32, (C, C), 0)
    c = lax.broadcasted_iota(jnp.int32, (C, C), 1)
    masks = []
    k = 0
    while (1 << k) < C:
        rr = r >> k
        cc = c >> k
        m = jnp.where((rr ^ cc) == 1, jnp.where((rr & 1) == 1, 1.0, 0.0), 0.0)
        masks.append(m.astype(F32))
        k += 1
    return r, c, masks


def _unit_lower_inverses(Ls, masks, eye):
    Ts = [eye - L * masks[0] for L in Ls]
    for m in masks[1:]:
        tmp = [_bdot(L * m, T) for L, T in zip(Ls, Ts)]
        Ts = [T - _bdot(T, t) for T, t in zip(Ts, tmp)]
    return Ts


def _chunk_tri(R, C):
    rt = lax.broadcasted_iota(jnp.int32, (R, R), 0)
    ct = lax.broadcasted_iota(jnp.int32, (R, R), 1)
    sh = C.bit_length() - 1
    same = (rt >> sh) == (ct >> sh)
    tri = jnp.where(same, jnp.where(rt >= ct, 1.0, 0.0), 0.0).astype(BF16)
    tri_t = jnp.where(same, jnp.where(ct >= rt, 1.0, 0.0), 0.0).astype(BF16)
    return tri, tri_t


def _gdn_kernel(x_ref, s0_ref, hist_ref, ng_ref, wg_ref, wbat_ref, cw_ref, prow_ref, pcol_ref, gng_ref, rep_ref,
                o_ref, s_ref, cb_ref, pre_scr, *, Bb, Tt, C, t_real, t_total):
    ti = pl.program_id(1)
    R = Bb * Tt

    @pl.when(ti == 0)
    def _():
        s_ref[...] = s0_ref[...]
        pre_scr[:, PRE_ROW0 - 3:PRE_ROW0, :] = hist_ref[...]

    h = _rms(x_ref[...].reshape(R, D_MODEL), ng_ref[...]).astype(BF16)
    proj = jnp.dot(h, wg_ref[...], preferred_element_type=F32)
    ba_row = lax.dot_general(wbat_ref[...], h, (((1,), (1,)), ((), ())),
                             preferred_element_type=F32)

    cw = cw_ref[...]
    pre_all = proj[:, :GDN_CONV_CH]
    real_rows = min(Tt, t_real)
    heads = []
    for bb in range(Bb):
        pre_scr[bb, PRE_ROW0:2 * PRE_ROW0, :] = pre_all[bb * Tt:bb * Tt + PRE_ROW0]
        yh = pre_scr[bb, PRE_ROW0 - 3:2 * PRE_ROW0 - 3, :] * cw[0:1]
        for i in range(1, GDN_CONV):
            yh = yh + pre_scr[bb, PRE_ROW0 - 3 + i:2 * PRE_ROW0 - 3 + i, :] * cw[i:i + 1]
        heads.append(yh)
        if real_rows >= PRE_ROW0:
            carry = pre_all[(bb + 1) * Tt - 3:(bb + 1) * Tt]
        else:
            carry = pre_scr[bb, PRE_ROW0 - 3 + real_rows:PRE_ROW0 + real_rows, :]
        pre_scr[bb, PRE_ROW0 - 3:PRE_ROW0, :] = carry
        cb_ref[bb] = carry
    if Tt == PRE_ROW0:
        y = heads[0] if Bb == 1 else jnp.concatenate(heads, axis=0)
    else:
        y = pre_all * cw[GDN_CONV - 1:GDN_CONV]
        for k in range(1, GDN_CONV):
            y = y + pltpu.roll(pre_all, k, axis=0) * cw[GDN_CONV - 1 - k:GDN_CONV - k]
        pieces = []
        for bb in range(Bb):
            pieces += [heads[bb], y[bb * Tt + PRE_ROW0:(bb + 1) * Tt]]
        y = jnp.concatenate(pieces, axis=0)
    qkv = y * _sigmoid(y)

    ba = proj[:, GDN_CONV_CH + GDN_VAL_W:]
    prow = prow_ref[...]
    beta_col = _sigmoid(ba)
    loga_col = -jnp.exp(prow[0:1]) * _softplus(ba + prow[1:2])
    pcol = pcol_ref[...]
    loga_row = -jnp.exp(pcol[:, 0:1]) * _softplus(ba_row + pcol[:, 1:2])
    if t_real < t_total:
        tcol = (lax.broadcasted_iota(jnp.int32, (R, 1), 0) & (Tt - 1)) + ti * Tt
        trow = (lax.broadcasted_iota(jnp.int32, (1, R), 1) & (Tt - 1)) + ti * Tt
        beta_col = jnp.where(tcol < t_real, beta_col, 0.0)
        loga_col = jnp.where(tcol < t_real, loga_col, 0.0)
        loga_row = jnp.where(trow < t_real, loga_row, 0.0)

    rep = rep_ref[...]
    beta_rep = _split_dot(beta_col, rep[:, :GDN_VAL_W], 2)
    loga_rep = _split_dot(loga_col, rep[:, GDN_VAL_W:], 3)
    tri, tri_t = _chunk_tri(R, C)
    gcol_rep = _split_dot_rhs(tri, loga_rep, 3)
    grow = _split_dot(loga_row, tri_t, 3)

    r, c, masks = _tri_masks(C)
    eye = jnp.where(r == c, 1.0, 0.0).astype(F32)
    incl = r >= c
    strict = r > c
    z = proj[:, GDN_CONV_CH:GDN_CONV_CH + GDN_VAL_W]
    gng = gng_ref[...]
    scale = GDN_DK ** -0.5

    qs, ks = [], []
    for hh in range(GDN_HEADS):
        qh = qkv[:, hh * GDN_DK:(hh + 1) * GDN_DK]
        kh = qkv[:, GDN_KEY_W + hh * GDN_DK:GDN_KEY_W + (hh + 1) * GDN_DK]
        qs.append(qh * lax.rsqrt(jnp.sum(qh * qh, axis=-1, keepdims=True) + L2_EPS) * scale)
        ks.append(kh * lax.rsqrt(jnp.sum(kh * kh, axis=-1, keepdims=True) + L2_EPS))

    nchunk = Tt // C
    bodies = [(bb, n, hh) for n in range(nchunk) for bb in range(Bb) for hh in range(GDN_HEADS)]
    Ls, pre = [], []
    for (bb, n, hh) in bodies:
        r0 = bb * Tt + n * C
        q = qs[hh][r0:r0 + C]
        k = ks[hh][r0:r0 + C]
        v = qkv[r0:r0 + C, 2 * GDN_KEY_W + hh * GDN_DV:2 * GDN_KEY_W + (hh + 1) * GDN_DV]
        bc = beta_rep[r0:r0 + C, hh * LANES:(hh + 1) * LANES]
        gc = gcol_rep[r0:r0 + C, hh * LANES:(hh + 1) * LANES]
        gr = grow[GDN_HEADS + hh:GDN_HEADS + hh + 1, r0:r0 + C]
        decay = jnp.where(incl, jnp.exp(jnp.where(incl, gc[:, :C] - gr, 0.0)), 0.0)
        eg = jnp.exp(gc)
        kb = k * bc
        kq = _bdot_nt(jnp.concatenate([kb, q], axis=0), k)
        Ls.append(jnp.where(strict, kq[:C] * decay, 0.0))
        qk = jnp.where(incl, kq[C:] * decay, 0.0)
        g_last = gc[C - 1:C, :]
        kd = k * jnp.exp(g_last - gc)
        pre.append((q * eg, qk, jnp.concatenate([v * bc, kb * eg], axis=1), kd, jnp.exp(g_last)))
    Ts = _unit_lower_inverses(Ls, masks, eye)
    uws = [_bdot(T, p[2]) for T, p in zip(Ts, pre)]
    kuws = [_bdot(p[3].T, uw) for p, uw in zip(pre, uws)]
    quws = [_bdot(p[1], uw) for p, uw in zip(pre, uws)]

    for i, (bb, n, hh) in enumerate(bodies):
        S = s_ref[bb, hh]
        qeg, _, _, _, egl = pre[i]
        o = _bdot(qeg - quws[i][:, GDN_DV:], S) + quws[i][:, :GDN_DV]
        s_ref[bb, hh] = S * egl - _bdot(kuws[i][:, GDN_DV:], S) + kuws[i][:, :GDN_DV]
        r0 = bb * Tt + n * C
        zh = z[r0:r0 + C, hh * GDN_DV:(hh + 1) * GDN_DV]
        o_ref[bb, n * C:(n + 1) * C, hh * GDN_DV:(hh + 1) * GDN_DV] = _rms(o, gng) * (zh * _sigmoid(zh))


def _wspec(a, l, **kw):
    if a.ndim == 2:
        return pl.BlockSpec(a.shape, lambda *_: (0, 0), **kw)
    return pl.BlockSpec((None,) + a.shape[1:], lambda *_: (l,) + (0,) * (a.ndim - 1), **kw)


def _gdn_call(x, s0, hist, ls, w, l, *, Bb, Tt, C, t_real):
    B, T, D = x.shape
    assert Tt & (Tt - 1) == 0 and C & (C - 1) == 0 and B % Bb == 0 and T % Tt == 0 and Tt % C == 0
    kern = functools.partial(_gdn_kernel, Bb=Bb, Tt=Tt, C=C, t_real=t_real, t_total=T)
    names = ['norm_mix_g', 'gdn_w', 'gdn_wba_t', 'gdn_conv_w', 'gdn_prow', 'gdn_pcol', 'gdn_norm_g', 'gdn_rep']
    return pl.pallas_call(
        kern,
        grid=(B // Bb, T // Tt),
        in_specs=[
            pl.BlockSpec((Bb, Tt, D), lambda b, t: (b, t, 0)),
            pl.BlockSpec((None, Bb, GDN_HEADS, GDN_DK, GDN_DV), lambda b, t: (ls, b, 0, 0, 0)),
            pl.BlockSpec((None, Bb, GDN_CONV - 1, GDN_CONV_CH), lambda b, t: (ls, b, 0, 0)),
        ] + [_wspec(w[n], l) for n in names],
        out_specs=[
            pl.BlockSpec((Bb, Tt, GDN_VAL_W), lambda b, t: (b, t, 0)),
            pl.BlockSpec((Bb, GDN_HEADS, GDN_DK, GDN_DV), lambda b, t: (b, 0, 0, 0)),
            pl.BlockSpec((Bb, GDN_CONV - 1, GDN_CONV_CH), lambda b, t: (b, 0, 0)),
        ],
        out_shape=[
            jax.ShapeDtypeStruct((B, T, GDN_VAL_W), F32),
            jax.ShapeDtypeStruct((B, GDN_HEADS, GDN_DK, GDN_DV), F32),
            jax.ShapeDtypeStruct((B, GDN_CONV - 1, GDN_CONV_CH), F32),
        ],
        scratch_shapes=[pltpu.VMEM((Bb, 2 * PRE_ROW0, GDN_CONV_CH), F32)],
        compiler_params=pltpu.CompilerParams(
            dimension_semantics=("arbitrary", "arbitrary"), vmem_limit_bytes=VMEM_LIMIT),
        name="gdn_mixer",
    )(x, s0, hist, *[w[n] for n in names])


def _rwkv_kernel(x_ref, s0_ref, sh_ref, ng_ref, wr_ref, mu_ref, w0_ref, w2_ref, a0_ref, a2_ref, g2_ref,
                 kk_ref, ka_ref, rk_ref, lng_ref, lnb_ref, e_ref,
                 o_ref, s_ref, sho_ref, pre_scr, y_scr, sp_scr, *, Bb, Tt, C, t_real, t_total):
    ti = pl.program_id(1)
    N = RWKV_HEAD
    R = Bb * Tt

    @pl.when(ti == 0)
    def _():
        pre_scr[:, 0:1, :] = sh_ref[...]

    h = _rms(x_ref[...].reshape(R, D_MODEL), ng_ref[...]).astype(BF16)
    pre = jnp.dot(h, wr_ref[...], preferred_element_type=F32)
    real_rows = min(Tt, t_real)
    rolled = pltpu.roll(pre, 1, axis=0)
    first = lax.broadcasted_iota(jnp.int32, (Tt, 1), 0) == 0
    prevs = []
    for bb in range(Bb):
        prevs.append(jnp.where(first, pre_scr[bb, 0:1, :], rolled[bb * Tt:(bb + 1) * Tt]))
        last = pre[bb * Tt + real_rows - 1:bb * Tt + real_rows]
        pre_scr[bb, 0:1, :] = last
        sho_ref[bb] = last
    prev = prevs[0] if Bb == 1 else jnp.concatenate(prevs, axis=0)

    xm = pre + (prev - pre) * mu_ref[...]
    W = RWKV_W
    rr = xm[:, 0:W]
    kx = xm[:, W:2 * W]
    vv = xm[:, 2 * W:3 * W]
    wlo = xm[:, 3 * W:3 * W + LANES]
    alo = xm[:, 3 * W + LANES:3 * W + 2 * LANES]
    glo = xm[:, 3 * W + 2 * LANES:3 * W + 3 * LANES]
    w_log = -_softplus(-(w0_ref[...] + _bdot(jnp.tanh(wlo), w2_ref[...]))) - 0.5
    logd = -jnp.exp(w_log)
    a = _sigmoid(a0_ref[...] + _bdot(alo, a2_ref[...]))
    gb = _bdot(_sigmoid(glo), g2_ref[...])
    E = e_ref[...]
    kkr = kx * kk_ref[...]
    kk = kkr * lax.rsqrt(_split_dot(kkr * kkr, E, 1) + L2_EPS)
    kb = kx * (1.0 + (a - 1.0) * ka_ref[...])
    a_eff = a
    v_eff = vv
    if t_real < t_total:
        tcol = (lax.broadcasted_iota(jnp.int32, (R, 1), 0) & (Tt - 1)) + ti * Tt
        valid = tcol < t_real
        logd = jnp.where(valid, logd, 0.0)
        a_eff = jnp.where(valid, a, 0.0)
        v_eff = jnp.where(valid, vv, 0.0)

    tri, _ = _chunk_tri(R, C)
    logG = _split_dot_rhs(tri, logd, 3)
    G = jnp.exp(logG)
    Ginv = jnp.exp(-logG)
    aq_all = kk * jnp.exp(logG - logd)
    bk_all = -(a_eff * kk) * Ginv
    kd_all = kb * Ginv
    rq_all = rr * G

    GH = RWKV_GROUP
    GW = GH * N
    HP = RWKV_HEADS // GH
    CG = GH * C
    head_c = lax.broadcasted_iota(jnp.int32, (1, GW), 1) >> (N.bit_length() - 1)
    rt = lax.broadcasted_iota(jnp.int32, (C, CG), 0)
    lt = lax.broadcasted_iota(jnp.int32, (C, CG), 1)
    ct = lt & (C - 1)
    head_t = lt >> (C.bit_length() - 1)
    strict = rt > ct
    incl = rt >= ct
    eye = jnp.where(rt == ct, 1.0, 0.0).astype(F32)
    masks = []
    k = 0
    while (1 << k) < C:
        masks.append(jnp.where(((rt >> k) ^ (ct >> k)) == 1, jnp.where(((rt >> k) & 1) == 1, 1.0, 0.0), 0.0)
                     .astype(F32))
        k += 1
    nshift = N.bit_length() - 1
    same_head = ((lax.broadcasted_iota(jnp.int32, (GW, GW), 0) >> nshift)
                 == (lax.broadcasted_iota(jnp.int32, (GW, GW), 1) >> nshift))

    def stack_c(y):
        return jnp.concatenate([jnp.where(head_c == j, y, 0.0) for j in range(GH)], axis=0)

    def stack_t(t):
        return jnp.concatenate([jnp.where(head_t == j, t, 0.0) for j in range(GH)], axis=0)

    @pl.when(ti == 0)
    def _():
        for bb in range(Bb):
            for pp in range(HP):
                rows = []
                for j in range(GH):
                    blocks = [s0_ref[bb, GH * pp + j] if jj == j else jnp.zeros((N, N), F32) for jj in range(GH)]
                    rows.append(jnp.concatenate(blocks, axis=1))
                sp_scr[bb, pp] = jnp.concatenate(rows, axis=0)

    nchunk = Tt // C
    bodies = [(bb, n, pp) for n in range(nchunk) for bb in range(Bb) for pp in range(HP)]
    nb = len(bodies)
    aqs, bks, kds, rqs, vhs, gls = [], [], [], [], [], []
    for (bb, n, pp) in bodies:
        r0 = bb * Tt + n * C
        lp = slice(pp * GW, (pp + 1) * GW)
        aqs.append(aq_all[r0:r0 + C, lp])
        bks.append(bk_all[r0:r0 + C, lp])
        kds.append(kd_all[r0:r0 + C, lp])
        rqs.append(rq_all[r0:r0 + C, lp])
        vhs.append(v_eff[r0:r0 + C, lp])
        gls.append(G[r0 + C - 1:r0 + C, lp])
    ars = [jnp.concatenate([aqs[i], rqs[i]], axis=0) for i in range(nb)]
    sbs = [_bdot_nt(ars[i], stack_c(bks[i])) for i in range(nb)]
    sks = [_bdot_nt(ars[i], stack_c(kds[i])) for i in range(nb)]
    Ls = [jnp.where(strict, -sbs[i][:C], 0.0) for i in range(nb)]
    ras = [jnp.where(incl, sbs[i][C:], 0.0) for i in range(nb)]
    rks = [jnp.where(incl, sks[i][C:], 0.0) for i in range(nb)]
    vstk = [stack_c(vhs[i]) for i in range(nb)]
    bmvs = [_bdot(jnp.where(strict, sks[i][:C], 0.0), vstk[i]) for i in range(nb)]
    Ts = [eye - L * masks[0] for L in Ls]
    for m in masks[1:]:
        tmp = [_bdot(Ls[i] * m, stack_t(Ts[i])) for i in range(nb)]
        Ts = [Ts[i] - _bdot(Ts[i], stack_t(tmp[i])) for i in range(nb)]
    TAs = [_bdot(Ts[i], stack_c(aqs[i])) for i in range(nb)]
    TBVs = [_bdot(Ts[i], stack_c(bmvs[i])) for i in range(nb)]
    Xs = [bks[i] * gls[i] for i in range(nb)]
    Zs = [kds[i] * gls[i] for i in range(nb)]
    Q1s = [jnp.where(same_head, _bdot(TAs[i].T, Xs[i]), 0.0) for i in range(nb)]
    M0s = [jnp.where(same_head,
                     _bdot(jnp.concatenate([TBVs[i], vhs[i]], axis=0).T, jnp.concatenate([Xs[i], Zs[i]], axis=0)),
                     0.0) for i in range(nb)]
    rqps = [rqs[i] + _bdot(ras[i], stack_c(TAs[i])) for i in range(nb)]
    y0s = [_bdot(ras[i], stack_c(TBVs[i])) + _bdot(rks[i], vstk[i]) for i in range(nb)]

    per_chunk = Bb * HP
    for n in range(nchunk):
        idx = range(n * per_chunk, (n + 1) * per_chunk)
        Ss = {i: sp_scr[bodies[i][0], bodies[i][2]] for i in idx}
        for i in idx:
            bb, _, pp = bodies[i]
            r0 = bb * Tt + n * C
            y_scr[r0:r0 + C, pp * GW:(pp + 1) * GW] = _bdot_nt(rqps[i], Ss[i]) + y0s[i]
        for i in idx:
            bb, _, pp = bodies[i]
            sp_scr[bb, pp] = Ss[i] * gls[i] + _bdot(Ss[i], Q1s[i]) + M0s[i]

    @pl.when(ti == pl.num_programs(1) - 1)
    def _():
        for bb in range(Bb):
            for pp in range(HP):
                sp = sp_scr[bb, pp]
                for j in range(GH):
                    s_ref[bb, GH * pp + j] = sp[j * N:(j + 1) * N, j * N:(j + 1) * N]

    y = y_scr[...]
    inv_n = 1.0 / N
    mu = _split_dot(y, E, 2) * inv_n
    yc = y - mu
    var = _split_dot(yc * yc, E, 1) * inv_n
    yn = yc * lax.rsqrt(var + RWKV_LN_EPS) * lng_ref[...] + lnb_ref[...]
    bonus = _split_dot(rr * kb * rk_ref[...], E, 1) * vv
    o_ref[...] = ((yn + bonus) * gb).reshape(Bb, Tt, RWKV_W)


def _rwkv_call(x, s0, shift, ls, w, l, *, Bb, Tt, C, t_real):
    B, T, D = x.shape
    assert Tt & (Tt - 1) == 0 and C & (C - 1) == 0 and B % Bb == 0 and T % Tt == 0 and Tt % C == 0
    kern = functools.partial(_rwkv_kernel, Bb=Bb, Tt=Tt, C=C, t_real=t_real, t_total=T)
    names = ['norm_mix_g', 'rwkv_w', 'rwkv_mu', 'rwkv_w0', 'rwkv_w2', 'rwkv_a0', 'rwkv_a2', 'rwkv_g2',
             'rwkv_k_k', 'rwkv_k_a', 'rwkv_r_k', 'rwkv_ln_g', 'rwkv_ln_b', 'rwkv_e']
    return pl.pallas_call(
        kern,
        grid=(B // Bb, T // Tt),
        in_specs=[
            pl.BlockSpec((Bb, Tt, D), lambda b, t: (b, t, 0)),
            pl.BlockSpec((None, Bb, RWKV_HEADS, RWKV_HEAD, RWKV_HEAD), lambda b, t: (ls, b, 0, 0, 0)),
            pl.BlockSpec((None, Bb, 1, RW_PAD), lambda b, t: (ls, b, 0, 0)),
        ] + [_wspec(w[n], l) for n in names],
        out_specs=[
            pl.BlockSpec((Bb, Tt, RWKV_W), lambda b, t: (b, t, 0)),
            pl.BlockSpec((Bb, RWKV_HEADS, RWKV_HEAD, RWKV_HEAD), lambda b, t: (b, 0, 0, 0)),
            pl.BlockSpec((Bb, 1, RW_PAD), lambda b, t: (b, 0, 0)),
        ],
        out_shape=[
            jax.ShapeDtypeStruct((B, T, RWKV_W), F32),
            jax.ShapeDtypeStruct((B, RWKV_HEADS, RWKV_HEAD, RWKV_HEAD), F32),
            jax.ShapeDtypeStruct((B, 1, RW_PAD), F32),
        ],
        scratch_shapes=[pltpu.VMEM((Bb, SUBLANES, RW_PAD), F32), pltpu.VMEM((Bb * Tt, RWKV_W), F32),
                        pltpu.VMEM((Bb, RWKV_HEADS // RWKV_GROUP, RWKV_GROUP * RWKV_HEAD, RWKV_GROUP * RWKV_HEAD),
                                   F32)],
        compiler_params=pltpu.CompilerParams(
            dimension_semantics=("arbitrary", "arbitrary"), vmem_limit_bytes=VMEM_LIMIT),
        name="rwkv_mixer",
    )(x, s0, shift, *[w[n] for n in names])


def _mla_proj_kernel(x_ref, ng_ref, wm_ref, qg_ref, kvg_ref, wq_ref, cos_ref, sin_ref, *rest, prompt):
    if prompt:
        wuk_ref, wuv_ref, ckv_ref, kr_ref, q_ref, k_ref, v_ref = rest
    else:
        ckv_ref, kr_ref, qn_ref, qr_ref = rest
    h = _rms(x_ref[...], ng_ref[...]).astype(BF16)
    p = jnp.dot(h, wm_ref[...], preferred_element_type=F32)
    cqn = _rms(p[:, :MLA_Q_RANK], qg_ref[...]).astype(BF16)
    ckv = _rms(p[:, MLA_Q_RANK:MLA_Q_RANK + MLA_KV_RANK], kvg_ref[...])
    o = MLA_Q_RANK + MLA_KV_RANK
    cos = cos_ref[...]
    sin = sin_ref[...]
    krp = p[:, o:o + LANES] * cos + p[:, o + LANES:o + 2 * LANES] * sin
    ckv_ref[...] = ckv
    kr_ref[...] = krp[:, :MLA_ROPE]
    q = jnp.dot(cqn, wq_ref[...], preferred_element_type=F32)
    wn = MLA_HEADS * MLA_NOPE
    wr = MLA_HEADS * LANES
    cos4 = jnp.concatenate([cos] * MLA_HEADS, axis=1)
    sin4 = jnp.concatenate([sin] * MLA_HEADS, axis=1)
    qn = q[:, :wn] * MLA_SCALE
    qr = (q[:, wn:wn + wr] * cos4 + q[:, wn + wr:wn + 2 * wr] * sin4) * MLA_SCALE
    if prompt:
        cb = ckv.astype(BF16)
        kn = jnp.dot(cb, wuk_ref[...], preferred_element_type=F32)
        qparts, kparts = [], []
        for hh in range(MLA_HEADS):
            qparts += [qn[:, hh * MLA_NOPE:(hh + 1) * MLA_NOPE], qr[:, hh * LANES:(hh + 1) * LANES]]
            kparts += [kn[:, hh * MLA_NOPE:(hh + 1) * MLA_NOPE], krp]
        q_ref[...] = jnp.concatenate(qparts, axis=1).astype(BF16)
        k_ref[...] = jnp.concatenate(kparts, axis=1).astype(BF16)
        v_ref[...] = lax.dot_general(wuv_ref[...], cb, (((1,), (1,)), ((), ())),
                                     preferred_element_type=F32).astype(BF16)
    else:
        qn_ref[...] = qn
        qr_ref[...] = qr


def _mla_proj_call(x2, w, l, cos, sin, *, tr, prompt):
    M, D = x2.shape
    nt = cos.shape[0] // tr
    full = lambda a: _wspec(a, l)
    tab = pl.BlockSpec((tr, LANES), lambda i: (i % nt, 0))
    tok = lambda n: pl.BlockSpec((tr, n), lambda i: (i, 0))
    ins = [x2, w['norm_mix_g'], w['mla_w'], w['mla_q_norm_g'], w['mla_kv_norm_g'], w['mla_wq'], cos, sin]
    in_specs = [tok(D)] + [full(a) for a in ins[1:6]] + [tab, tab]
    out_specs = [tok(MLA_KV_RANK), tok(MLA_ROPE)]
    out_shape = [jax.ShapeDtypeStruct((M, MLA_KV_RANK), F32), jax.ShapeDtypeStruct((M, MLA_ROPE), F32)]
    if prompt:
        T = cos.shape[0]
        ins += [w['mla_wuk'], w['mla_wuv_t']]
        in_specs += [full(w['mla_wuk']), full(w['mla_wuv_t'])]
        out_specs += [tok(MLA_HEADS * QK_HEAD), tok(MLA_HEADS * QK_HEAD),
                      pl.BlockSpec((None, MLA_HEADS * MLA_V, tr), lambda i: (i // nt, 0, i % nt))]
        out_shape += [jax.ShapeDtypeStruct((M, MLA_HEADS * QK_HEAD), BF16),
                      jax.ShapeDtypeStruct((M, MLA_HEADS * QK_HEAD), BF16),
                      jax.ShapeDtypeStruct((M // T, MLA_HEADS * MLA_V, T), BF16)]
    else:
        out_specs += [tok(MLA_HEADS * MLA_NOPE), tok(MLA_HEADS * LANES)]
        out_shape += [jax.ShapeDtypeStruct((M, MLA_HEADS * MLA_NOPE), F32),
                      jax.ShapeDtypeStruct((M, MLA_HEADS * LANES), F32)]
    return pl.pallas_call(
        functools.partial(_mla_proj_kernel, prompt=prompt),
        grid=(M // tr,),
        in_specs=in_specs, out_specs=out_specs, out_shape=out_shape,
        compiler_params=pltpu.CompilerParams(dimension_semantics=("arbitrary",), vmem_limit_bytes=VMEM_LIMIT),
        name="mla_proj_prompt" if prompt else "mla_proj_sample",
    )(*ins)


def _flash_kernel(q_ref, k_ref, vt_ref, o_ref, m_scr, l_scr, acc_scr, *, tq):
    qi = pl.program_id(1)
    ki = pl.program_id(2)

    @pl.when(ki == 0)
    def _():
        m_scr[...] = jnp.full(m_scr.shape, -jnp.inf, F32)
        l_scr[...] = jnp.zeros(l_scr.shape, F32)
        acc_scr[...] = jnp.zeros(acc_scr.shape, F32)

    def step(diagonal):
        if diagonal:
            kpos = lax.broadcasted_iota(jnp.int32, (tq, tq), 0)
            qpos = lax.broadcasted_iota(jnp.int32, (tq, tq), 1)
            keep = kpos <= qpos
        sts = []
        for hh in range(MLA_HEADS):
            lq = slice(hh * QK_HEAD, (hh + 1) * QK_HEAD)
            sts.append(lax.dot_general(k_ref[0, :, lq], q_ref[0, :, lq], (((1,), (1,)), ((), ())),
                                       preferred_element_type=F32))
        for hh in range(MLA_HEADS):
            rv = slice(hh * MLA_V, (hh + 1) * MLA_V)
            st = sts[hh]
            if diagonal:
                st = jnp.where(keep, st, -jnp.inf)
            m_old = m_scr[hh:hh + 1, :]
            m_new = jnp.maximum(m_old, jnp.max(st, axis=0, keepdims=True))
            alpha = jnp.exp(m_old - m_new)
            p = jnp.exp(st - m_new)
            l_scr[hh:hh + 1, :] = alpha * l_scr[hh:hh + 1, :] + jnp.sum(p, axis=0, keepdims=True)
            acc_scr[rv, :] = alpha * acc_scr[rv, :] + jnp.dot(vt_ref[0, rv, :], p.astype(BF16),
                                                              preferred_element_type=F32)
            m_scr[hh:hh + 1, :] = m_new

    @pl.when(ki < qi)
    def _():
        step(False)

    @pl.when(ki == qi)
    def _():
        step(True)
        for hh in range(MLA_HEADS):
            rv = slice(hh * MLA_V, (hh + 1) * MLA_V)
            o_ref[0, rv, :] = acc_scr[rv, :] / l_scr[hh:hh + 1, :]


def _flash_call(q, k, vt, *, tq):
    B, T, _ = q.shape
    nq = T // tq
    hv = vt.shape[1]
    return pl.pallas_call(
        functools.partial(_flash_kernel, tq=tq),
        grid=(B, nq, nq),
        in_specs=[pl.BlockSpec((1, tq, q.shape[2]), lambda b, i, j: (b, i, 0)),
                  pl.BlockSpec((1, tq, k.shape[2]), lambda b, i, j: (b, jnp.minimum(i, j), 0)),
                  pl.BlockSpec((1, hv, tq), lambda b, i, j: (b, 0, jnp.minimum(i, j)))],
        out_specs=pl.BlockSpec((1, hv, tq), lambda b, i, j: (b, 0, i)),
        out_shape=jax.ShapeDtypeStruct((B, hv, T), F32),
        scratch_shapes=[pltpu.VMEM((SUBLANES, tq), F32), pltpu.VMEM((SUBLANES, tq), F32),
                        pltpu.VMEM((hv, tq), F32)],
        compiler_params=pltpu.CompilerParams(
            dimension_semantics=("arbitrary", "arbitrary", "arbitrary"), vmem_limit_bytes=VMEM_LIMIT),
        name="mla_flash",
    )(q, k, vt)


def _decode_kernel(pt_ref, qn_ref, qr_ref, ckvn_ref, krn_ref, wuk_ref, wuv_ref, ckv_hbm, krt_hbm, o_ref,
                   bufc, bufk, sem, *, G, NJ, BB, layer, nsteps):
    s = pl.program_id(0)
    total = nsteps * NJ
    ahead = DECODE_SLOTS - 1
    nj_shift = NJ.bit_length() - 1
    page = bufc.shape[3]
    rid = lax.broadcasted_iota(jnp.int32, (SUBLANES, 1), 0)

    def page_copies(t, bi, i, slot):
        pg = pt_ref[(t >> nj_shift) * BB + bi, (t & (NJ - 1)) * G + i]
        return (pltpu.make_async_copy(ckv_hbm.at[layer, pg], bufc.at[bi, slot, i], sem.at[0, slot]),
                pltpu.make_async_copy(krt_hbm.at[layer, pg], bufk.at[bi, slot, i], sem.at[1, slot]))

    @pl.when(s == 0)
    def _():
        for t0 in range(ahead):
            for bi in range(BB):
                for i in range(G):
                    for c in page_copies(t0, bi, i, t0):
                        c.start(priority=i % 2)

    qlats, qrms = [], []
    for bi in range(BB):
        qn = qn_ref[bi]
        qr = qr_ref[bi]
        qlat = jnp.zeros((SUBLANES, MLA_KV_RANK), F32)
        qrm = jnp.zeros((SUBLANES, MLA_ROPE), F32)
        for hh in range(MLA_HEADS):
            ql = _bdot_nt(qn[:, hh * MLA_NOPE:(hh + 1) * MLA_NOPE], wuk_ref[hh])
            qlat = jnp.where(rid == hh, ql[0:1, :], qlat)
            qrm = jnp.where(rid == hh, qr[0:1, hh * LANES:hh * LANES + MLA_ROPE], qrm)
        qlats.append(qlat)
        qrms.append(qrm)

    def body(j, carry):
        t = s * NJ + j
        slot = lax.rem(t, DECODE_SLOTS)
        t_next = jnp.minimum(t + ahead, total - 1)
        slot_next = lax.rem(t + ahead, DECODE_SLOTS)
        for bi in range(BB):
            for i in range(G):
                for c in page_copies(t, bi, i, slot):
                    c.wait()
        scores = [[] for _ in range(BB)]
        for i in range(G):
            for bi in range(BB):
                sc = (_bdot_nt(qlats[bi], bufc[bi, slot, i]) + _bdot(qrms[bi], bufk[bi, slot, i]))
                scores[bi].append(sc)
                for c in page_copies(t_next, bi, i, slot_next):
                    c.start(priority=i % 2)
        out = []
        for bi in range(BB):
            m_old, l_old, acc_old = carry[bi]
            sc = jnp.concatenate(scores[bi], axis=1)
            m_new = jnp.maximum(m_old, jnp.max(sc, axis=-1, keepdims=True))
            alpha = jnp.exp(m_old - m_new)
            p = jnp.exp(sc - m_new)
            l_new = alpha * l_old + jnp.sum(p, axis=-1, keepdims=True)
            acc = alpha * acc_old
            for i in range(G):
                acc = acc + _bdot(p[:, i * page:(i + 1) * page], bufc[bi, slot, i])
            out.append((m_new, l_new, acc))
        return tuple(out)

    init = tuple((jnp.full((SUBLANES, 1), -jnp.inf, F32), jnp.zeros((SUBLANES, 1), F32),
                  jnp.zeros((SUBLANES, MLA_KV_RANK), F32)) for _ in range(BB))
    final = lax.fori_loop(0, NJ, body, init)

    @pl.when(s == nsteps - 1)
    def _():
        for extra in range(ahead):
            for bi in range(BB):
                for i in range(G):
                    for c in page_copies(total - 1, bi, i, (total + extra) % DECODE_SLOTS):
                        c.wait()

    for bi in range(BB):
        m_new, l_new, acc = final[bi]
        ckvn = ckvn_ref[bi][0:1, :]
        krn = krn_ref[bi][0:1, :]
        s_new = (jnp.sum(qlats[bi] * ckvn, axis=-1, keepdims=True)
                 + jnp.sum(qrms[bi] * krn, axis=-1, keepdims=True))
        m_fin = jnp.maximum(m_new, s_new)
        a2 = jnp.exp(m_new - m_fin)
        p_new = jnp.exp(s_new - m_fin)
        l_fin = a2 * l_new + p_new
        o_lat = (a2 * acc + p_new * ckvn) / l_fin
        outs = []
        for hh in range(MLA_HEADS):
            oh = _bdot(o_lat, wuv_ref[hh])
            outs.append(oh[hh:hh + 1, :])
        o_ref[bi] = jnp.broadcast_to(jnp.concatenate(outs, axis=1), (SUBLANES, MLA_HEADS * MLA_V))


def _decode_call(page_table, qn, qr, ckvn, krn, cache_ckv, cache_krope_t, layer, w):
    B = qn.shape[0]
    n_pages = page_table.shape[1]
    page = cache_ckv.shape[2]
    G = _pick_tile(n_pages // 2, DECODE_PAGES)
    NJ = n_pages // G
    BB = _pick_tile(B, DECODE_SEQS)
    assert NJ >= 2 and NJ & (NJ - 1) == 0
    nsteps = B // BB
    tok = lambda a: pl.BlockSpec((BB,) + a.shape[1:], lambda b, pt: (b, 0, 0))
    full = lambda a: _wspec(a, layer)
    hbm = pl.BlockSpec(memory_space=pl.ANY)
    grid_spec = pltpu.PrefetchScalarGridSpec(
        num_scalar_prefetch=1,
        grid=(nsteps,),
        in_specs=[tok(qn), tok(qr), tok(ckvn), tok(krn), full(w['mla_wuk_h']), full(w['mla_wuv_h']), hbm, hbm],
        out_specs=pl.BlockSpec((BB, SUBLANES, MLA_HEADS * MLA_V), lambda b, pt: (b, 0, 0)),
        scratch_shapes=[pltpu.VMEM((BB, DECODE_SLOTS, G, page, MLA_KV_RANK), F32),
                        pltpu.VMEM((BB, DECODE_SLOTS, G, MLA_ROPE, page), F32),
                        pltpu.SemaphoreType.DMA((2, DECODE_SLOTS))],
    )
    return pl.pallas_call(
        functools.partial(_decode_kernel, G=G, NJ=NJ, BB=BB, layer=layer, nsteps=nsteps),
        grid_spec=grid_spec,
        out_shape=jax.ShapeDtypeStruct((B, SUBLANES, MLA_HEADS * MLA_V), F32),
        compiler_params=pltpu.CompilerParams(
            dimension_semantics=("arbitrary",), vmem_limit_bytes=VMEM_LIMIT),
        name="mla_decode",
    )(page_table, qn, qr, ckvn, krn, w['mla_wuk_h'], w['mla_wuv_h'], cache_ckv, cache_krope_t)


def _merge_ffn_kernel(x_ref, oa_ref, ob_ref, oc_ref, ng_ref, wgt_ref, bg_ref, wa_ref, wb_ref, wc_ref, wo_ref,
                      nf_ref, wup_ref, wdn_ref, fin_ref, y_ref, *, final, oc_transposed):
    x = x_ref[...]
    h = _rms(x, ng_ref[...]).astype(BF16)
    gates = _sigmoid(jnp.dot(h, wgt_ref[...], preferred_element_type=F32) + bg_ref[...])
    D = D_MODEL
    oc = oc_ref[...].T if oc_transposed else oc_ref[...]
    merged = (gates[:, 0:D] * _bdot(oa_ref[...], wa_ref[...])
              + gates[:, D:2 * D] * _bdot(ob_ref[...], wb_ref[...])
              + gates[:, 2 * D:3 * D] * _bdot(oc, wc_ref[...]))
    x1 = x + _bdot(merged, wo_ref[...])
    h2 = _rms(x1, nf_ref[...]).astype(BF16)
    up = jnp.dot(h2, wup_ref[...], preferred_element_type=F32)
    g = up[:, :D_FF]
    x2 = x1 + _bdot(g * _sigmoid(g) * up[:, D_FF:], wdn_ref[...])
    y_ref[...] = _rms(x2, fin_ref[...]) if final else x2


def _merge_ffn_call(x, oa, ob, oc, w, l, fin_g, *, tm, final):
    M, D = x.shape
    const = lambda a: _wspec(a, l, pipeline_mode=pl.Buffered(1))
    row = lambda n: pl.BlockSpec((tm, n), lambda i: (i, 0))
    names = ['norm_mix_g', 'w_gate', 'b_gate', 'w_br_a', 'w_br_b', 'w_br_c', 'w_out', 'norm_ffn_g',
             'w_ffn_up', 'w_ffn_down']
    oc_transposed = oc.ndim == 3
    if oc_transposed:
        nt = oc.shape[2] // tm
        oc_spec = pl.BlockSpec((None, oc.shape[1], tm), lambda i: (i // nt, 0, i % nt))
    else:
        oc_spec = row(oc.shape[1])
    return pl.pallas_call(
        functools.partial(_merge_ffn_kernel, final=final, oc_transposed=oc_transposed),
        grid=(M // tm,),
        in_specs=[row(D), row(oa.shape[1]), row(ob.shape[1]), oc_spec]
                 + [const(w[n]) for n in names] + [const(fin_g)],
        out_specs=row(D),
        out_shape=jax.ShapeDtypeStruct((M, D), F32),
        compiler_params=pltpu.CompilerParams(
            dimension_semantics=("arbitrary",), vmem_limit_bytes=VMEM_LIMIT),
        name="merge_ffn",
    )(x, oa, ob, oc, *[w[n] for n in names], fin_g)


def _pad_cols(a, n):
    return jnp.pad(a, [(0, 0)] * (a.ndim - 1) + [(0, n - a.shape[-1])])


def _swap_halves(a, width):
    shp = a.shape
    a = a.reshape(shp[:-1] + (shp[-1] // width, 2, width // 2))
    return a[..., ::-1, :].reshape(shp)


def _pad_groups(a, width, to):
    shp = a.shape
    a = a.reshape(shp[:-1] + (shp[-1] // width, width))
    a = jnp.pad(a, [(0, 0)] * (a.ndim - 1) + [(0, to - width)])
    return a.reshape(shp[:-1] + (-1,))


def _rw_pad(a):
    W = RWKV_W
    z = jnp.zeros(a.shape[:-1] + (LANES - RWKV_DECAY_LORA,), a.dtype)
    return jnp.concatenate([a[..., :3 * W], a[..., 3 * W:3 * W + RWKV_DECAY_LORA], z,
                            a[..., 3 * W + RWKV_DECAY_LORA:3 * W + RWKV_DECAY_LORA + RWKV_AAA_LORA], z,
                            a[..., 3 * W + RWKV_DECAY_LORA + RWKV_AAA_LORA:]], axis=-1)


def _rw_unpad(a):
    W = RWKV_W
    return jnp.concatenate([a[..., :3 * W], a[..., 3 * W:3 * W + RWKV_DECAY_LORA],
                            a[..., 3 * W + LANES:3 * W + LANES + RWKV_AAA_LORA],
                            a[..., 3 * W + 2 * LANES:]], axis=-1)


def _prep_weights(p):
    w_in = p['w_in']
    depth = w_in.shape[0]
    row = lambda a: a.reshape(depth, 1, -1).astype(F32)
    w = {}
    w['norm_mix_g'] = row(p['norm_mix_g'])
    w['norm_ffn_g'] = row(p['norm_ffn_g'])
    ba = w_in[..., _O_BETA:_O_Z]
    w['gdn_w'] = jnp.concatenate([w_in[..., _O_QKV:_O_BETA], w_in[..., _O_Z:_O_RW], _pad_cols(ba, LANES)],
                                 axis=-1).astype(BF16)
    w['gdn_wba_t'] = jnp.swapaxes(ba, 1, 2).astype(BF16)
    w['gdn_conv_w'] = p['gdn_conv_w']
    zero4 = jnp.zeros((depth, GDN_HEADS), F32)
    a_log = jnp.concatenate([zero4, p['gdn_a_log']], axis=1)
    dt_b = jnp.concatenate([zero4, p['gdn_dt_bias']], axis=1)
    w['gdn_prow'] = _pad_cols(jnp.stack([a_log, dt_b], axis=1), LANES)
    w['gdn_pcol'] = jnp.stack([a_log, dt_b], axis=2)
    w['gdn_norm_g'] = row(p['gdn_norm_g'])
    lane = jnp.arange(LANES)[:, None]
    head_of_col = jnp.arange(GDN_VAL_W)[None, :] // GDN_DV
    w['gdn_rep'] = jnp.concatenate([lane == head_of_col, lane == head_of_col + GDN_HEADS], axis=1).astype(BF16)
    w['rwkv_w'] = _rw_pad(w_in[..., _O_RW:_O_CQ]).astype(BF16)
    w['rwkv_mu'] = row(_rw_pad(p['rwkv_mu']))
    w['rwkv_w0'] = row(p['rwkv_w0'])
    w['rwkv_w2'] = jnp.pad(p['rwkv_w2'], ((0, 0), (0, LANES - RWKV_DECAY_LORA), (0, 0))).astype(BF16)
    w['rwkv_a0'] = row(p['rwkv_a0'])
    w['rwkv_a2'] = jnp.pad(p['rwkv_a2'], ((0, 0), (0, LANES - RWKV_AAA_LORA), (0, 0))).astype(BF16)
    w['rwkv_g2'] = p['rwkv_g2'].astype(BF16)
    w['rwkv_k_k'] = row(p['rwkv_k_k'])
    w['rwkv_k_a'] = row(p['rwkv_k_a'])
    w['rwkv_r_k'] = row(p['rwkv_r_k'])
    w['rwkv_ln_g'] = row(p['rwkv_ln_g'])
    w['rwkv_ln_b'] = row(p['rwkv_ln_b'])
    hid = jnp.arange(RWKV_W) // RWKV_HEAD
    w['rwkv_e'] = (hid[:, None] == hid[None, :]).astype(BF16)
    w_kr = w_in[..., _O_KR:_O_GATE]
    w['mla_w'] = jnp.concatenate([w_in[..., _O_CQ:_O_KR], _pad_cols(w_kr, LANES),
                                  _pad_cols(_swap_halves(w_kr, MLA_ROPE), LANES)], axis=-1).astype(BF16)
    w['mla_q_norm_g'] = row(p['mla_q_norm_g'])
    w['mla_kv_norm_g'] = row(p['mla_kv_norm_g'])
    w_uq = p['mla_w_uq']
    wq_n = w_uq[..., :MLA_NOPE].reshape(depth, MLA_Q_RANK, -1)
    wq_r = w_uq[..., MLA_NOPE:].reshape(depth, MLA_Q_RANK, -1)
    w['mla_wq'] = jnp.concatenate([wq_n, _pad_groups(wq_r, MLA_ROPE, LANES),
                                   _pad_groups(_swap_halves(wq_r, MLA_ROPE), MLA_ROPE, LANES)],
                                  axis=-1).astype(BF16)
    w['mla_wuk'] = p['mla_w_uk'].reshape(depth, MLA_KV_RANK, -1).astype(BF16)
    w['mla_wuv_t'] = jnp.swapaxes(p['mla_w_uv'].reshape(depth, MLA_KV_RANK, -1), 1, 2).astype(BF16)
    w['mla_wuk_h'] = jnp.transpose(p['mla_w_uk'], (0, 2, 1, 3)).astype(BF16)
    w['mla_wuv_h'] = jnp.transpose(p['mla_w_uv'], (0, 2, 1, 3)).astype(BF16)
    w['w_gate'] = w_in[..., _O_GATE:].astype(BF16)
    w['b_gate'] = row(p['b_gate'])
    for n in ('w_br_a', 'w_br_b', 'w_br_c', 'w_out', 'w_ffn_up', 'w_ffn_down'):
        w[n] = p[n].astype(BF16)
    return w


def _rope_tables(pos):
    half = MLA_ROPE // 2
    freq = ROPE_THETA ** (-jnp.arange(half, dtype=F32) / half)
    ang = pos.astype(F32)[:, None] * freq
    cos = jnp.cos(ang)
    sin = jnp.sin(ang)
    return (_pad_cols(jnp.concatenate([cos, cos], axis=1), LANES),
            _pad_cols(jnp.concatenate([-sin, sin], axis=1), LANES))


def _pick_tile(T, pref):
    t = min(T, pref)
    while T % t:
        t //= 2
    return t


def kernel(x_prompt, x_sample, cache_ckv, cache_krope, page_table, state_gdn, state_gdn_conv, state_rwkv, state_rwkv_shift, norm_mix_g, norm_ffn_g, norm_final_g, w_in, b_gate, gdn_conv_w, gdn_a_log, gdn_dt_bias, gdn_norm_g, rwkv_mu, rwkv_w0, rwkv_w2, rwkv_a0, rwkv_a2, rwkv_g2, rwkv_k_k, rwkv_k_a, rwkv_r_k, rwkv_ln_g, rwkv_ln_b, mla_q_norm_g, mla_kv_norm_g, mla_w_uq, mla_w_uk, mla_w_uv, w_br_a, w_br_b, w_br_c, w_out, w_ffn_up, w_ffn_down):
    p = dict(norm_mix_g=norm_mix_g, norm_ffn_g=norm_ffn_g, w_in=w_in, b_gate=b_gate, gdn_conv_w=gdn_conv_w,
             gdn_a_log=gdn_a_log, gdn_dt_bias=gdn_dt_bias, gdn_norm_g=gdn_norm_g, rwkv_mu=rwkv_mu,
             rwkv_w0=rwkv_w0, rwkv_w2=rwkv_w2, rwkv_a0=rwkv_a0, rwkv_a2=rwkv_a2, rwkv_g2=rwkv_g2,
             rwkv_k_k=rwkv_k_k, rwkv_k_a=rwkv_k_a, rwkv_r_k=rwkv_r_k, rwkv_ln_g=rwkv_ln_g,
             rwkv_ln_b=rwkv_ln_b, mla_q_norm_g=mla_q_norm_g, mla_kv_norm_g=mla_kv_norm_g,
             mla_w_uq=mla_w_uq, mla_w_uk=mla_w_uk, mla_w_uv=mla_w_uv, w_br_a=w_br_a, w_br_b=w_br_b,
             w_br_c=w_br_c, w_out=w_out, w_ffn_up=w_ffn_up, w_ffn_down=w_ffn_down)
    depth = w_in.shape[0]
    bp, sp, D = x_prompt.shape
    bs, ss, _ = x_sample.shape
    assert ss == 1, "the sample group decodes one new token per sequence"
    past_len = page_table.shape[1] * cache_ckv.shape[2]
    fin_g = norm_final_g.reshape(1, -1).astype(F32)
    cache_krope_t = jnp.swapaxes(cache_krope, 2, 3)

    tp = _pick_tile(sp, MIXER_TILE)
    cp = _pick_tile(tp, MIXER_CHUNK)
    tq = _pick_tile(sp, ROW_TILE)
    ts = SUBLANES
    bb_s = _pick_tile(bs, MIXER_SEQS_SAMPLE)
    bb_p = _pick_tile(bp, MIXER_SEQS_PROMPT)
    cos_p, sin_p = _rope_tables(jnp.arange(sp, dtype=jnp.int32))
    cos_s, sin_s = _rope_tables(jnp.full((bs * ts,), past_len, dtype=jnp.int32))
    zeros_p = dict(
        gdn=jnp.zeros((1, bp, GDN_HEADS, GDN_DK, GDN_DV), F32),
        conv=jnp.zeros((1, bp, GDN_CONV - 1, GDN_CONV_CH), F32),
        rwkv=jnp.zeros((1, bp, RWKV_HEADS, RWKV_HEAD, RWKV_HEAD), F32),
        shift=jnp.zeros((1, bp, 1, RW_PAD), F32))
    shift_s = _rw_pad(state_rwkv_shift)[:, :, None]

    w = _prep_weights(p)
    xp = x_prompt
    xs = jnp.pad(x_sample, ((0, 0), (0, ts - ss), (0, 0)))
    new_p = [[] for _ in range(6)]
    new_s = [[] for _ in range(6)]
    mp = bp * sp
    tm_p = tq
    for l in range(depth):
        final = l == depth - 1
        oa, s_g, cbuf = _gdn_call(xp, zeros_p['gdn'], zeros_p['conv'], 0, w, l, Bb=bb_p, Tt=tp, C=cp, t_real=sp)
        ob, s_r, sh = _rwkv_call(xp, zeros_p['rwkv'], zeros_p['shift'], 0, w, l, Bb=bb_p, Tt=tp, C=cp, t_real=sp)
        ckv, kr, q, k, vt = _mla_proj_call(xp.reshape(mp, D), w, l, cos_p, sin_p, tr=tq, prompt=True)
        oc_t = _flash_call(q.reshape(bp, sp, -1), k.reshape(bp, sp, -1), vt, tq=tq)
        for lst, arr in zip(new_p, (ckv.reshape(bp, sp, -1), kr.reshape(bp, sp, -1), s_g, cbuf, s_r, sh[:, 0])):
            lst.append(arr)
        xp = _merge_ffn_call(xp.reshape(mp, D), oa.reshape(mp, -1), ob.reshape(mp, -1),
                             oc_t, w, l, fin_g, tm=tm_p, final=final).reshape(bp, sp, D)
        oa, s_g, cbuf = _gdn_call(xs, state_gdn, state_gdn_conv, l, w, l, Bb=bb_s, Tt=ts, C=ts, t_real=ss)
        ob, s_r, sh = _rwkv_call(xs, state_rwkv, shift_s, l, w, l, Bb=bb_s, Tt=ts, C=ts, t_real=ss)
        ckv, kr, qn, qr = _mla_proj_call(xs.reshape(bs * ts, D), w, l, cos_s, sin_s, tr=bs * ts, prompt=False)
        ckv = ckv.reshape(bs, ts, -1)
        kr = kr.reshape(bs, ts, -1)
        oc = _decode_call(page_table, qn.reshape(bs, ts, -1), qr.reshape(bs, ts, -1), ckv, kr,
                          cache_ckv, cache_krope_t, l, w)
        for lst, arr in zip(new_s, (ckv[:, :ss], kr[:, :ss], s_g, cbuf, s_r, sh[:, 0])):
            lst.append(arr)
        xs_real = _merge_ffn_call(xs[:, 0], oa[:, 0], ob[:, 0], oc[:, 0], w, l, fin_g, tm=bs, final=final)
        xs = jnp.pad(xs_real[:, None], ((0, 0), (0, ts - ss), (0, 0)))
    y_prompt = xp
    y_sample = xs[:, :ss]
    outs_p = [jnp.stack(a) for a in new_p]
    outs_s = [jnp.stack(a) for a in new_s]
    outs_p[5] = _rw_unpad(outs_p[5])
    outs_s[5] = _rw_unpad(outs_s[5])
    return (y_prompt, y_sample, *outs_p, *outs_s)
```

```python
import functools
import math

import jax
import jax.numpy as jnp
from jax import lax
from jax.experimental import pallas as pl
from jax.experimental.pallas import tpu as pltpu

F32 = jnp.float32
BF16 = jnp.bfloat16

D_MODEL = 1024
GDN_HEADS = 4
GDN_DK = 128
GDN_DV = 128
GDN_CONV = 4
GDN_KEY_W = GDN_HEADS * GDN_DK
GDN_VAL_W = GDN_HEADS * GDN_DV
GDN_CONV_CH = 2 * GDN_KEY_W + GDN_VAL_W
RWKV_HEADS = 8
RWKV_HEAD = 64
RWKV_W = RWKV_HEADS * RWKV_HEAD
RWKV_DECAY_LORA = 64
RWKV_AAA_LORA = 64
RWKV_GATE_LORA = 128
RWKV_COLS = 3 * RWKV_W + RWKV_DECAY_LORA + RWKV_AAA_LORA + RWKV_GATE_LORA
RWKV_LN_EPS = 64e-5
MLA_HEADS = 4
MLA_Q_RANK = 256
MLA_KV_RANK = 256
MLA_NOPE = 128
MLA_ROPE = 64
MLA_V = 128
MLA_SCALE = 1.0 / math.sqrt(MLA_NOPE + MLA_ROPE)
ROPE_THETA = 10000.0
D_FF = ((8 * D_MODEL // 3 + 255) // 256) * 256
NORM_EPS = 1e-6
L2_EPS = 1e-6

LANES = 128
SUBLANES = 8
VMEM_LIMIT = 56 * 1024 * 1024

_O_QKV = 0
_O_BETA = _O_QKV + GDN_CONV_CH
_O_ALPHA = _O_BETA + GDN_HEADS
_O_Z = _O_ALPHA + GDN_HEADS
_O_RW = _O_Z + GDN_VAL_W
_O_CQ = _O_RW + RWKV_COLS
_O_CKV = _O_CQ + MLA_Q_RANK
_O_KR = _O_CKV + MLA_KV_RANK
_O_GATE = _O_KR + MLA_ROPE

RW_PAD = 3 * RWKV_W + 3 * LANES
PRE_ROW0 = SUBLANES
QK_HEAD = MLA_NOPE + LANES
DECODE_SLOTS = 3
RWKV_GROUP = 2

MIXER_TILE = 256
MIXER_CHUNK = 64
MIXER_SEQS_PROMPT = 2
MIXER_SEQS_SAMPLE = 8
ROW_TILE = 512
DECODE_PAGES = 16
DECODE_SEQS = 2


def _bdot(a, b):
    return jnp.dot(a.astype(BF16), b.astype(BF16), preferred_element_type=F32)


def _bdot_nt(a, b):
    return lax.dot_general(a.astype(BF16), b.astype(BF16), (((1,), (1,)), ((), ())),
                           preferred_element_type=F32)


def _split_dot(a, b_exact, parts):
    acc = None
    rem = a
    for _ in range(parts):
        hi = rem.astype(BF16)
        t = jnp.dot(hi, b_exact, preferred_element_type=F32)
        acc = t if acc is None else acc + t
        rem = rem - hi.astype(F32)
    return acc


def _split_dot_rhs(a_exact, b, parts):
    acc = None
    rem = b
    for _ in range(parts):
        hi = rem.astype(BF16)
        t = jnp.dot(a_exact, hi, preferred_element_type=F32)
        acc = t if acc is None else acc + t
        rem = rem - hi.astype(F32)
    return acc


def _rms(x, g, eps=NORM_EPS):
    return x * lax.rsqrt(jnp.mean(x * x, axis=-1, keepdims=True) + eps) * g


def _sigmoid(x):
    return 0.5 * jnp.tanh(0.5 * x) + 0.5


def _softplus(x):
    return jnp.maximum(x, 0.0) + jnp.log(1.0 + jnp.exp(-jnp.abs(x)))


def _tri_masks(C):
    r = lax.broadcasted_iota(jnp.int32, (C, C), 0)
    c = lax.broadcasted_iota(jnp.int32, (C, C), 1)
    masks = []
    k = 0
    while (1 << k) < C:
        rr = r >> k
        cc = c >> k
        m = jnp.where((rr ^ cc) == 1, jnp.where((rr & 1) == 1, 1.0, 0.0), 0.0)
        masks.append(m.astype(F32))
        k += 1
    return r, c, masks


def _unit_lower_inverses(Ls, masks, eye):
    Ts = [eye - L * masks[0] for L in Ls]
    for m in masks[1:]:
        tmp = [_bdot(L * m, T) for L, T in zip(Ls, Ts)]
        Ts = [T - _bdot(T, t) for T, t in zip(Ts, tmp)]
    return Ts


def _chunk_tri(R, C):
    rt = lax.broadcasted_iota(jnp.int32, (R, R), 0)
    ct = lax.broadcasted_iota(jnp.int32, (R, R), 1)
    sh = C.bit_length() - 1
    same = (rt >> sh) == (ct >> sh)
    tri = jnp.where(same, jnp.where(rt >= ct, 1.0, 0.0), 0.0).astype(BF16)
    tri_t = jnp.where(same, jnp.where(ct >= rt, 1.0, 0.0), 0.0).astype(BF16)
    return tri, tri_t


def _gdn_kernel(x_ref, s0_ref, hist_ref, ng_ref, wg_ref, wbat_ref, cw_ref, prow_ref, pcol_ref, gng_ref, rep_ref,
                o_ref, s_ref, cb_ref, pre_scr, *, Bb, Tt, C, t_real, t_total):
    ti = pl.program_id(1)
    R = Bb * Tt

    @pl.when(ti == 0)
    def _():
        s_ref[...] = s0_ref[...]
        pre_scr[:, PRE_ROW0 - 3:PRE_ROW0, :] = hist_ref[...]

    h = _rms(x_ref[...].reshape(R, D_MODEL), ng_ref[...]).astype(BF16)
    proj = jnp.dot(h, wg_ref[...], preferred_element_type=F32)
    ba_row = lax.dot_general(wbat_ref[...], h, (((1,), (1,)), ((), ())),
                             preferred_element_type=F32)

    cw = cw_ref[...]
    pre_all = proj[:, :GDN_CONV_CH]
    real_rows = min(Tt, t_real)
    heads = []
    for bb in range(Bb):
        pre_scr[bb, PRE_ROW0:2 * PRE_ROW0, :] = pre_all[bb * Tt:bb * Tt + PRE_ROW0]
        yh = pre_scr[bb, PRE_ROW0 - 3:2 * PRE_ROW0 - 3, :] * cw[0:1]
        for i in range(1, GDN_CONV):
            yh = yh + pre_scr[bb, PRE_ROW0 - 3 + i:2 * PRE_ROW0 - 3 + i, :] * cw[i:i + 1]
        heads.append(yh)
        if real_rows >= PRE_ROW0:
            carry = pre_all[(bb + 1) * Tt - 3:(bb + 1) * Tt]
        else:
            carry = pre_scr[bb, PRE_ROW0 - 3 + real_rows:PRE_ROW0 + real_rows, :]
        pre_scr[bb, PRE_ROW0 - 3:PRE_ROW0, :] = carry
        cb_ref[bb] = carry
    if Tt == PRE_ROW0:
        y = heads[0] if Bb == 1 else jnp.concatenate(heads, axis=0)
    else:
        y = pre_all * cw[GDN_CONV - 1:GDN_CONV]
        for k in range(1, GDN_CONV):
            y = y + pltpu.roll(pre_all, k, axis=0) * cw[GDN_CONV - 1 - k:GDN_CONV - k]
        pieces = []
        for bb in range(Bb):
            pieces += [heads[bb], y[bb * Tt + PRE_ROW0:(bb + 1) * Tt]]
        y = jnp.concatenate(pieces, axis=0)
    qkv = y * _sigmoid(y)

    ba = proj[:, GDN_CONV_CH + GDN_VAL_W:]
    prow = prow_ref[...]
    beta_col = _sigmoid(ba)
    loga_col = -jnp.exp(prow[0:1]) * _softplus(ba + prow[1:2])
    pcol = pcol_ref[...]
    loga_row = -jnp.exp(pcol[:, 0:1]) * _softplus(ba_row + pcol[:, 1:2])
    if t_real < t_total:
        tcol = (lax.broadcasted_iota(jnp.int32, (R, 1), 0) & (Tt - 1)) + ti * Tt
        trow = (lax.broadcasted_iota(jnp.int32, (1, R), 1) & (Tt - 1)) + ti * Tt
        beta_col = jnp.where(tcol < t_real, beta_col, 0.0)
        loga_col = jnp.where(tcol < t_real, loga_col, 0.0)
        loga_row = jnp.where(trow < t_real, loga_row, 0.0)

    rep = rep_ref[...]
    beta_rep = _split_dot(beta_col, rep[:, :GDN_VAL_W], 2)
    loga_rep = _split_dot(loga_col, rep[:, GDN_VAL_W:], 3)
    tri, tri_t = _chunk_tri(R, C)
    gcol_rep = _split_dot_rhs(tri, loga_rep, 3)
    grow = _split_dot(loga_row, tri_t, 3)

    r, c, masks = _tri_masks(C)
    eye = jnp.where(r == c, 1.0, 0.0).astype(F32)
    incl = r >= c
    strict = r > c
    z = proj[:, GDN_CONV_CH:GDN_CONV_CH + GDN_VAL_W]
    gng = gng_ref[...]
    scale = GDN_DK ** -0.5

    qs, ks = [], []
    for hh in range(GDN_HEADS):
        qh = qkv[:, hh * GDN_DK:(hh + 1) * GDN_DK]
        kh = qkv[:, GDN_KEY_W + hh * GDN_DK:GDN_KEY_W + (hh + 1) * GDN_DK]
        qs.append(qh * lax.rsqrt(jnp.sum(qh * qh, axis=-1, keepdims=True) + L2_EPS) * scale)
        ks.append(kh * lax.rsqrt(jnp.sum(kh * kh, axis=-1, keepdims=True) + L2_EPS))

    nchunk = Tt // C
    bodies = [(bb, n, hh) for n in range(nchunk) for bb in range(Bb) for hh in range(GDN_HEADS)]
    Ls, pre = [], []
    for (bb, n, hh) in bodies:
        r0 = bb * Tt + n * C
        q = qs[hh][r0:r0 + C]
        k = ks[hh][r0:r0 + C]
        v = qkv[r0:r0 + C, 2 * GDN_KEY_W + hh * GDN_DV:2 * GDN_KEY_W + (hh + 1) * GDN_DV]
        bc = beta_rep[r0:r0 + C, hh * LANES:(hh + 1) * LANES]
        gc = gcol_rep[r0:r0 + C, hh * LANES:(hh + 1) * LANES]
        gr = grow[GDN_HEADS + hh:GDN_HEADS + hh + 1, r0:r0 + C]
        decay = jnp.where(incl, jnp.exp(jnp.where(incl, gc[:, :C] - gr, 0.0)), 0.0)
        eg = jnp.exp(gc)
        kb = k * bc
        kq = _bdot_nt(jnp.concatenate([kb, q], axis=0), k)
        Ls.append(jnp.where(strict, kq[:C] * decay, 0.0))
        qk = jnp.where(incl, kq[C:] * decay, 0.0)
        g_last = gc[C - 1:C, :]
        kd = k * jnp.exp(g_last - gc)
        pre.append((q * eg, qk, jnp.concatenate([v * bc, kb * eg], axis=1), kd, jnp.exp(g_last)))
    Ts = _unit_lower_inverses(Ls, masks, eye)
    uws = [_bdot(T, p[2]) for T, p in zip(Ts, pre)]
    kuws = [_bdot(p[3].T, uw) for p, uw in zip(pre, uws)]
    quws = [_bdot(p[1], uw) for p, uw in zip(pre, uws)]

    for i, (bb, n, hh) in enumerate(bodies):
        S = s_ref[bb, hh]
        qeg, _, _, _, egl = pre[i]
        o = _bdot(qeg - quws[i][:, GDN_DV:], S) + quws[i][:, :GDN_DV]
        s_ref[bb, hh] = S * egl - _bdot(kuws[i][:, GDN_DV:], S) + kuws[i][:, :GDN_DV]
        r0 = bb * Tt + n * C
        zh = z[r0:r0 + C, hh * GDN_DV:(hh + 1) * GDN_DV]
        o_ref[bb, n * C:(n + 1) * C, hh * GDN_DV:(hh + 1) * GDN_DV] = _rms(o, gng) * (zh * _sigmoid(zh))


def _wspec(a, l, **kw):
    if a.ndim == 2:
        return pl.BlockSpec(a.shape, lambda *_: (0, 0), **kw)
    return pl.BlockSpec((None,) + a.shape[1:], lambda *_: (l,) + (0,) * (a.ndim - 1), **kw)


def _gdn_call(x, s0, hist, ls, w, l, *, Bb, Tt, C, t_real):
    B, T, D = x.shape
    assert Tt & (Tt - 1) == 0 and C & (C - 1) == 0 and B % Bb == 0 and T % Tt == 0 and Tt % C == 0
    kern = functools.partial(_gdn_kernel, Bb=Bb, Tt=Tt, C=C, t_real=t_real, t_total=T)
    names = ['norm_mix_g', 'gdn_w', 'gdn_wba_t', 'gdn_conv_w', 'gdn_prow', 'gdn_pcol', 'gdn_norm_g', 'gdn_rep']
    return pl.pallas_call(
        kern,
        grid=(B // Bb, T // Tt),
        in_specs=[
            pl.BlockSpec((Bb, Tt, D), lambda b, t: (b, t, 0)),
            pl.BlockSpec((None, Bb, GDN_HEADS, GDN_DK, GDN_DV), lambda b, t: (ls, b, 0, 0, 0)),
            pl.BlockSpec((None, Bb, GDN_CONV - 1, GDN_CONV_CH), lambda b, t: (ls, b, 0, 0)),
        ] + [_wspec(w[n], l) for n in names],
        out_specs=[
            pl.BlockSpec((Bb, Tt, GDN_VAL_W), lambda b, t: (b, t, 0)),
            pl.BlockSpec((Bb, GDN_HEADS, GDN_DK, GDN_DV), lambda b, t: (b, 0, 0, 0)),
            pl.BlockSpec((Bb, GDN_CONV - 1, GDN_CONV_CH), lambda b, t: (b, 0, 0)),
        ],
        out_shape=[
            jax.ShapeDtypeStruct((B, T, GDN_VAL_W), F32),
            jax.ShapeDtypeStruct((B, GDN_HEADS, GDN_DK, GDN_DV), F32),
            jax.ShapeDtypeStruct((B, GDN_CONV - 1, GDN_CONV_CH), F32),
        ],
        scratch_shapes=[pltpu.VMEM((Bb, 2 * PRE_ROW0, GDN_CONV_CH), F32)],
        compiler_params=pltpu.CompilerParams(
            dimension_semantics=("arbitrary", "arbitrary"), vmem_limit_bytes=VMEM_LIMIT),
        name="gdn_mixer",
    )(x, s0, hist, *[w[n] for n in names])


def _rwkv_kernel(x_ref, s0_ref, sh_ref, ng_ref, wr_ref, mu_ref, w0_ref, w2_ref, a0_ref, a2_ref, g2_ref,
                 kk_ref, ka_ref, rk_ref, lng_ref, lnb_ref, e_ref,
                 o_ref, s_ref, sho_ref, pre_scr, y_scr, sp_scr, *, Bb, Tt, C, t_real, t_total):
    ti = pl.program_id(1)
    N = RWKV_HEAD
    R = Bb * Tt

    @pl.when(ti == 0)
    def _():
        pre_scr[:, 0:1, :] = sh_ref[...]

    h = _rms(x_ref[...].reshape(R, D_MODEL), ng_ref[...]).astype(BF16)
    pre = jnp.dot(h, wr_ref[...], preferred_element_type=F32)
    real_rows = min(Tt, t_real)
    rolled = pltpu.roll(pre, 1, axis=0)
    first = lax.broadcasted_iota(jnp.int32, (Tt, 1), 0) == 0
    prevs = []
    for bb in range(Bb):
        prevs.append(jnp.where(first, pre_scr[bb, 0:1, :], rolled[bb * Tt:(bb + 1) * Tt]))
        last = pre[bb * Tt + real_rows - 1:bb * Tt + real_rows]
        pre_scr[bb, 0:1, :] = last
        sho_ref[bb] = last
    prev = prevs[0] if Bb == 1 else jnp.concatenate(prevs, axis=0)

    xm = pre + (prev - pre) * mu_ref[...]
    W = RWKV_W
    rr = xm[:, 0:W]
    kx = xm[:, W:2 * W]
    vv = xm[:, 2 * W:3 * W]
    wlo = xm[:, 3 * W:3 * W + LANES]
    alo = xm[:, 3 * W + LANES:3 * W + 2 * LANES]
    glo = xm[:, 3 * W + 2 * LANES:3 * W + 3 * LANES]
    w_log = -_softplus(-(w0_ref[...] + _bdot(jnp.tanh(wlo), w2_ref[...]))) - 0.5
    logd = -jnp.exp(w_log)
    a = _sigmoid(a0_ref[...] + _bdot(alo, a2_ref[...]))
    gb = _bdot(_sigmoid(glo), g2_ref[...])
    E = e_ref[...]
    kkr = kx * kk_ref[...]
    kk = kkr * lax.rsqrt(_split_dot(kkr * kkr, E, 1) + L2_EPS)
    kb = kx * (1.0 + (a - 1.0) * ka_ref[...])
    a_eff = a
    v_eff = vv
    if t_real < t_total:
        tcol = (lax.broadcasted_iota(jnp.int32, (R, 1), 0) & (Tt - 1)) + ti * Tt
        valid = tcol < t_real
        logd = jnp.where(valid, logd, 0.0)
        a_eff = jnp.where(valid, a, 0.0)
        v_eff = jnp.where(valid, vv, 0.0)

    tri, _ = _chunk_tri(R, C)
    logG = _split_dot_rhs(tri, logd, 3)
    G = jnp.exp(logG)
    Ginv = jnp.exp(-logG)
    aq_all = kk * jnp.exp(logG - logd)
    bk_all = -(a_eff * kk) * Ginv
    kd_all = kb * Ginv
    rq_all = rr * G

    GH = RWKV_GROUP
    GW = GH * N
    HP = RWKV_HEADS // GH
    CG = GH * C
    head_c = lax.broadcasted_iota(jnp.int32, (1, GW), 1) >> (N.bit_length() - 1)
    rt = lax.broadcasted_iota(jnp.int32, (C, CG), 0)
    lt = lax.broadcasted_iota(jnp.int32, (C, CG), 1)
    ct = lt & (C - 1)
    head_t = lt >> (C.bit_length() - 1)
    strict = rt > ct
    incl = rt >= ct
    eye = jnp.where(rt == ct, 1.0, 0.0).astype(F32)
    masks = []
    k = 0
    while (1 << k) < C:
        masks.append(jnp.where(((rt >> k) ^ (ct >> k)) == 1, jnp.where(((rt >> k) & 1) == 1, 1.0, 0.0), 0.0)
                     .astype(F32))
        k += 1
    nshift = N.bit_length() - 1
    same_head = ((lax.broadcasted_iota(jnp.int32, (GW, GW), 0) >> nshift)
                 == (lax.broadcasted_iota(jnp.int32, (GW, GW), 1) >> nshift))

    def stack_c(y):
        return jnp.concatenate([jnp.where(head_c == j, y, 0.0) for j in range(GH)], axis=0)

    def stack_t(t):
        return jnp.concatenate([jnp.where(head_t == j, t, 0.0) for j in range(GH)], axis=0)

    @pl.when(ti == 0)
    def _():
        for bb in range(Bb):
            for pp in range(HP):
                rows = []
                for j in range(GH):
                    blocks = [s0_ref[bb, GH * pp + j] if jj == j else jnp.zeros((N, N), F32) for jj in range(GH)]
                    rows.append(jnp.concatenate(blocks, axis=1))
                sp_scr[bb, pp] = jnp.concatenate(rows, axis=0)

    nchunk = Tt // C
    bodies = [(bb, n, pp) for n in range(nchunk) for bb in range(Bb) for pp in range(HP)]
    nb = len(bodies)
    aqs, bks, kds, rqs, vhs, gls = [], [], [], [], [], []
    for (bb, n, pp) in bodies:
        r0 = bb * Tt + n * C
        lp = slice(pp * GW, (pp + 1) * GW)
        aqs.append(aq_all[r0:r0 + C, lp])
        bks.append(bk_all[r0:r0 + C, lp])
        kds.append(kd_all[r0:r0 + C, lp])
        rqs.append(rq_all[r0:r0 + C, lp])
        vhs.append(v_eff[r0:r0 + C, lp])
        gls.append(G[r0 + C - 1:r0 + C, lp])
    ars = [jnp.concatenate([aqs[i], rqs[i]], axis=0) for i in range(nb)]
    sbs = [_bdot_nt(ars[i], stack_c(bks[i])) for i in range(nb)]
    sks = [_bdot_nt(ars[i], stack_c(kds[i])) for i in range(nb)]
    Ls = [jnp.where(strict, -sbs[i][:C], 0.0) for i in range(nb)]
    ras = [jnp.where(incl, sbs[i][C:], 0.0) for i in range(nb)]
    rks = [jnp.where(incl, sks[i][C:], 0.0) for i in range(nb)]
    vstk = [stack_c(vhs[i]) for i in range(nb)]
    bmvs = [_bdot(jnp.where(strict, sks[i][:C], 0.0), vstk[i]) for i in range(nb)]
    Ts = [eye - L * masks[0] for L in Ls]
    for m in masks[1:]:
        tmp = [_bdot(Ls[i] * m, stack_t(Ts[i])) for i in range(nb)]
        Ts = [Ts[i] - _bdot(Ts[i], stack_t(tmp[i])) for i in range(nb)]
    TAs = [_bdot(Ts[i], stack_c(aqs[i])) for i in range(nb)]
    TBVs = [_bdot(Ts[i], stack_c(bmvs[i])) for i in range(nb)]
    Xs = [bks[i] * gls[i] for i in range(nb)]
    Zs = [kds[i] * gls[i] for i in range(nb)]
    Q1s = [jnp.where(same_head, _bdot(TAs[i].T, Xs[i]), 0.0) for i in range(nb)]
    M0s = [jnp.where(same_head,
                     _bdot(jnp.concatenate([TBVs[i], vhs[i]], axis=0).T, jnp.concatenate([Xs[i], Zs[i]], axis=0)),
                     0.0) for i in range(nb)]
    rqps = [rqs[i] + _bdot(ras[i], stack_c(TAs[i])) for i in range(nb)]
    y0s = [_bdot(ras[i], stack_c(TBVs[i])) + _bdot(rks[i], vstk[i]) for i in range(nb)]

    per_chunk = Bb * HP
    for n in range(nchunk):
        idx = range(n * per_chunk, (n + 1) * per_chunk)
        Ss = {i: sp_scr[bodies[i][0], bodies[i][2]] for i in idx}
        for i in idx:
            bb, _, pp = bodies[i]
            r0 = bb * Tt + n * C
            y_scr[r0:r0 + C, pp * GW:(pp + 1) * GW] = _bdot_nt(rqps[i], Ss[i]) + y0s[i]
        for i in idx:
            bb, _, pp = bodies[i]
            sp_scr[bb, pp] = Ss[i] * gls[i] + _bdot(Ss[i], Q1s[i]) + M0s[i]

    @pl.when(ti == pl.num_programs(1) - 1)
    def _():
        for bb in range(Bb):
            for pp in range(HP):
                sp = sp_scr[bb, pp]
                for j in range(GH):
                    s_ref[bb, GH * pp + j] = sp[j * N:(j + 1) * N, j * N:(j + 1) * N]

    y = y_scr[...]
    inv_n = 1.0 / N
    mu = _split_dot(y, E, 2) * inv_n
    yc = y - mu
    var = _split_dot(yc * yc, E, 1) * inv_n
    yn = yc * lax.rsqrt(var + RWKV_LN_EPS) * lng_ref[...] + lnb_ref[...]
    bonus = _split_dot(rr * kb * rk_ref[...], E, 1) * vv
    o_ref[...] = ((yn + bonus) * gb).reshape(Bb, Tt, RWKV_W)


def _rwkv_call(x, s0, shift, ls, w, l, *, Bb, Tt, C, t_real):
    B, T, D = x.shape
    assert Tt & (Tt - 1) == 0 and C & (C - 1) == 0 and B % Bb == 0 and T % Tt == 0 and Tt % C == 0
    kern = functools.partial(_rwkv_kernel, Bb=Bb, Tt=Tt, C=C, t_real=t_real, t_total=T)
    names = ['norm_mix_g', 'rwkv_w', 'rwkv_mu', 'rwkv_w0', 'rwkv_w2', 'rwkv_a0', 'rwkv_a2', 'rwkv_g2',
             'rwkv_k_k', 'rwkv_k_a', 'rwkv_r_k', 'rwkv_ln_g', 'rwkv_ln_b', 'rwkv_e']
    return pl.pallas_call(
        kern,
        grid=(B // Bb, T // Tt),
        in_specs=[
            pl.BlockSpec((Bb, Tt, D), lambda b, t: (b, t, 0)),
            pl.BlockSpec((None, Bb, RWKV_HEADS, RWKV_HEAD, RWKV_HEAD), lambda b, t: (ls, b, 0, 0, 0)),
            pl.BlockSpec((None, Bb, 1, RW_PAD), lambda b, t: (ls, b, 0, 0)),
        ] + [_wspec(w[n], l) for n in names],
        out_specs=[
            pl.BlockSpec((Bb, Tt, RWKV_W), lambda b, t: (b, t, 0)),
            pl.BlockSpec((Bb, RWKV_HEADS, RWKV_HEAD, RWKV_HEAD), lambda b, t: (b, 0, 0, 0)),
            pl.BlockSpec((Bb, 1, RW_PAD), lambda b, t: (b, 0, 0)),
        ],
        out_shape=[
            jax.ShapeDtypeStruct((B, T, RWKV_W), F32),
            jax.ShapeDtypeStruct((B, RWKV_HEADS, RWKV_HEAD, RWKV_HEAD), F32),
            jax.ShapeDtypeStruct((B, 1, RW_PAD), F32),
        ],
        scratch_shapes=[pltpu.VMEM((Bb, SUBLANES, RW_PAD), F32), pltpu.VMEM((Bb * Tt, RWKV_W), F32),
                        pltpu.VMEM((Bb, RWKV_HEADS // RWKV_GROUP, RWKV_GROUP * RWKV_HEAD, RWKV_GROUP * RWKV_HEAD),
                                   F32)],
        compiler_params=pltpu.CompilerParams(
            dimension_semantics=("arbitrary", "arbitrary"), vmem_limit_bytes=VMEM_LIMIT),
        name="rwkv_mixer",
    )(x, s0, shift, *[w[n] for n in names])


def _mla_proj_kernel(x_ref, ng_ref, wm_ref, qg_ref, kvg_ref, wq_ref, cos_ref, sin_ref, *rest, prompt):
    if prompt:
        wuk_ref, wuv_ref, ckv_ref, kr_ref, q_ref, k_ref, v_ref = rest
    else:
        ckv_ref, kr_ref, qn_ref, qr_ref = rest
    h = _rms(x_ref[...], ng_ref[...]).astype(BF16)
    p = jnp.dot(h, wm_ref[...], preferred_element_type=F32)
    cqn = _rms(p[:, :MLA_Q_RANK], qg_ref[...]).astype(BF16)
    ckv = _rms(p[:, MLA_Q_RANK:MLA_Q_RANK + MLA_KV_RANK], kvg_ref[...])
    o = MLA_Q_RANK + MLA_KV_RANK
    cos = cos_ref[...]
    sin = sin_ref[...]
    krp = p[:, o:o + LANES] * cos + p[:, o + LANES:o + 2 * LANES] * sin
    ckv_ref[...] = ckv
    kr_ref[...] = krp[:, :MLA_ROPE]
    q = jnp.dot(cqn, wq_ref[...], preferred_element_type=F32)
    wn = MLA_HEADS * MLA_NOPE
    wr = MLA_HEADS * LANES
    cos4 = jnp.concatenate([cos] * MLA_HEADS, axis=1)
    sin4 = jnp.concatenate([sin] * MLA_HEADS, axis=1)
    qn = q[:, :wn] * MLA_SCALE
    qr = (q[:, wn:wn + wr] * cos4 + q[:, wn + wr:wn + 2 * wr] * sin4) * MLA_SCALE
    if prompt:
        cb = ckv.astype(BF16)
        kn = jnp.dot(cb, wuk_ref[...], preferred_element_type=F32)
        qparts, kparts = [], []
        for hh in range(MLA_HEADS):
            qparts += [qn[:, hh * MLA_NOPE:(hh + 1) * MLA_NOPE], qr[:, hh * LANES:(hh + 1) * LANES]]
            kparts += [kn[:, hh * MLA_NOPE:(hh + 1) * MLA_NOPE], krp]
        q_ref[...] = jnp.concatenate(qparts, axis=1).astype(BF16)
        k_ref[...] = jnp.concatenate(kparts, axis=1).astype(BF16)
        v_ref[...] = lax.dot_general(wuv_ref[...], cb, (((1,), (1,)), ((), ())),
                                     preferred_element_type=F32).astype(BF16)
    else:
        qn_ref[...] = qn
        qr_ref[...] = qr


def _mla_proj_call(x2, w, l, cos, sin, *, tr, prompt):
    M, D = x2.shape
    nt = cos.shape[0] // tr
    full = lambda a: _wspec(a, l)
    tab = pl.BlockSpec((tr, LANES), lambda i: (i % nt, 0))
    tok = lambda n: pl.BlockSpec((tr, n), lambda i: (i, 0))
    ins = [x2, w['norm_mix_g'], w['mla_w'], w['mla_q_norm_g'], w['mla_kv_norm_g'], w['mla_wq'], cos, sin]
    in_specs = [tok(D)] + [full(a) for a in ins[1:6]] + [tab, tab]
    out_specs = [tok(MLA_KV_RANK), tok(MLA_ROPE)]
    out_shape = [jax.ShapeDtypeStruct((M, MLA_KV_RANK), F32), jax.ShapeDtypeStruct((M, MLA_ROPE), F32)]
    if prompt:
        T = cos.shape[0]
        ins += [w['mla_wuk'], w['mla_wuv_t']]
        in_specs += [full(w['mla_wuk']), full(w['mla_wuv_t'])]
        out_specs += [tok(MLA_HEADS * QK_HEAD), tok(MLA_HEADS * QK_HEAD),
                      pl.BlockSpec((None, MLA_HEADS * MLA_V, tr), lambda i: (i // nt, 0, i % nt))]
        out_shape += [jax.ShapeDtypeStruct((M, MLA_HEADS * QK_HEAD), BF16),
                      jax.ShapeDtypeStruct((M, MLA_HEADS * QK_HEAD), BF16),
                      jax.ShapeDtypeStruct((M // T, MLA_HEADS * MLA_V, T), BF16)]
    else:
        out_specs += [tok(MLA_HEADS * MLA_NOPE), tok(MLA_HEADS * LANES)]
        out_shape += [jax.ShapeDtypeStruct((M, MLA_HEADS * MLA_NOPE), F32),
                      jax.ShapeDtypeStruct((M, MLA_HEADS * LANES), F32)]
    return pl.pallas_call(
        functools.partial(_mla_proj_kernel, prompt=prompt),
        grid=(M // tr,),
        in_specs=in_specs, out_specs=out_specs, out_shape=out_shape,
        compiler_params=pltpu.CompilerParams(dimension_semantics=("arbitrary",), vmem_limit_bytes=VMEM_LIMIT),
        name="mla_proj_prompt" if prompt else "mla_proj_sample",
    )(*ins)


def _flash_kernel(qi_ref, ki_ref, q_ref, k_ref, vt_ref, o_ref, m_scr, l_scr, acc_scr, *, tq):
    qi = qi_ref[pl.program_id(1)]
    ki = ki_ref[pl.program_id(1)]

    @pl.when(ki == 0)
    def _():
        m_scr[...] = jnp.full(m_scr.shape, -jnp.inf, F32)
        l_scr[...] = jnp.zeros(l_scr.shape, F32)
        acc_scr[...] = jnp.zeros(acc_scr.shape, F32)

    def step(diagonal):
        if diagonal:
            kpos = lax.broadcasted_iota(jnp.int32, (tq, tq), 0)
            qpos = lax.broadcasted_iota(jnp.int32, (tq, tq), 1)
            keep = kpos <= qpos
        sts = []
        for hh in range(MLA_HEADS):
            lq = slice(hh * QK_HEAD, (hh + 1) * QK_HEAD)
            sts.append(lax.dot_general(k_ref[0, :, lq], q_ref[0, :, lq], (((1,), (1,)), ((), ())),
                                       preferred_element_type=F32))
        for hh in range(MLA_HEADS):
            rv = slice(hh * MLA_V, (hh + 1) * MLA_V)
            st = sts[hh]
            if diagonal:
                st = jnp.where(keep, st, -jnp.inf)
            m_old = m_scr[hh:hh + 1, :]
            m_new = jnp.maximum(m_old, jnp.max(st, axis=0, keepdims=True))
            alpha = jnp.exp(m_old - m_new)
            p = jnp.exp(st - m_new)
            l_scr[hh:hh + 1, :] = alpha * l_scr[hh:hh + 1, :] + jnp.sum(p, axis=0, keepdims=True)
            acc_scr[rv, :] = alpha * acc_scr[rv, :] + jnp.dot(vt_ref[0, rv, :], p.astype(BF16),
                                                              preferred_element_type=F32)
            m_scr[hh:hh + 1, :] = m_new

    @pl.when(ki < qi)
    def _():
        step(False)

    @pl.when(ki == qi)
    def _():
        step(True)
        for hh in range(MLA_HEADS):
            rv = slice(hh * MLA_V, (hh + 1) * MLA_V)
            o_ref[0, rv, :] = acc_scr[rv, :] / l_scr[hh:hh + 1, :]


def _flash_call(q, k, vt, *, tq):
    B, T, _ = q.shape
    nq = T // tq
    hv = vt.shape[1]
    pairs = [(i, j) for i in range(nq) for j in range(i + 1)]
    qi_list = jnp.asarray([p[0] for p in pairs], jnp.int32)
    ki_list = jnp.asarray([p[1] for p in pairs], jnp.int32)
    grid_spec = pltpu.PrefetchScalarGridSpec(
        num_scalar_prefetch=2,
        grid=(B, len(pairs)),
        in_specs=[pl.BlockSpec((1, tq, q.shape[2]), lambda b, p, qi, ki: (b, qi[p], 0)),
                  pl.BlockSpec((1, tq, k.shape[2]), lambda b, p, qi, ki: (b, ki[p], 0)),
                  pl.BlockSpec((1, hv, tq), lambda b, p, qi, ki: (b, 0, ki[p]))],
        out_specs=pl.BlockSpec((1, hv, tq), lambda b, p, qi, ki: (b, 0, qi[p])),
        scratch_shapes=[pltpu.VMEM((SUBLANES, tq), F32), pltpu.VMEM((SUBLANES, tq), F32),
                        pltpu.VMEM((hv, tq), F32)],
    )
    return pl.pallas_call(
        functools.partial(_flash_kernel, tq=tq),
        grid_spec=grid_spec,
        out_shape=jax.ShapeDtypeStruct((B, hv, T), F32),
        compiler_params=pltpu.CompilerParams(
            dimension_semantics=("arbitrary", "arbitrary"), vmem_limit_bytes=VMEM_LIMIT),
        name="mla_flash",
    )(qi_list, ki_list, q, k, vt)


def _decode_kernel(pt_ref, qn_ref, qr_ref, ckvn_ref, krn_ref, wuk_ref, wuv_ref, ckv_hbm, krt_hbm, o_ref,
                   bufc, bufk, sem, *, G, NJ, BB, layer, nsteps):
    s = pl.program_id(0)
    total = nsteps * NJ
    ahead = DECODE_SLOTS - 1
    nj_shift = NJ.bit_length() - 1
    page = bufc.shape[3]
    rid = lax.broadcasted_iota(jnp.int32, (SUBLANES, 1), 0)

    def page_copies(t, bi, i, slot):
        pg = pt_ref[(t >> nj_shift) * BB + bi, (t & (NJ - 1)) * G + i]
        return (pltpu.make_async_copy(ckv_hbm.at[layer, pg], bufc.at[bi, slot, i], sem.at[0, slot]),
                pltpu.make_async_copy(krt_hbm.at[layer, pg], bufk.at[bi, slot, i], sem.at[1, slot]))

    @pl.when(s == 0)
    def _():
        for t0 in range(ahead):
            for bi in range(BB):
                for i in range(G):
                    for c in page_copies(t0, bi, i, t0):
                        c.start(priority=i % 2)

    qlats, qrms = [], []
    for bi in range(BB):
        qn = qn_ref[bi]
        qr = qr_ref[bi]
        qlat = jnp.zeros((SUBLANES, MLA_KV_RANK), F32)
        qrm = jnp.zeros((SUBLANES, MLA_ROPE), F32)
        for hh in range(MLA_HEADS):
            ql = _bdot_nt(qn[:, hh * MLA_NOPE:(hh + 1) * MLA_NOPE], wuk_ref[hh])
            qlat = jnp.where(rid == hh, ql[0:1, :], qlat)
            qrm = jnp.where(rid == hh, qr[0:1, hh * LANES:hh * LANES + MLA_ROPE], qrm)
        qlats.append(qlat)
        qrms.append(qrm)

    def body(j, carry):
        t = s * NJ + j
        slot = lax.rem(t, DECODE_SLOTS)
        t_next = jnp.minimum(t + ahead, total - 1)
        slot_next = lax.rem(t + ahead, DECODE_SLOTS)
        for bi in range(BB):
            for i in range(G):
                for c in page_copies(t, bi, i, slot):
                    c.wait()
        scores = [[] for _ in range(BB)]
        for i in range(G):
            for bi in range(BB):
                sc = (_bdot_nt(qlats[bi], bufc[bi, slot, i]) + _bdot(qrms[bi], bufk[bi, slot, i]))
                scores[bi].append(sc)
                for c in page_copies(t_next, bi, i, slot_next):
                    c.start(priority=i % 2)
        out = []
        for bi in range(BB):
            m_old, l_old, acc_old = carry[bi]
            sc = jnp.concatenate(scores[bi], axis=1)
            m_new = jnp.maximum(m_old, jnp.max(sc, axis=-1, keepdims=True))
            alpha = jnp.exp(m_old - m_new)
            p = jnp.exp(sc - m_new)
            l_new = alpha * l_old + jnp.sum(p, axis=-1, keepdims=True)
            acc = alpha * acc_old
            for i in range(G):
                acc = acc + _bdot(p[:, i * page:(i + 1) * page], bufc[bi, slot, i])
            out.append((m_new, l_new, acc))
        return tuple(out)

    init = tuple((jnp.full((SUBLANES, 1), -jnp.inf, F32), jnp.zeros((SUBLANES, 1), F32),
                  jnp.zeros((SUBLANES, MLA_KV_RANK), F32)) for _ in range(BB))
    final = lax.fori_loop(0, NJ, body, init)

    @pl.when(s == nsteps - 1)
    def _():
        for extra in range(ahead):
            for bi in range(BB):
                for i in range(G):
                    for c in page_copies(total - 1, bi, i, (total + extra) % DECODE_SLOTS):
                        c.wait()

    for bi in range(BB):
        m_new, l_new, acc = final[bi]
        ckvn = ckvn_ref[bi][0:1, :]
        krn = krn_ref[bi][0:1, :]
        s_new = (jnp.sum(qlats[bi] * ckvn, axis=-1, keepdims=True)
                 + jnp.sum(qrms[bi] * krn, axis=-1, keepdims=True))
        m_fin = jnp.maximum(m_new, s_new)
        a2 = jnp.exp(m_new - m_fin)
        p_new = jnp.exp(s_new - m_fin)
        l_fin = a2 * l_new + p_new
        o_lat = (a2 * acc + p_new * ckvn) / l_fin
        outs = []
        for hh in range(MLA_HEADS):
            oh = _bdot(o_lat, wuv_ref[hh])
            outs.append(oh[hh:hh + 1, :])
        o_ref[bi] = jnp.broadcast_to(jnp.concatenate(outs, axis=1), (SUBLANES, MLA_HEADS * MLA_V))


def _decode_call(page_table, qn, qr, ckvn, krn, cache_ckv, cache_krope_t, layer, w):
    B = qn.shape[0]
    n_pages = page_table.shape[1]
    page = cache_ckv.shape[2]
    G = _pick_tile(n_pages // 2, DECODE_PAGES)
    NJ = n_pages // G
    BB = _pick_tile(B, DECODE_SEQS)
    assert NJ >= 2 and NJ & (NJ - 1) == 0
    nsteps = B // BB
    tok = lambda a: pl.BlockSpec((BB,) + a.shape[1:], lambda b, pt: (b, 0, 0))
    full = lambda a: _wspec(a, layer)
    hbm = pl.BlockSpec(memory_space=pl.ANY)
    grid_spec = pltpu.PrefetchScalarGridSpec(
        num_scalar_prefetch=1,
        grid=(nsteps,),
        in_specs=[tok(qn), tok(qr), tok(ckvn), tok(krn), full(w['mla_wuk_h']), full(w['mla_wuv_h']), hbm, hbm],
        out_specs=pl.BlockSpec((BB, SUBLANES, MLA_HEADS * MLA_V), lambda b, pt: (b, 0, 0)),
        scratch_shapes=[pltpu.VMEM((BB, DECODE_SLOTS, G, page, MLA_KV_RANK), F32),
                        pltpu.VMEM((BB, DECODE_SLOTS, G, MLA_ROPE, page), F32),
                        pltpu.SemaphoreType.DMA((2, DECODE_SLOTS))],
    )
    return pl.pallas_call(
        functools.partial(_decode_kernel, G=G, NJ=NJ, BB=BB, layer=layer, nsteps=nsteps),
        grid_spec=grid_spec,
        out_shape=jax.ShapeDtypeStruct((B, SUBLANES, MLA_HEADS * MLA_V), F32),
        compiler_params=pltpu.CompilerParams(
            dimension_semantics=("arbitrary",), vmem_limit_bytes=VMEM_LIMIT),
        name="mla_decode",
    )(page_table, qn, qr, ckvn, krn, w['mla_wuk_h'], w['mla_wuv_h'], cache_ckv, cache_krope_t)


def _merge_ffn_kernel(x_ref, oa_ref, ob_ref, oc_ref, ng_ref, wgt_ref, bg_ref, wa_ref, wb_ref, wc_ref, wo_ref,
                      nf_ref, wup_ref, wdn_ref, fin_ref, y_ref, *, final, oc_transposed):
    x = x_ref[...]
    h = _rms(x, ng_ref[...]).astype(BF16)
    gates = _sigmoid(jnp.dot(h, wgt_ref[...], preferred_element_type=F32) + bg_ref[...])
    D = D_MODEL
    oc = oc_ref[...].T if oc_transposed else oc_ref[...]
    merged = (gates[:, 0:D] * _bdot(oa_ref[...], wa_ref[...])
              + gates[:, D:2 * D] * _bdot(ob_ref[...], wb_ref[...])
              + gates[:, 2 * D:3 * D] * _bdot(oc, wc_ref[...]))
    x1 = x + _bdot(merged, wo_ref[...])
    h2 = _rms(x1, nf_ref[...]).astype(BF16)
    up = jnp.dot(h2, wup_ref[...], preferred_element_type=F32)
    g = up[:, :D_FF]
    x2 = x1 + _bdot(g * _sigmoid(g) * up[:, D_FF:], wdn_ref[...])
    y_ref[...] = _rms(x2, fin_ref[...]) if final else x2


def _merge_ffn_call(x, oa, ob, oc, w, l, fin_g, *, tm, final):
    M, D = x.shape
    const = lambda a: _wspec(a, l, pipeline_mode=pl.Buffered(1))
    row = lambda n: pl.BlockSpec((tm, n), lambda i: (i, 0))
    names = ['norm_mix_g', 'w_gate', 'b_gate', 'w_br_a', 'w_br_b', 'w_br_c', 'w_out', 'norm_ffn_g',
             'w_ffn_up', 'w_ffn_down']
    oc_transposed = oc.ndim == 3
    if oc_transposed:
        nt = oc.shape[2] // tm
        oc_spec = pl.BlockSpec((None, oc.shape[1], tm), lambda i: (i // nt, 0, i % nt))
    else:
        oc_spec = row(oc.shape[1])
    return pl.pallas_call(
        functools.partial(_merge_ffn_kernel, final=final, oc_transposed=oc_transposed),
        grid=(M // tm,),
        in_specs=[row(D), row(oa.shape[1]), row(ob.shape[1]), oc_spec]
                 + [const(w[n]) for n in names] + [const(fin_g)],
        out_specs=row(D),
        out_shape=jax.ShapeDtypeStruct((M, D), F32),
        compiler_params=pltpu.CompilerParams(
            dimension_semantics=("arbitrary",), vmem_limit_bytes=VMEM_LIMIT),
        name="merge_ffn",
    )(x, oa, ob, oc, *[w[n] for n in names], fin_g)


def _pad_cols(a, n):
    return jnp.pad(a, [(0, 0)] * (a.ndim - 1) + [(0, n - a.shape[-1])])


def _swap_halves(a, width):
    shp = a.shape
    a = a.reshape(shp[:-1] + (shp[-1] // width, 2, width // 2))
    return a[..., ::-1, :].reshape(shp)


def _pad_groups(a, width, to):
    shp = a.shape
    a = a.reshape(shp[:-1] + (shp[-1] // width, width))
    a = jnp.pad(a, [(0, 0)] * (a.ndim - 1) + [(0, to - width)])
    return a.reshape(shp[:-1] + (-1,))


def _rw_pad(a):
    W = RWKV_W
    z = jnp.zeros(a.shape[:-1] + (LANES - RWKV_DECAY_LORA,), a.dtype)
    return jnp.concatenate([a[..., :3 * W], a[..., 3 * W:3 * W + RWKV_DECAY_LORA], z,
                            a[..., 3 * W + RWKV_DECAY_LORA:3 * W + RWKV_DECAY_LORA + RWKV_AAA_LORA], z,
                            a[..., 3 * W + RWKV_DECAY_LORA + RWKV_AAA_LORA:]], axis=-1)


def _rw_unpad(a):
    W = RWKV_W
    return jnp.concatenate([a[..., :3 * W], a[..., 3 * W:3 * W + RWKV_DECAY_LORA],
                            a[..., 3 * W + LANES:3 * W + LANES + RWKV_AAA_LORA],
                            a[..., 3 * W + 2 * LANES:]], axis=-1)


def _prep_weights(p):
    w_in = p['w_in']
    depth = w_in.shape[0]
    row = lambda a: a.reshape(depth, 1, -1).astype(F32)
    w = {}
    w['norm_mix_g'] = row(p['norm_mix_g'])
    w['norm_ffn_g'] = row(p['norm_ffn_g'])
    ba = w_in[..., _O_BETA:_O_Z]
    w['gdn_w'] = jnp.concatenate([w_in[..., _O_QKV:_O_BETA], w_in[..., _O_Z:_O_RW], _pad_cols(ba, LANES)],
                                 axis=-1).astype(BF16)
    w['gdn_wba_t'] = jnp.swapaxes(ba, 1, 2).astype(BF16)
    w['gdn_conv_w'] = p['gdn_conv_w']
    zero4 = jnp.zeros((depth, GDN_HEADS), F32)
    a_log = jnp.concatenate([zero4, p['gdn_a_log']], axis=1)
    dt_b = jnp.concatenate([zero4, p['gdn_dt_bias']], axis=1)
    w['gdn_prow'] = _pad_cols(jnp.stack([a_log, dt_b], axis=1), LANES)
    w['gdn_pcol'] = jnp.stack([a_log, dt_b], axis=2)
    w['gdn_norm_g'] = row(p['gdn_norm_g'])
    lane = jnp.arange(LANES)[:, None]
    head_of_col = jnp.arange(GDN_VAL_W)[None, :] // GDN_DV
    w['gdn_rep'] = jnp.concatenate([lane == head_of_col, lane == head_of_col + GDN_HEADS], axis=1).astype(BF16)
    w['rwkv_w'] = _rw_pad(w_in[..., _O_RW:_O_CQ]).astype(BF16)
    w['rwkv_mu'] = row(_rw_pad(p['rwkv_mu']))
    w['rwkv_w0'] = row(p['rwkv_w0'])
    w['rwkv_w2'] = jnp.pad(p['rwkv_w2'], ((0, 0), (0, LANES - RWKV_DECAY_LORA), (0, 0))).astype(BF16)
    w['rwkv_a0'] = row(p['rwkv_a0'])
    w['rwkv_a2'] = jnp.pad(p['rwkv_a2'], ((0, 0), (0, LANES - RWKV_AAA_LORA), (0, 0))).astype(BF16)
    w['rwkv_g2'] = p['rwkv_g2'].astype(BF16)
    w['rwkv_k_k'] = row(p['rwkv_k_k'])
    w['rwkv_k_a'] = row(p['rwkv_k_a'])
    w['rwkv_r_k'] = row(p['rwkv_r_k'])
    w['rwkv_ln_g'] = row(p['rwkv_ln_g'])
    w['rwkv_ln_b'] = row(p['rwkv_ln_b'])
    hid = jnp.arange(RWKV_W) // RWKV_HEAD
    w['rwkv_e'] = (hid[:, None] == hid[None, :]).astype(BF16)
    w_kr = w_in[..., _O_KR:_O_GATE]
    w['mla_w'] = jnp.concatenate([w_in[..., _O_CQ:_O_KR], _pad_cols(w_kr, LANES),
                                  _pad_cols(_swap_halves(w_kr, MLA_ROPE), LANES)], axis=-1).astype(BF16)
    w['mla_q_norm_g'] = row(p['mla_q_norm_g'])
    w['mla_kv_norm_g'] = row(p['mla_kv_norm_g'])
    w_uq = p['mla_w_uq']
    wq_n = w_uq[..., :MLA_NOPE].reshape(depth, MLA_Q_RANK, -1)
    wq_r = w_uq[..., MLA_NOPE:].reshape(depth, MLA_Q_RANK, -1)
    w['mla_wq'] = jnp.concatenate([wq_n, _pad_groups(wq_r, MLA_ROPE, LANES),
                                   _pad_groups(_swap_halves(wq_r, MLA_ROPE), MLA_ROPE, LANES)],
                                  axis=-1).astype(BF16)
    w['mla_wuk'] = p['mla_w_uk'].reshape(depth, MLA_KV_RANK, -1).astype(BF16)
    w['mla_wuv_t'] = jnp.swapaxes(p['mla_w_uv'].reshape(depth, MLA_KV_RANK, -1), 1, 2).astype(BF16)
    w['mla_wuk_h'] = jnp.transpose(p['mla_w_uk'], (0, 2, 1, 3)).astype(BF16)
    w['mla_wuv_h'] = jnp.transpose(p['mla_w_uv'], (0, 2, 1, 3)).astype(BF16)
    w['w_gate'] = w_in[..., _O_GATE:].astype(BF16)
    w['b_gate'] = row(p['b_gate'])
    for n in ('w_br_a', 'w_br_b', 'w_br_c', 'w_out', 'w_ffn_up', 'w_ffn_down'):
        w[n] = p[n].astype(BF16)
    return w


def _rope_tables(pos):
    half = MLA_ROPE // 2
    freq = ROPE_THETA ** (-jnp.arange(half, dtype=F32) / half)
    ang = pos.astype(F32)[:, None] * freq
    cos = jnp.cos(ang)
    sin = jnp.sin(ang)
    return (_pad_cols(jnp.concatenate([cos, cos], axis=1), LANES),
            _pad_cols(jnp.concatenate([-sin, sin], axis=1), LANES))


def _pick_tile(T, pref):
    t = min(T, pref)
    while T % t:
        t //= 2
    return t


def kernel(x_prompt, x_sample, cache_ckv, cache_krope, page_table, state_gdn, state_gdn_conv, state_rwkv, state_rwkv_shift, norm_mix_g, norm_ffn_g, norm_final_g, w_in, b_gate, gdn_conv_w, gdn_a_log, gdn_dt_bias, gdn_norm_g, rwkv_mu, rwkv_w0, rwkv_w2, rwkv_a0, rwkv_a2, rwkv_g2, rwkv_k_k, rwkv_k_a, rwkv_r_k, rwkv_ln_g, rwkv_ln_b, mla_q_norm_g, mla_kv_norm_g, mla_w_uq, mla_w_uk, mla_w_uv, w_br_a, w_br_b, w_br_c, w_out, w_ffn_up, w_ffn_down):
    p = dict(norm_mix_g=norm_mix_g, norm_ffn_g=norm_ffn_g, w_in=w_in, b_gate=b_gate, gdn_conv_w=gdn_conv_w,
             gdn_a_log=gdn_a_log, gdn_dt_bias=gdn_dt_bias, gdn_norm_g=gdn_norm_g, rwkv_mu=rwkv_mu,
             rwkv_w0=rwkv_w0, rwkv_w2=rwkv_w2, rwkv_a0=rwkv_a0, rwkv_a2=rwkv_a2, rwkv_g2=rwkv_g2,
             rwkv_k_k=rwkv_k_k, rwkv_k_a=rwkv_k_a, rwkv_r_k=rwkv_r_k, rwkv_ln_g=rwkv_ln_g,
             rwkv_ln_b=rwkv_ln_b, mla_q_norm_g=mla_q_norm_g, mla_kv_norm_g=mla_kv_norm_g,
             mla_w_uq=mla_w_uq, mla_w_uk=mla_w_uk, mla_w_uv=mla_w_uv, w_br_a=w_br_a, w_br_b=w_br_b,
             w_br_c=w_br_c, w_out=w_out, w_ffn_up=w_ffn_up, w_ffn_down=w_ffn_down)
    depth = w_in.shape[0]
    bp, sp, D = x_prompt.shape
    bs, ss, _ = x_sample.shape
    assert ss == 1, "the sample group decodes one new token per sequence"
    past_len = page_table.shape[1] * cache_ckv.shape[2]
    fin_g = norm_final_g.reshape(1, -1).astype(F32)
    cache_krope_t = jnp.swapaxes(cache_krope, 2, 3)

    tp = _pick_tile(sp, MIXER_TILE)
    cp = _pick_tile(tp, MIXER_CHUNK)
    tq = _pick_tile(sp, ROW_TILE)
    ts = SUBLANES
    bb_s = _pick_tile(bs, MIXER_SEQS_SAMPLE)
    bb_p = _pick_tile(bp, MIXER_SEQS_PROMPT)
    cos_p, sin_p = _rope_tables(jnp.arange(sp, dtype=jnp.int32))
    cos_s, sin_s = _rope_tables(jnp.full((bs * ts,), past_len, dtype=jnp.int32))
    zeros_p = dict(
        gdn=jnp.zeros((1, bp, GDN_HEADS, GDN_DK, GDN_DV), F32),
        conv=jnp.zeros((1, bp, GDN_CONV - 1, GDN_CONV_CH), F32),
        rwkv=jnp.zeros((1, bp, RWKV_HEADS, RWKV_HEAD, RWKV_HEAD), F32),
        shift=jnp.zeros((1, bp, 1, RW_PAD), F32))
    shift_s = _rw_pad(state_rwkv_shift)[:, :, None]

    w = _prep_weights(p)
    xp = x_prompt
    xs = jnp.pad(x_sample, ((0, 0), (0, ts - ss), (0, 0)))
    new_p = [[] for _ in range(6)]
    new_s = [[] for _ in range(6)]
    mp = bp * sp
    tm_p = tq
    for l in range(depth):
        final = l == depth - 1
        oa, s_g, cbuf = _gdn_call(xp, zeros_p['gdn'], zeros_p['conv'], 0, w, l, Bb=bb_p, Tt=tp, C=cp, t_real=sp)
        ob, s_r, sh = _rwkv_call(xp, zeros_p['rwkv'], zeros_p['shift'], 0, w, l, Bb=bb_p, Tt=tp, C=cp, t_real=sp)
        ckv, kr, q, k, vt = _mla_proj_call(xp.reshape(mp, D), w, l, cos_p, sin_p, tr=tq, prompt=True)
        oc_t = _flash_call(q.reshape(bp, sp, -1), k.reshape(bp, sp, -1), vt, tq=tq)
        for lst, arr in zip(new_p, (ckv.reshape(bp, sp, -1), kr.reshape(bp, sp, -1), s_g, cbuf, s_r, sh[:, 0])):
            lst.append(arr)
        xp = _merge_ffn_call(xp.reshape(mp, D), oa.reshape(mp, -1), ob.reshape(mp, -1),
                             oc_t, w, l, fin_g, tm=tm_p, final=final).reshape(bp, sp, D)
        oa, s_g, cbuf = _gdn_call(xs, state_gdn, state_gdn_conv, l, w, l, Bb=bb_s, Tt=ts, C=ts, t_real=ss)
        ob, s_r, sh = _rwkv_call(xs, state_rwkv, shift_s, l, w, l, Bb=bb_s, Tt=ts, C=ts, t_real=ss)
        ckv, kr, qn, qr = _mla_proj_call(xs.reshape(bs * ts, D), w, l, cos_s, sin_s, tr=bs * ts, prompt=False)
        ckv = ckv.reshape(bs, ts, -1)
        kr = kr.reshape(bs, ts, -1)
        oc = _decode_call(page_table, qn.reshape(bs, ts, -1), qr.reshape(bs, ts, -1), ckv, kr,
                          cache_ckv, cache_krope_t, l, w)
        for lst, arr in zip(new_s, (ckv[:, :ss], kr[:, :ss], s_g, cbuf, s_r, sh[:, 0])):
            lst.append(arr)
        xs_real = _merge_ffn_call(xs[:, 0], oa[:, 0], ob[:, 0], oc[:, 0], w, l, fin_g, tm=bs, final=final)
        xs = jnp.pad(xs_real[:, None], ((0, 0), (0, ts - ss), (0, 0)))
    y_prompt = xp
    y_sample = xs[:, :ss]
    outs_p = [jnp.stack(a) for a in new_p]
    outs_s = [jnp.stack(a) for a in new_s]
    outs_p[5] = _rw_unpad(outs_p[5])
    outs_s[5] = _rw_unpad(outs_s[5])
    return (y_prompt, y_sample, *outs_p, *outs_s)
```

```python
import functools
import math

import jax
import jax.numpy as jnp
from jax import lax
from jax.experimental import pallas as pl
from jax.experimental.pallas import tpu as pltpu

F32 = jnp.float32
BF16 = jnp.bfloat16

D_MODEL = 1024
GDN_HEADS = 4
GDN_DK = 128
GDN_DV = 128
GDN_CONV = 4
GDN_KEY_W = GDN_HEADS * GDN_DK
GDN_VAL_W = GDN_HEADS * GDN_DV
GDN_CONV_CH = 2 * GDN_KEY_W + GDN_VAL_W
RWKV_HEADS = 8
RWKV_HEAD = 64
RWKV_W = RWKV_HEADS * RWKV_HEAD
RWKV_DECAY_LORA = 64
RWKV_AAA_LORA = 64
RWKV_GATE_LORA = 128
RWKV_COLS = 3 * RWKV_W + RWKV_DECAY_LORA + RWKV_AAA_LORA + RWKV_GATE_LORA
RWKV_LN_EPS = 64e-5
MLA_HEADS = 4
MLA_Q_RANK = 256
MLA_KV_RANK = 256
MLA_NOPE = 128
MLA_ROPE = 64
MLA_V = 128
MLA_SCALE = 1.0 / math.sqrt(MLA_NOPE + MLA_ROPE)
ROPE_THETA = 10000.0
D_FF = ((8 * D_MODEL // 3 + 255) // 256) * 256
NORM_EPS = 1e-6
L2_EPS = 1e-6

LANES = 128
SUBLANES = 8
VMEM_LIMIT = 56 * 1024 * 1024

_O_QKV = 0
_O_BETA = _O_QKV + GDN_CONV_CH
_O_ALPHA = _O_BETA + GDN_HEADS
_O_Z = _O_ALPHA + GDN_HEADS
_O_RW = _O_Z + GDN_VAL_W
_O_CQ = _O_RW + RWKV_COLS
_O_CKV = _O_CQ + MLA_Q_RANK
_O_KR = _O_CKV + MLA_KV_RANK
_O_GATE = _O_KR + MLA_ROPE

RW_PAD = 3 * RWKV_W + 3 * LANES
PRE_ROW0 = SUBLANES
QK_HEAD = MLA_NOPE + LANES
DECODE_SLOTS = 3
RWKV_GROUP = 2

MIXER_TILE = 256
MIXER_CHUNK = 64
MIXER_SEQS_PROMPT = 2
MIXER_SEQS_SAMPLE = 8
ROW_TILE = 512
DECODE_PAGES = 16
DECODE_SEQS = 2


def _bdot(a, b):
    return jnp.dot(a.astype(BF16), b.astype(BF16), preferred_element_type=F32)


def _bdot_nt(a, b):
    return lax.dot_general(a.astype(BF16), b.astype(BF16), (((1,), (1,)), ((), ())),
                           preferred_element_type=F32)


def _split_dot(a, b_exact, parts):
    acc = None
    rem = a
    for _ in range(parts):
        hi = rem.astype(BF16)
        t = jnp.dot(hi, b_exact, preferred_element_type=F32)
        acc = t if acc is None else acc + t
        rem = rem - hi.astype(F32)
    return acc


def _split_dot_rhs(a_exact, b, parts):
    acc = None
    rem = b
    for _ in range(parts):
        hi = rem.astype(BF16)
        t = jnp.dot(a_exact, hi, preferred_element_type=F32)
        acc = t if acc is None else acc + t
        rem = rem - hi.astype(F32)
    return acc


def _rms(x, g, eps=NORM_EPS):
    return x * lax.rsqrt(jnp.mean(x * x, axis=-1, keepdims=True) + eps) * g


def _sigmoid(x):
    return 0.5 * jnp.tanh(0.5 * x) + 0.5


def _softplus(x):
    return jnp.maximum(x, 0.0) + jnp.log(1.0 + jnp.exp(-jnp.abs(x)))


def _tri_masks(C):
    r = lax.broadcasted_iota(jnp.int32, (C, C), 0)
    c = lax.broadcasted_iota(jnp.int32, (C, C), 1)
    masks = []
    k = 0
    while (1 << k) < C:
        rr = r >> k
        cc = c >> k
        m = jnp.where((rr ^ cc) == 1, jnp.where((rr & 1) == 1, 1.0, 0.0), 0.0)
        masks.append(m.astype(F32))
        k += 1
    return r, c, masks


def _unit_lower_inverses(Ls, masks, eye):
    Ts = [eye - L * masks[0] for L in Ls]
    for m in masks[1:]:
        tmp = [_bdot(L * m, T) for L, T in zip(Ls, Ts)]
        Ts = [T - _bdot(T, t) for T, t in zip(Ts, tmp)]
    return Ts


def _chunk_tri(R, C):
    rt = lax.broadcasted_iota(jnp.int32, (R, R), 0)
    ct = lax.broadcasted_iota(jnp.int32, (R, R), 1)
    sh = C.bit_length() - 1
    same = (rt >> sh) == (ct >> sh)
    tri = jnp.where(same, jnp.where(rt >= ct, 1.0, 0.0), 0.0).astype(BF16)
    tri_t = jnp.where(same, jnp.where(ct >= rt, 1.0, 0.0), 0.0).astype(BF16)
    return tri, tri_t


def _gdn_kernel(x_ref, s0_ref, hist_ref, ng_ref, wg_ref, wbat_ref, cw_ref, prow_ref, pcol_ref, gng_ref, rep_ref,
                o_ref, s_ref, cb_ref, pre_scr, *, Bb, Tt, C, t_real, t_total):
    ti = pl.program_id(1)
    R = Bb * Tt

    @pl.when(ti == 0)
    def _():
        s_ref[...] = s0_ref[...]
        pre_scr[:, PRE_ROW0 - 3:PRE_ROW0, :] = hist_ref[...]

    h = _rms(x_ref[...].reshape(R, D_MODEL), ng_ref[...]).astype(BF16)
    proj = jnp.dot(h, wg_ref[...], preferred_element_type=F32)
    ba_row = lax.dot_general(wbat_ref[...], h, (((1,), (1,)), ((), ())),
                             preferred_element_type=F32)

    cw = cw_ref[...]
    pre_all = proj[:, :GDN_CONV_CH]
    real_rows = min(Tt, t_real)
    heads = []
    for bb in range(Bb):
        pre_scr[bb, PRE_ROW0:2 * PRE_ROW0, :] = pre_all[bb * Tt:bb * Tt + PRE_ROW0]
        yh = pre_scr[bb, PRE_ROW0 - 3:2 * PRE_ROW0 - 3, :] * cw[0:1]
        for i in range(1, GDN_CONV):
            yh = yh + pre_scr[bb, PRE_ROW0 - 3 + i:2 * PRE_ROW0 - 3 + i, :] * cw[i:i + 1]
        heads.append(yh)
        if real_rows >= PRE_ROW0:
            carry = pre_all[(bb + 1) * Tt - 3:(bb + 1) * Tt]
        else:
            carry = pre_scr[bb, PRE_ROW0 - 3 + real_rows:PRE_ROW0 + real_rows, :]
        pre_scr[bb, PRE_ROW0 - 3:PRE_ROW0, :] = carry
        cb_ref[bb] = carry
    if Tt == PRE_ROW0:
        y = heads[0] if Bb == 1 else jnp.concatenate(heads, axis=0)
    else:
        y = pre_all * cw[GDN_CONV - 1:GDN_CONV]
        for k in range(1, GDN_CONV):
            y = y + pltpu.roll(pre_all, k, axis=0) * cw[GDN_CONV - 1 - k:GDN_CONV - k]
        pieces = []
        for bb in range(Bb):
            pieces += [heads[bb], y[bb * Tt + PRE_ROW0:(bb + 1) * Tt]]
        y = jnp.concatenate(pieces, axis=0)
    qkv = y * _sigmoid(y)

    ba = proj[:, GDN_CONV_CH + GDN_VAL_W:]
    prow = prow_ref[...]
    beta_col = _sigmoid(ba)
    loga_col = -jnp.exp(prow[0:1]) * _softplus(ba + prow[1:2])
    pcol = pcol_ref[...]
    loga_row = -jnp.exp(pcol[:, 0:1]) * _softplus(ba_row + pcol[:, 1:2])
    if t_real < t_total:
        tcol = (lax.broadcasted_iota(jnp.int32, (R, 1), 0) & (Tt - 1)) + ti * Tt
        trow = (lax.broadcasted_iota(jnp.int32, (1, R), 1) & (Tt - 1)) + ti * Tt
        beta_col = jnp.where(tcol < t_real, beta_col, 0.0)
        loga_col = jnp.where(tcol < t_real, loga_col, 0.0)
        loga_row = jnp.where(trow < t_real, loga_row, 0.0)

    rep = rep_ref[...]
    beta_rep = _split_dot(beta_col, rep[:, :GDN_VAL_W], 2)
    loga_rep = _split_dot(loga_col, rep[:, GDN_VAL_W:], 3)
    tri, tri_t = _chunk_tri(R, C)
    gcol_rep = _split_dot_rhs(tri, loga_rep, 3)
    grow = _split_dot(loga_row, tri_t, 3)

    r, c, masks = _tri_masks(C)
    eye = jnp.where(r == c, 1.0, 0.0).astype(F32)
    incl = r >= c
    strict = r > c
    z = proj[:, GDN_CONV_CH:GDN_CONV_CH + GDN_VAL_W]
    gng = gng_ref[...]
    scale = GDN_DK ** -0.5

    qs, ks = [], []
    for hh in range(GDN_HEADS):
        qh = qkv[:, hh * GDN_DK:(hh + 1) * GDN_DK]
        kh = qkv[:, GDN_KEY_W + hh * GDN_DK:GDN_KEY_W + (hh + 1) * GDN_DK]
        qs.append(qh * lax.rsqrt(jnp.sum(qh * qh, axis=-1, keepdims=True) + L2_EPS) * scale)
        ks.append(kh * lax.rsqrt(jnp.sum(kh * kh, axis=-1, keepdims=True) + L2_EPS))

    nchunk = Tt // C
    bodies = [(bb, n, hh) for n in range(nchunk) for bb in range(Bb) for hh in range(GDN_HEADS)]
    Ls, pre = [], []
    for (bb, n, hh) in bodies:
        r0 = bb * Tt + n * C
        q = qs[hh][r0:r0 + C]
        k = ks[hh][r0:r0 + C]
        v = qkv[r0:r0 + C, 2 * GDN_KEY_W + hh * GDN_DV:2 * GDN_KEY_W + (hh + 1) * GDN_DV]
        bc = beta_rep[r0:r0 + C, hh * LANES:(hh + 1) * LANES]
        gc = gcol_rep[r0:r0 + C, hh * LANES:(hh + 1) * LANES]
        gr = grow[GDN_HEADS + hh:GDN_HEADS + hh + 1, r0:r0 + C]
        decay = jnp.where(incl, jnp.exp(jnp.where(incl, gc[:, :C] - gr, 0.0)), 0.0)
        eg = jnp.exp(gc)
        kb = k * bc
        kq = _bdot_nt(jnp.concatenate([kb, q], axis=0), k)
        Ls.append(jnp.where(strict, kq[:C] * decay, 0.0))
        qk = jnp.where(incl, kq[C:] * decay, 0.0)
        g_last = gc[C - 1:C, :]
        kd = k * jnp.exp(g_last - gc)
        pre.append((q * eg, qk, jnp.concatenate([v * bc, kb * eg], axis=1), kd, jnp.exp(g_last)))
    Ts = _unit_lower_inverses(Ls, masks, eye)
    uws = [_bdot(T, p[2]) for T, p in zip(Ts, pre)]
    kuws = [_bdot(p[3].T, uw) for p, uw in zip(pre, uws)]
    quws = [_bdot(p[1], uw) for p, uw in zip(pre, uws)]

    for i, (bb, n, hh) in enumerate(bodies):
        S = s_ref[bb, hh]
        qeg, _, _, _, egl = pre[i]
        o = _bdot(qeg - quws[i][:, GDN_DV:], S) + quws[i][:, :GDN_DV]
        s_ref[bb, hh] = S * egl - _bdot(kuws[i][:, GDN_DV:], S) + kuws[i][:, :GDN_DV]
        r0 = bb * Tt + n * C
        zh = z[r0:r0 + C, hh * GDN_DV:(hh + 1) * GDN_DV]
        o_ref[bb, n * C:(n + 1) * C, hh * GDN_DV:(hh + 1) * GDN_DV] = _rms(o, gng) * (zh * _sigmoid(zh))


def _wspec(a, l, **kw):
    if a.ndim == 2:
        return pl.BlockSpec(a.shape, lambda *_: (0, 0), **kw)
    return pl.BlockSpec((None,) + a.shape[1:], lambda *_: (l,) + (0,) * (a.ndim - 1), **kw)


def _gdn_call(x, s0, hist, ls, w, l, *, Bb, Tt, C, t_real):
    B, T, D = x.shape
    assert Tt & (Tt - 1) == 0 and C & (C - 1) == 0 and B % Bb == 0 and T % Tt == 0 and Tt % C == 0
    kern = functools.partial(_gdn_kernel, Bb=Bb, Tt=Tt, C=C, t_real=t_real, t_total=T)
    names = ['norm_mix_g', 'gdn_w', 'gdn_wba_t', 'gdn_conv_w', 'gdn_prow', 'gdn_pcol', 'gdn_norm_g', 'gdn_rep']
    return pl.pallas_call(
        kern,
        grid=(B // Bb, T // Tt),
        in_specs=[
            pl.BlockSpec((Bb, Tt, D), lambda b, t: (b, t, 0)),
            pl.BlockSpec((None, Bb, GDN_HEADS, GDN_DK, GDN_DV), lambda b, t: (ls, b, 0, 0, 0)),
            pl.BlockSpec((None, Bb, GDN_CONV - 1, GDN_CONV_CH), lambda b, t: (ls, b, 0, 0)),
        ] + [_wspec(w[n], l) for n in names],
        out_specs=[
            pl.BlockSpec((Bb, Tt, GDN_VAL_W), lambda b, t: (b, t, 0)),
            pl.BlockSpec((Bb, GDN_HEADS, GDN_DK, GDN_DV), lambda b, t: (b, 0, 0, 0)),
            pl.BlockSpec((Bb, GDN_CONV - 1, GDN_CONV_CH), lambda b, t: (b, 0, 0)),
        ],
        out_shape=[
            jax.ShapeDtypeStruct((B, T, GDN_VAL_W), F32),
            jax.ShapeDtypeStruct((B, GDN_HEADS, GDN_DK, GDN_DV), F32),
            jax.ShapeDtypeStruct((B, GDN_CONV - 1, GDN_CONV_CH), F32),
        ],
        scratch_shapes=[pltpu.VMEM((Bb, 2 * PRE_ROW0, GDN_CONV_CH), F32)],
        compiler_params=pltpu.CompilerParams(
            dimension_semantics=("arbitrary", "arbitrary"), vmem_limit_bytes=VMEM_LIMIT),
        name="gdn_mixer",
    )(x, s0, hist, *[w[n] for n in names])


def _rwkv_kernel(x_ref, s0_ref, sh_ref, ng_ref, wr_ref, mu_ref, w0_ref, w2_ref, a0_ref, a2_ref, g2_ref,
                 kk_ref, ka_ref, rk_ref, lng_ref, lnb_ref, e_ref,
                 o_ref, s_ref, sho_ref, pre_scr, y_scr, sp_scr, *, Bb, Tt, C, t_real, t_total):
    ti = pl.program_id(1)
    N = RWKV_HEAD
    R = Bb * Tt

    @pl.when(ti == 0)
    def _():
        pre_scr[:, 0:1, :] = sh_ref[...]

    h = _rms(x_ref[...].reshape(R, D_MODEL), ng_ref[...]).astype(BF16)
    pre = jnp.dot(h, wr_ref[...], preferred_element_type=F32)
    real_rows = min(Tt, t_real)
    rolled = pltpu.roll(pre, 1, axis=0)
    first = lax.broadcasted_iota(jnp.int32, (Tt, 1), 0) == 0
    prevs = []
    for bb in range(Bb):
        prevs.append(jnp.where(first, pre_scr[bb, 0:1, :], rolled[bb * Tt:(bb + 1) * Tt]))
        last = pre[bb * Tt + real_rows - 1:bb * Tt + real_rows]
        pre_scr[bb, 0:1, :] = last
        sho_ref[bb] = last
    prev = prevs[0] if Bb == 1 else jnp.concatenate(prevs, axis=0)

    xm = pre + (prev - pre) * mu_ref[...]
    W = RWKV_W
    rr = xm[:, 0:W]
    kx = xm[:, W:2 * W]
    vv = xm[:, 2 * W:3 * W]
    wlo = xm[:, 3 * W:3 * W + LANES]
    alo = xm[:, 3 * W + LANES:3 * W + 2 * LANES]
    glo = xm[:, 3 * W + 2 * LANES:3 * W + 3 * LANES]
    w_log = -_softplus(-(w0_ref[...] + _bdot(jnp.tanh(wlo), w2_ref[...]))) - 0.5
    logd = -jnp.exp(w_log)
    a = _sigmoid(a0_ref[...] + _bdot(alo, a2_ref[...]))
    gb = _bdot(_sigmoid(glo), g2_ref[...])
    E = e_ref[...]
    kkr = kx * kk_ref[...]
    kk = kkr * lax.rsqrt(_split_dot(kkr * kkr, E, 1) + L2_EPS)
    kb = kx * (1.0 + (a - 1.0) * ka_ref[...])
    a_eff = a
    v_eff = vv
    if t_real < t_total:
        tcol = (lax.broadcasted_iota(jnp.int32, (R, 1), 0) & (Tt - 1)) + ti * Tt
        valid = tcol < t_real
        logd = jnp.where(valid, logd, 0.0)
        a_eff = jnp.where(valid, a, 0.0)
        v_eff = jnp.where(valid, vv, 0.0)

    tri, _ = _chunk_tri(R, C)
    logG = _split_dot_rhs(tri, logd, 3)
    G = jnp.exp(logG)
    Ginv = jnp.exp(-logG)
    aq_all = kk * jnp.exp(logG - logd)
    bk_all = -(a_eff * kk) * Ginv
    kd_all = kb * Ginv
    rq_all = rr * G

    GH = RWKV_GROUP
    GW = GH * N
    HP = RWKV_HEADS // GH
    CG = GH * C
    head_c = lax.broadcasted_iota(jnp.int32, (1, GW), 1) >> (N.bit_length() - 1)
    rt = lax.broadcasted_iota(jnp.int32, (C, CG), 0)
    lt = lax.broadcasted_iota(jnp.int32, (C, CG), 1)
    ct = lt & (C - 1)
    head_t = lt >> (C.bit_length() - 1)
    strict = rt > ct
    incl = rt >= ct
    eye = jnp.where(rt == ct, 1.0, 0.0).astype(F32)
    masks = []
    k = 0
    while (1 << k) < C:
        masks.append(jnp.where(((rt >> k) ^ (ct >> k)) == 1, jnp.where(((rt >> k) & 1) == 1, 1.0, 0.0), 0.0)
                     .astype(F32))
        k += 1
    nshift = N.bit_length() - 1
    same_head = ((lax.broadcasted_iota(jnp.int32, (GW, GW), 0) >> nshift)
                 == (lax.broadcasted_iota(jnp.int32, (GW, GW), 1) >> nshift))

    def stack_c(y):
        return jnp.concatenate([jnp.where(head_c == j, y, 0.0) for j in range(GH)], axis=0)

    def stack_t(t):
        return jnp.concatenate([jnp.where(head_t == j, t, 0.0) for j in range(GH)], axis=0)

    @pl.when(ti == 0)
    def _():
        for bb in range(Bb):
            for pp in range(HP):
                rows = []
                for j in range(GH):
                    blocks = [s0_ref[bb, GH * pp + j] if jj == j else jnp.zeros((N, N), F32) for jj in range(GH)]
                    rows.append(jnp.concatenate(blocks, axis=1))
                sp_scr[bb, pp] = jnp.concatenate(rows, axis=0)

    nchunk = Tt // C
    bodies = [(bb, n, pp) for n in range(nchunk) for bb in range(Bb) for pp in range(HP)]
    nb = len(bodies)
    aqs, bks, kds, rqs, vhs, gls = [], [], [], [], [], []
    for (bb, n, pp) in bodies:
        r0 = bb * Tt + n * C
        lp = slice(pp * GW, (pp + 1) * GW)
        aqs.append(aq_all[r0:r0 + C, lp])
        bks.append(bk_all[r0:r0 + C, lp])
        kds.append(kd_all[r0:r0 + C, lp])
        rqs.append(rq_all[r0:r0 + C, lp])
        vhs.append(v_eff[r0:r0 + C, lp])
        gls.append(G[r0 + C - 1:r0 + C, lp])
    ars = [jnp.concatenate([aqs[i], rqs[i]], axis=0) for i in range(nb)]
    sbs = [_bdot_nt(ars[i], stack_c(bks[i])) for i in range(nb)]
    sks = [_bdot_nt(ars[i], stack_c(kds[i])) for i in range(nb)]
    Ls = [jnp.where(strict, -sbs[i][:C], 0.0) for i in range(nb)]
    ras = [jnp.where(incl, sbs[i][C:], 0.0) for i in range(nb)]
    rks = [jnp.where(incl, sks[i][C:], 0.0) for i in range(nb)]
    vstk = [stack_c(vhs[i]) for i in range(nb)]
    bmvs = [_bdot(jnp.where(strict, sks[i][:C], 0.0), vstk[i]) for i in range(nb)]
    Ts = [eye - L * masks[0] for L in Ls]
    for m in masks[1:]:
        tmp = [_bdot(Ls[i] * m, stack_t(Ts[i])) for i in range(nb)]
        Ts = [Ts[i] - _bdot(Ts[i], stack_t(tmp[i])) for i in range(nb)]
    TAs = [_bdot(Ts[i], stack_c(aqs[i])) for i in range(nb)]
    TBVs = [_bdot(Ts[i], stack_c(bmvs[i])) for i in range(nb)]
    Xs = [bks[i] * gls[i] for i in range(nb)]
    Zs = [kds[i] * gls[i] for i in range(nb)]
    Q1s = [jnp.where(same_head, _bdot(TAs[i].T, Xs[i]), 0.0) for i in range(nb)]
    M0s = [jnp.where(same_head,
                     _bdot(jnp.concatenate([TBVs[i], vhs[i]], axis=0).T, jnp.concatenate([Xs[i], Zs[i]], axis=0)),
                     0.0) for i in range(nb)]
    rqps = [rqs[i] + _bdot(ras[i], stack_c(TAs[i])) for i in range(nb)]
    y0s = [_bdot(ras[i], stack_c(TBVs[i])) + _bdot(rks[i], vstk[i]) for i in range(nb)]

    per_chunk = Bb * HP
    for n in range(nchunk):
        idx = range(n * per_chunk, (n + 1) * per_chunk)
        Ss = {i: sp_scr[bodies[i][0], bodies[i][2]] for i in idx}
        for i in idx:
            bb, _, pp = bodies[i]
            r0 = bb * Tt + n * C
            y_scr[r0:r0 + C, pp * GW:(pp + 1) * GW] = _bdot_nt(rqps[i], Ss[i]) + y0s[i]
        for i in idx:
            bb, _, pp = bodies[i]
            sp_scr[bb, pp] = Ss[i] * gls[i] + _bdot(Ss[i], Q1s[i]) + M0s[i]

    @pl.when(ti == pl.num_programs(1) - 1)
    def _():
        for bb in range(Bb):
            for pp in range(HP):
                sp = sp_scr[bb, pp]
                for j in range(GH):
                    s_ref[bb, GH * pp + j] = sp[j * N:(j + 1) * N, j * N:(j + 1) * N]

    y = y_scr[...]
    inv_n = 1.0 / N
    mu = _split_dot(y, E, 2) * inv_n
    yc = y - mu
    var = _split_dot(yc * yc, E, 1) * inv_n
    yn = yc * lax.rsqrt(var + RWKV_LN_EPS) * lng_ref[...] + lnb_ref[...]
    bonus = _split_dot(rr * kb * rk_ref[...], E, 1) * vv
    o_ref[...] = ((yn + bonus) * gb).reshape(Bb, Tt, RWKV_W)


def _rwkv_call(x, s0, shift, ls, w, l, *, Bb, Tt, C, t_real):
    B, T, D = x.shape
    assert Tt & (Tt - 1) == 0 and C & (C - 1) == 0 and B % Bb == 0 and T % Tt == 0 and Tt % C == 0
    kern = functools.partial(_rwkv_kernel, Bb=Bb, Tt=Tt, C=C, t_real=t_real, t_total=T)
    names = ['norm_mix_g', 'rwkv_w', 'rwkv_mu', 'rwkv_w0', 'rwkv_w2', 'rwkv_a0', 'rwkv_a2', 'rwkv_g2',
             'rwkv_k_k', 'rwkv_k_a', 'rwkv_r_k', 'rwkv_ln_g', 'rwkv_ln_b', 'rwkv_e']
    return pl.pallas_call(
        kern,
        grid=(B // Bb, T // Tt),
        in_specs=[
            pl.BlockSpec((Bb, Tt, D), lambda b, t: (b, t, 0)),
            pl.BlockSpec((None, Bb, RWKV_HEADS, RWKV_HEAD, RWKV_HEAD), lambda b, t: (ls, b, 0, 0, 0)),
            pl.BlockSpec((None, Bb, 1, RW_PAD), lambda b, t: (ls, b, 0, 0)),
        ] + [_wspec(w[n], l) for n in names],
        out_specs=[
            pl.BlockSpec((Bb, Tt, RWKV_W), lambda b, t: (b, t, 0)),
            pl.BlockSpec((Bb, RWKV_HEADS, RWKV_HEAD, RWKV_HEAD), lambda b, t: (b, 0, 0, 0)),
            pl.BlockSpec((Bb, 1, RW_PAD), lambda b, t: (b, 0, 0)),
        ],
        out_shape=[
            jax.ShapeDtypeStruct((B, T, RWKV_W), F32),
            jax.ShapeDtypeStruct((B, RWKV_HEADS, RWKV_HEAD, RWKV_HEAD), F32),
            jax.ShapeDtypeStruct((B, 1, RW_PAD), F32),
        ],
        scratch_shapes=[pltpu.VMEM((Bb, SUBLANES, RW_PAD), F32), pltpu.VMEM((Bb * Tt, RWKV_W), F32),
                        pltpu.VMEM((Bb, RWKV_HEADS // RWKV_GROUP, RWKV_GROUP * RWKV_HEAD, RWKV_GROUP * RWKV_HEAD),
                                   F32)],
        compiler_params=pltpu.CompilerParams(
            dimension_semantics=("arbitrary", "arbitrary"), vmem_limit_bytes=VMEM_LIMIT),
        name="rwkv_mixer",
    )(x, s0, shift, *[w[n] for n in names])


def _mla_proj_kernel(x_ref, ng_ref, wm_ref, qg_ref, kvg_ref, wq_ref, cos_ref, sin_ref, *rest, prompt):
    if prompt:
        wuk_ref, wuv_ref, ckv_ref, kr_ref, q_ref, k_ref, v_ref = rest
    else:
        ckv_ref, kr_ref, qn_ref, qr_ref = rest
    h = _rms(x_ref[...], ng_ref[...]).astype(BF16)
    p = jnp.dot(h, wm_ref[...], preferred_element_type=F32)
    cqn = _rms(p[:, :MLA_Q_RANK], qg_ref[...]).astype(BF16)
    ckv = _rms(p[:, MLA_Q_RANK:MLA_Q_RANK + MLA_KV_RANK], kvg_ref[...])
    o = MLA_Q_RANK + MLA_KV_RANK
    cos = cos_ref[...]
    sin = sin_ref[...]
    krp = p[:, o:o + LANES] * cos + p[:, o + LANES:o + 2 * LANES] * sin
    ckv_ref[...] = ckv
    kr_ref[...] = krp[:, :MLA_ROPE]
    q = jnp.dot(cqn, wq_ref[...], preferred_element_type=F32)
    wn = MLA_HEADS * MLA_NOPE
    wr = MLA_HEADS * LANES
    cos4 = jnp.concatenate([cos] * MLA_HEADS, axis=1)
    sin4 = jnp.concatenate([sin] * MLA_HEADS, axis=1)
    qn = q[:, :wn] * MLA_SCALE
    qr = (q[:, wn:wn + wr] * cos4 + q[:, wn + wr:wn + 2 * wr] * sin4) * MLA_SCALE
    if prompt:
        cb = ckv.astype(BF16)
        kn = jnp.dot(cb, wuk_ref[...], preferred_element_type=F32)
        qparts, kparts = [], []
        for hh in range(MLA_HEADS):
            qparts += [qn[:, hh * MLA_NOPE:(hh + 1) * MLA_NOPE], qr[:, hh * LANES:(hh + 1) * LANES]]
            kparts += [kn[:, hh * MLA_NOPE:(hh + 1) * MLA_NOPE], krp]
        q_ref[...] = jnp.concatenate(qparts, axis=1).astype(BF16)
        k_ref[...] = jnp.concatenate(kparts, axis=1).astype(BF16)
        v_ref[...] = lax.dot_general(wuv_ref[...], cb, (((1,), (1,)), ((), ())),
                                     preferred_element_type=F32).astype(BF16)
    else:
        qn_ref[...] = qn
        qr_ref[...] = qr


def _mla_proj_call(x2, w, l, cos, sin, *, tr, prompt):
    M, D = x2.shape
    nt = cos.shape[0] // tr
    full = lambda a: _wspec(a, l)
    tab = pl.BlockSpec((tr, LANES), lambda i: (i % nt, 0))
    tok = lambda n: pl.BlockSpec((tr, n), lambda i: (i, 0))
    ins = [x2, w['norm_mix_g'], w['mla_w'], w['mla_q_norm_g'], w['mla_kv_norm_g'], w['mla_wq'], cos, sin]
    in_specs = [tok(D)] + [full(a) for a in ins[1:6]] + [tab, tab]
    out_specs = [tok(MLA_KV_RANK), tok(MLA_ROPE)]
    out_shape = [jax.ShapeDtypeStruct((M, MLA_KV_RANK), F32), jax.ShapeDtypeStruct((M, MLA_ROPE), F32)]
    if prompt:
        T = cos.shape[0]
        ins += [w['mla_wuk'], w['mla_wuv_t']]
        in_specs += [full(w['mla_wuk']), full(w['mla_wuv_t'])]
        out_specs += [tok(MLA_HEADS * QK_HEAD), tok(MLA_HEADS * QK_HEAD),
                      pl.BlockSpec((None, MLA_HEADS * MLA_V, tr), lambda i: (i // nt, 0, i % nt))]
        out_shape += [jax.ShapeDtypeStruct((M, MLA_HEADS * QK_HEAD), BF16),
                      jax.ShapeDtypeStruct((M, MLA_HEADS * QK_HEAD), BF16),
                      jax.ShapeDtypeStruct((M // T, MLA_HEADS * MLA_V, T), BF16)]
    else:
        out_specs += [tok(MLA_HEADS * MLA_NOPE), tok(MLA_HEADS * LANES)]
        out_shape += [jax.ShapeDtypeStruct((M, MLA_HEADS * MLA_NOPE), F32),
                      jax.ShapeDtypeStruct((M, MLA_HEADS * LANES), F32)]
    return pl.pallas_call(
        functools.partial(_mla_proj_kernel, prompt=prompt),
        grid=(M // tr,),
        in_specs=in_specs, out_specs=out_specs, out_shape=out_shape,
        compiler_params=pltpu.CompilerParams(dimension_semantics=("arbitrary",), vmem_limit_bytes=VMEM_LIMIT),
        name="mla_proj_prompt" if prompt else "mla_proj_sample",
    )(*ins)


def _flash_kernel(qi_ref, ki_ref, q_ref, k_ref, vt_ref, o_ref, m_scr, l_scr, acc_scr, *, tq):
    qi = qi_ref[pl.program_id(1)]
    ki = ki_ref[pl.program_id(1)]

    @pl.when(ki == 0)
    def _():
        m_scr[...] = jnp.full(m_scr.shape, -jnp.inf, F32)
        l_scr[...] = jnp.zeros(l_scr.shape, F32)
        acc_scr[...] = jnp.zeros(acc_scr.shape, F32)

    def step(diagonal):
        if diagonal:
            kpos = lax.broadcasted_iota(jnp.int32, (tq, tq), 0)
            qpos = lax.broadcasted_iota(jnp.int32, (tq, tq), 1)
            keep = kpos <= qpos
        sts = []
        for hh in range(MLA_HEADS):
            lq = slice(hh * QK_HEAD, (hh + 1) * QK_HEAD)
            sts.append(lax.dot_general(k_ref[0, :, lq], q_ref[0, :, lq], (((1,), (1,)), ((), ())),
                                       preferred_element_type=F32))
        for hh in range(MLA_HEADS):
            rv = slice(hh * MLA_V, (hh + 1) * MLA_V)
            st = sts[hh]
            if diagonal:
                st = jnp.where(keep, st, -jnp.inf)
            m_old = m_scr[hh:hh + 1, :]
            m_new = jnp.maximum(m_old, jnp.max(st, axis=0, keepdims=True))
            alpha = jnp.exp(m_old - m_new)
            p = jnp.exp(st - m_new)
            l_scr[hh:hh + 1, :] = alpha * l_scr[hh:hh + 1, :] + jnp.sum(p, axis=0, keepdims=True)
            acc_scr[rv, :] = alpha * acc_scr[rv, :] + jnp.dot(vt_ref[0, rv, :], p.astype(BF16),
                                                              preferred_element_type=F32)
            m_scr[hh:hh + 1, :] = m_new

    @pl.when(ki < qi)
    def _():
        step(False)

    @pl.when(ki == qi)
    def _():
        step(True)
        for hh in range(MLA_HEADS):
            rv = slice(hh * MLA_V, (hh + 1) * MLA_V)
            o_ref[0, rv, :] = acc_scr[rv, :] / l_scr[hh:hh + 1, :]


def _flash_call(q, k, vt, *, tq):
    B, T, _ = q.shape
    nq = T // tq
    hv = vt.shape[1]
    pairs = [(i, j) for i in range(nq) for j in range(i + 1)]
    qi_list = jnp.asarray([p[0] for p in pairs], jnp.int32)
    ki_list = jnp.asarray([p[1] for p in pairs], jnp.int32)
    grid_spec = pltpu.PrefetchScalarGridSpec(
        num_scalar_prefetch=2,
        grid=(B, len(pairs)),
        in_specs=[pl.BlockSpec((1, tq, q.shape[2]), lambda b, p, qi, ki: (b, qi[p], 0)),
                  pl.BlockSpec((1, tq, k.shape[2]), lambda b, p, qi, ki: (b, ki[p], 0)),
                  pl.BlockSpec((1, hv, tq), lambda b, p, qi, ki: (b, 0, ki[p]))],
        out_specs=pl.BlockSpec((1, hv, tq), lambda b, p, qi, ki: (b, 0, qi[p])),
        scratch_shapes=[pltpu.VMEM((SUBLANES, tq), F32), pltpu.VMEM((SUBLANES, tq), F32),
                        pltpu.VMEM((hv, tq), F32)],
    )
    return pl.pallas_call(
        functools.partial(_flash_kernel, tq=tq),
        grid_spec=grid_spec,
        out_shape=jax.ShapeDtypeStruct((B, hv, T), F32),
        compiler_params=pltpu.CompilerParams(
            dimension_semantics=("arbitrary", "arbitrary"), vmem_limit_bytes=VMEM_LIMIT),
        name="mla_flash",
    )(qi_list, ki_list, q, k, vt)


def _decode_kernel(pt_ref, qn_ref, qr_ref, ckvn_ref, krn_ref, wuk_ref, wuv_ref, ckv_hbm, krt_hbm, o_ref,
                   bufc, bufk, sem, *, G, NJ, BB, layer, nsteps):
    s = pl.program_id(0)
    total = nsteps * NJ
    ahead = DECODE_SLOTS - 1
    nj_shift = NJ.bit_length() - 1
    page = bufc.shape[3]
    rid = lax.broadcasted_iota(jnp.int32, (SUBLANES, 1), 0)

    def page_copies(t, bi, i, slot):
        pg = pt_ref[(t >> nj_shift) * BB + bi, (t & (NJ - 1)) * G + i]
        return (pltpu.make_async_copy(ckv_hbm.at[layer, pg], bufc.at[bi, slot, i], sem.at[0, slot]),
                pltpu.make_async_copy(krt_hbm.at[layer, pg], bufk.at[bi, slot, i], sem.at[1, slot]))

    @pl.when(s == 0)
    def _():
        for t0 in range(ahead):
            for bi in range(BB):
                for i in range(G):
                    for c in page_copies(t0, bi, i, t0):
                        c.start(priority=i % 2)

    qlats, qrms = [], []
    for bi in range(BB):
        qn = qn_ref[bi]
        qr = qr_ref[bi]
        qlat = jnp.zeros((SUBLANES, MLA_KV_RANK), F32)
        qrm = jnp.zeros((SUBLANES, MLA_ROPE), F32)
        for hh in range(MLA_HEADS):
            ql = _bdot_nt(qn[:, hh * MLA_NOPE:(hh + 1) * MLA_NOPE], wuk_ref[hh])
            qlat = jnp.where(rid == hh, ql[0:1, :], qlat)
            qrm = jnp.where(rid == hh, qr[0:1, hh * LANES:hh * LANES + MLA_ROPE], qrm)
        qlats.append(qlat)
        qrms.append(qrm)

    def body(j, carry):
        t = s * NJ + j
        slot = lax.rem(t, DECODE_SLOTS)
        t_next = jnp.minimum(t + ahead, total - 1)
        slot_next = lax.rem(t + ahead, DECODE_SLOTS)
        for bi in range(BB):
            for i in range(G):
                for c in page_copies(t, bi, i, slot):
                    c.wait()
        scores = [[] for _ in range(BB)]
        for i in range(G):
            for bi in range(BB):
                sc = (_bdot_nt(qlats[bi], bufc[bi, slot, i]) + _bdot(qrms[bi], bufk[bi, slot, i]))
                scores[bi].append(sc)
                for c in page_copies(t_next, bi, i, slot_next):
                    c.start(priority=i % 2)
        out = []
        for bi in range(BB):
            m_old, l_old, acc_old = carry[bi]
            sc = jnp.concatenate(scores[bi], axis=1)
            m_new = jnp.maximum(m_old, jnp.max(sc, axis=-1, keepdims=True))
            alpha = jnp.exp(m_old - m_new)
            p = jnp.exp(sc - m_new)
            l_new = alpha * l_old + jnp.sum(p, axis=-1, keepdims=True)
            acc = alpha * acc_old
            for i in range(G):
                acc = acc + _bdot(p[:, i * page:(i + 1) * page], bufc[bi, slot, i])
            out.append((m_new, l_new, acc))
        return tuple(out)

    init = tuple((jnp.full((SUBLANES, 1), -jnp.inf, F32), jnp.zeros((SUBLANES, 1), F32),
                  jnp.zeros((SUBLANES, MLA_KV_RANK), F32)) for _ in range(BB))
    final = lax.fori_loop(0, NJ, body, init)

    @pl.when(s == nsteps - 1)
    def _():
        for extra in range(ahead):
            for bi in range(BB):
                for i in range(G):
                    for c in page_copies(total - 1, bi, i, (total + extra) % DECODE_SLOTS):
                        c.wait()

    for bi in range(BB):
        m_new, l_new, acc = final[bi]
        ckvn = ckvn_ref[bi][0:1, :]
        krn = krn_ref[bi][0:1, :]
        s_new = (jnp.sum(qlats[bi] * ckvn, axis=-1, keepdims=True)
                 + jnp.sum(qrms[bi] * krn, axis=-1, keepdims=True))
        m_fin = jnp.maximum(m_new, s_new)
        a2 = jnp.exp(m_new - m_fin)
        p_new = jnp.exp(s_new - m_fin)
        l_fin = a2 * l_new + p_new
        o_lat = (a2 * acc + p_new * ckvn) / l_fin
        outs = []
        for hh in range(MLA_HEADS):
            oh = _bdot(o_lat, wuv_ref[hh])
            outs.append(oh[hh:hh + 1, :])
        o_ref[bi] = jnp.broadcast_to(jnp.concatenate(outs, axis=1), (SUBLANES, MLA_HEADS * MLA_V))


def _decode_call(page_table, qn, qr, ckvn, krn, cache_ckv, cache_krope_t, layer, w):
    B = qn.shape[0]
    n_pages = page_table.shape[1]
    page = cache_ckv.shape[2]
    G = _pick_tile(n_pages // 2, DECODE_PAGES)
    NJ = n_pages // G
    BB = _pick_tile(B, DECODE_SEQS)
    assert NJ >= 2 and NJ & (NJ - 1) == 0
    nsteps = B // BB
    tok = lambda a: pl.BlockSpec((BB,) + a.shape[1:], lambda b, pt: (b, 0, 0))
    full = lambda a: _wspec(a, layer)
    hbm = pl.BlockSpec(memory_space=pl.ANY)
    grid_spec = pltpu.PrefetchScalarGridSpec(
        num_scalar_prefetch=1,
        grid=(nsteps,),
        in_specs=[tok(qn), tok(qr), tok(ckvn), tok(krn), full(w['mla_wuk_h']), full(w['mla_wuv_h']), hbm, hbm],
        out_specs=pl.BlockSpec((BB, SUBLANES, MLA_HEADS * MLA_V), lambda b, pt: (b, 0, 0)),
        scratch_shapes=[pltpu.VMEM((BB, DECODE_SLOTS, G, page, MLA_KV_RANK), F32),
                        pltpu.VMEM((BB, DECODE_SLOTS, G, MLA_ROPE, page), F32),
                        pltpu.SemaphoreType.DMA((2, DECODE_SLOTS))],
    )
    return pl.pallas_call(
        functools.partial(_decode_kernel, G=G, NJ=NJ, BB=BB, layer=layer, nsteps=nsteps),
        grid_spec=grid_spec,
        out_shape=jax.ShapeDtypeStruct((B, SUBLANES, MLA_HEADS * MLA_V), F32),
        compiler_params=pltpu.CompilerParams(
            dimension_semantics=("arbitrary",), vmem_limit_bytes=VMEM_LIMIT),
        name="mla_decode",
    )(page_table, qn, qr, ckvn, krn, w['mla_wuk_h'], w['mla_wuv_h'], cache_ckv, cache_krope_t)


def _merge_ffn_kernel(x_ref, oa_ref, ob_ref, oc_ref, ng_ref, wgt_ref, bg_ref, wa_ref, wb_ref, wc_ref, wo_ref,
                      nf_ref, wup_ref, wdn_ref, fin_ref, y_ref, *, final, oc_transposed):
    x = x_ref[...]
    h = _rms(x, ng_ref[...]).astype(BF16)
    gates = _sigmoid(jnp.dot(h, wgt_ref[...], preferred_element_type=F32) + bg_ref[...])
    D = D_MODEL
    oc = oc_ref[...].T if oc_transposed else oc_ref[...]
    merged = (gates[:, 0:D] * _bdot(oa_ref[...], wa_ref[...])
              + gates[:, D:2 * D] * _bdot(ob_ref[...], wb_ref[...])
              + gates[:, 2 * D:3 * D] * _bdot(oc, wc_ref[...]))
    x1 = x + _bdot(merged, wo_ref[...])
    h2 = _rms(x1, nf_ref[...]).astype(BF16)
    up = jnp.dot(h2, wup_ref[...], preferred_element_type=F32)
    g = up[:, :D_FF]
    x2 = x1 + _bdot(g * _sigmoid(g) * up[:, D_FF:], wdn_ref[...])
    y_ref[...] = _rms(x2, fin_ref[...]) if final else x2


def _merge_ffn_call(x, oa, ob, oc, w, l, fin_g, *, tm, final):
    M, D = x.shape
    const = lambda a: _wspec(a, l, pipeline_mode=pl.Buffered(1))
    row = lambda n: pl.BlockSpec((tm, n), lambda i: (i, 0))
    names = ['norm_mix_g', 'w_gate', 'b_gate', 'w_br_a', 'w_br_b', 'w_br_c', 'w_out', 'norm_ffn_g',
             'w_ffn_up', 'w_ffn_down']
    oc_transposed = oc.ndim == 3
    if oc_transposed:
        nt = oc.shape[2] // tm
        oc_spec = pl.BlockSpec((None, oc.shape[1], tm), lambda i: (i // nt, 0, i % nt))
    else:
        oc_spec = row(oc.shape[1])
    return pl.pallas_call(
        functools.partial(_merge_ffn_kernel, final=final, oc_transposed=oc_transposed),
        grid=(M // tm,),
        in_specs=[row(D), row(oa.shape[1]), row(ob.shape[1]), oc_spec]
                 + [const(w[n]) for n in names] + [const(fin_g)],
        out_specs=row(D),
        out_shape=jax.ShapeDtypeStruct((M, D), F32),
        compiler_params=pltpu.CompilerParams(
            dimension_semantics=("arbitrary",), vmem_limit_bytes=VMEM_LIMIT),
        name="merge_ffn",
    )(x, oa, ob, oc, *[w[n] for n in names], fin_g)


def _pad_cols(a, n):
    return jnp.pad(a, [(0, 0)] * (a.ndim - 1) + [(0, n - a.shape[-1])])


def _swap_halves(a, width):
    shp = a.shape
    a = a.reshape(shp[:-1] + (shp[-1] // width, 2, width // 2))
    return a[..., ::-1, :].reshape(shp)


def _pad_groups(a, width, to):
    shp = a.shape
    a = a.reshape(shp[:-1] + (shp[-1] // width, width))
    a = jnp.pad(a, [(0, 0)] * (a.ndim - 1) + [(0, to - width)])
    return a.reshape(shp[:-1] + (-1,))


def _rw_pad(a):
    W = RWKV_W
    z = jnp.zeros(a.shape[:-1] + (LANES - RWKV_DECAY_LORA,), a.dtype)
    return jnp.concatenate([a[..., :3 * W], a[..., 3 * W:3 * W + RWKV_DECAY_LORA], z,
                            a[..., 3 * W + RWKV_DECAY_LORA:3 * W + RWKV_DECAY_LORA + RWKV_AAA_LORA], z,
                            a[..., 3 * W + RWKV_DECAY_LORA + RWKV_AAA_LORA:]], axis=-1)


def _rw_unpad(a):
    W = RWKV_W
    return jnp.concatenate([a[..., :3 * W], a[..., 3 * W:3 * W + RWKV_DECAY_LORA],
                            a[..., 3 * W + LANES:3 * W + LANES + RWKV_AAA_LORA],
                            a[..., 3 * W + 2 * LANES:]], axis=-1)


def _prep_weights(p):
    w_in = p['w_in']
    depth = w_in.shape[0]
    row = lambda a: a.reshape(depth, 1, -1).astype(F32)
    w = {}
    w['norm_mix_g'] = row(p['norm_mix_g'])
    w['norm_ffn_g'] = row(p['norm_ffn_g'])
    ba = w_in[..., _O_BETA:_O_Z]
    w['gdn_w'] = jnp.concatenate([w_in[..., _O_QKV:_O_BETA], w_in[..., _O_Z:_O_RW], _pad_cols(ba, LANES)],
                                 axis=-1).astype(BF16)
    w['gdn_wba_t'] = jnp.swapaxes(ba, 1, 2).astype(BF16)
    w['gdn_conv_w'] = p['gdn_conv_w']
    zero4 = jnp.zeros((depth, GDN_HEADS), F32)
    a_log = jnp.concatenate([zero4, p['gdn_a_log']], axis=1)
    dt_b = jnp.concatenate([zero4, p['gdn_dt_bias']], axis=1)
    w['gdn_prow'] = _pad_cols(jnp.stack([a_log, dt_b], axis=1), LANES)
    w['gdn_pcol'] = jnp.stack([a_log, dt_b], axis=2)
    w['gdn_norm_g'] = row(p['gdn_norm_g'])
    lane = jnp.arange(LANES)[:, None]
    head_of_col = jnp.arange(GDN_VAL_W)[None, :] // GDN_DV
    w['gdn_rep'] = jnp.concatenate([lane == head_of_col, lane == head_of_col + GDN_HEADS], axis=1).astype(BF16)
    w['rwkv_w'] = _rw_pad(w_in[..., _O_RW:_O_CQ]).astype(BF16)
    w['rwkv_mu'] = row(_rw_pad(p['rwkv_mu']))
    w['rwkv_w0'] = row(p['rwkv_w0'])
    w['rwkv_w2'] = jnp.pad(p['rwkv_w2'], ((0, 0), (0, LANES - RWKV_DECAY_LORA), (0, 0))).astype(BF16)
    w['rwkv_a0'] = row(p['rwkv_a0'])
    w['rwkv_a2'] = jnp.pad(p['rwkv_a2'], ((0, 0), (0, LANES - RWKV_AAA_LORA), (0, 0))).astype(BF16)
    w['rwkv_g2'] = p['rwkv_g2'].astype(BF16)
    w['rwkv_k_k'] = row(p['rwkv_k_k'])
    w['rwkv_k_a'] = row(p['rwkv_k_a'])
    w['rwkv_r_k'] = row(p['rwkv_r_k'])
    w['rwkv_ln_g'] = row(p['rwkv_ln_g'])
    w['rwkv_ln_b'] = row(p['rwkv_ln_b'])
    hid = jnp.arange(RWKV_W) // RWKV_HEAD
    w['rwkv_e'] = (hid[:, None] == hid[None, :]).astype(BF16)
    w_kr = w_in[..., _O_KR:_O_GATE]
    w['mla_w'] = jnp.concatenate([w_in[..., _O_CQ:_O_KR], _pad_cols(w_kr, LANES),
                                  _pad_cols(_swap_halves(w_kr, MLA_ROPE), LANES)], axis=-1).astype(BF16)
    w['mla_q_norm_g'] = row(p['mla_q_norm_g'])
    w['mla_kv_norm_g'] = row(p['mla_kv_norm_g'])
    w_uq = p['mla_w_uq']
    wq_n = w_uq[..., :MLA_NOPE].reshape(depth, MLA_Q_RANK, -1)
    wq_r = w_uq[..., MLA_NOPE:].reshape(depth, MLA_Q_RANK, -1)
    w['mla_wq'] = jnp.concatenate([wq_n, _pad_groups(wq_r, MLA_ROPE, LANES),
                                   _pad_groups(_swap_halves(wq_r, MLA_ROPE), MLA_ROPE, LANES)],
                                  axis=-1).astype(BF16)
    w['mla_wuk'] = p['mla_w_uk'].reshape(depth, MLA_KV_RANK, -1).astype(BF16)
    w['mla_wuv_t'] = jnp.swapaxes(p['mla_w_uv'].reshape(depth, MLA_KV_RANK, -1), 1, 2).astype(BF16)
    w['mla_wuk_h'] = jnp.transpose(p['mla_w_uk'], (0, 2, 1, 3)).astype(BF16)
    w['mla_wuv_h'] = jnp.transpose(p['mla_w_uv'], (0, 2, 1, 3)).astype(BF16)
    w['w_gate'] = w_in[..., _O_GATE:].astype(BF16)
    w['b_gate'] = row(p['b_gate'])
    for n in ('w_br_a', 'w_br_b', 'w_br_c', 'w_out', 'w_ffn_up', 'w_ffn_down'):
        w[n] = p[n].astype(BF16)
    return w


def _rope_tables(pos):
    half = MLA_ROPE // 2
    freq = ROPE_THETA ** (-jnp.arange(half, dtype=F32) / half)
    ang = pos.astype(F32)[:, None] * freq
    cos = jnp.cos(ang)
    sin = jnp.sin(ang)
    return (_pad_cols(jnp.concatenate([cos, cos], axis=1), LANES),
            _pad_cols(jnp.concatenate([-sin, sin], axis=1), LANES))


def _pick_tile(T, pref):
    t = min(T, pref)
    while T % t:
        t //= 2
    return t


def kernel(x_prompt, x_sample, cache_ckv, cache_krope, page_table, state_gdn, state_gdn_conv, state_rwkv, state_rwkv_shift, norm_mix_g, norm_ffn_g, norm_final_g, w_in, b_gate, gdn_conv_w, gdn_a_log, gdn_dt_bias, gdn_norm_g, rwkv_mu, rwkv_w0, rwkv_w2, rwkv_a0, rwkv_a2, rwkv_g2, rwkv_k_k, rwkv_k_a, rwkv_r_k, rwkv_ln_g, rwkv_ln_b, mla_q_norm_g, mla_kv_norm_g, mla_w_uq, mla_w_uk, mla_w_uv, w_br_a, w_br_b, w_br_c, w_out, w_ffn_up, w_ffn_down):
    p = dict(norm_mix_g=norm_mix_g, norm_ffn_g=norm_ffn_g, w_in=w_in, b_gate=b_gate, gdn_conv_w=gdn_conv_w,
             gdn_a_log=gdn_a_log, gdn_dt_bias=gdn_dt_bias, gdn_norm_g=gdn_norm_g, rwkv_mu=rwkv_mu,
             rwkv_w0=rwkv_w0, rwkv_w2=rwkv_w2, rwkv_a0=rwkv_a0, rwkv_a2=rwkv_a2, rwkv_g2=rwkv_g2,
             rwkv_k_k=rwkv_k_k, rwkv_k_a=rwkv_k_a, rwkv_r_k=rwkv_r_k, rwkv_ln_g=rwkv_ln_g,
             rwkv_ln_b=rwkv_ln_b, mla_q_norm_g=mla_q_norm_g, mla_kv_norm_g=mla_kv_norm_g,
             mla_w_uq=mla_w_uq, mla_w_uk=mla_w_uk, mla_w_uv=mla_w_uv, w_br_a=w_br_a, w_br_b=w_br_b,
             w_br_c=w_br_c, w_out=w_out, w_ffn_up=w_ffn_up, w_ffn_down=w_ffn_down)
    depth = w_in.shape[0]
    bp, sp, D = x_prompt.shape
    bs, ss, _ = x_sample.shape
    assert ss == 1, "the sample group decodes one new token per sequence"
    past_len = page_table.shape[1] * cache_ckv.shape[2]
    fin_g = norm_final_g.reshape(1, -1).astype(F32)
    cache_krope_t = jnp.swapaxes(cache_krope, 2, 3)

    tp = _pick_tile(sp, MIXER_TILE)
    cp = _pick_tile(tp, MIXER_CHUNK)
    tq = _pick_tile(sp, ROW_TILE)
    tr_p = _pick_tile(sp, 2 * ROW_TILE)
    ts = SUBLANES
    bb_s = _pick_tile(bs, MIXER_SEQS_SAMPLE)
    bb_p = _pick_tile(bp, MIXER_SEQS_PROMPT)
    cos_p, sin_p = _rope_tables(jnp.arange(sp, dtype=jnp.int32))
    cos_s, sin_s = _rope_tables(jnp.full((bs * ts,), past_len, dtype=jnp.int32))
    zeros_p = dict(
        gdn=jnp.zeros((1, bp, GDN_HEADS, GDN_DK, GDN_DV), F32),
        conv=jnp.zeros((1, bp, GDN_CONV - 1, GDN_CONV_CH), F32),
        rwkv=jnp.zeros((1, bp, RWKV_HEADS, RWKV_HEAD, RWKV_HEAD), F32),
        shift=jnp.zeros((1, bp, 1, RW_PAD), F32))
    shift_s = _rw_pad(state_rwkv_shift)[:, :, None]

    w = _prep_weights(p)
    xp = x_prompt
    xs = jnp.pad(x_sample, ((0, 0), (0, ts - ss), (0, 0)))
    new_p = [[] for _ in range(6)]
    new_s = [[] for _ in range(6)]
    mp = bp * sp
    tm_p = tq
    for l in range(depth):
        final = l == depth - 1
        oa, s_g, cbuf = _gdn_call(xp, zeros_p['gdn'], zeros_p['conv'], 0, w, l, Bb=bb_p, Tt=tp, C=cp, t_real=sp)
        ob, s_r, sh = _rwkv_call(xp, zeros_p['rwkv'], zeros_p['shift'], 0, w, l, Bb=bb_p, Tt=tp, C=cp, t_real=sp)
        ckv, kr, q, k, vt = _mla_proj_call(xp.reshape(mp, D), w, l, cos_p, sin_p, tr=tr_p, prompt=True)
        oc_t = _flash_call(q.reshape(bp, sp, -1), k.reshape(bp, sp, -1), vt, tq=tq)
        for lst, arr in zip(new_p, (ckv.reshape(bp, sp, -1), kr.reshape(bp, sp, -1), s_g, cbuf, s_r, sh[:, 0])):
            lst.append(arr)
        xp = _merge_ffn_call(xp.reshape(mp, D), oa.reshape(mp, -1), ob.reshape(mp, -1),
                             oc_t, w, l, fin_g, tm=tm_p, final=final).reshape(bp, sp, D)
        oa, s_g, cbuf = _gdn_call(xs, state_gdn, state_gdn_conv, l, w, l, Bb=bb_s, Tt=ts, C=ts, t_real=ss)
        ob, s_r, sh = _rwkv_call(xs, state_rwkv, shift_s, l, w, l, Bb=bb_s, Tt=ts, C=ts, t_real=ss)
        ckv, kr, qn, qr = _mla_proj_call(xs.reshape(bs * ts, D), w, l, cos_s, sin_s, tr=bs * ts, prompt=False)
        ckv = ckv.reshape(bs, ts, -1)
        kr = kr.reshape(bs, ts, -1)
        oc = _decode_call(page_table, qn.reshape(bs, ts, -1), qr.reshape(bs, ts, -1), ckv, kr,
                          cache_ckv, cache_krope_t, l, w)
        for lst, arr in zip(new_s, (ckv[:, :ss], kr[:, :ss], s_g, cbuf, s_r, sh[:, 0])):
            lst.append(arr)
        xs_real = _merge_ffn_call(xs[:, 0], oa[:, 0], ob[:, 0], oc[:, 0], w, l, fin_g, tm=bs, final=final)
        xs = jnp.pad(xs_real[:, None], ((0, 0), (0, ts - ss), (0, 0)))
    y_prompt = xp
    y_sample = xs[:, :ss]
    outs_p = [jnp.stack(a) for a in new_p]
    outs_s = [jnp.stack(a) for a in new_s]
    outs_p[5] = _rw_unpad(outs_p[5])
    outs_s[5] = _rw_unpad(outs_s[5])
    return (y_prompt, y_sample, *outs_p, *outs_s)
```
